```python
import math
import jax, jax.numpy as jnp
from jax import lax
import numpy as np

D_MODEL = 2048
BATCH = 1
SEQ = 8192
DEPTH = 4
DEC_BATCH = 4
DEC_SEQ = 4096
PAST_LEN = 128

HEAD_DIM = 128
N_SGU_GROUPS = 4
SGU_WIDTH = N_SGU_GROUPS * HEAD_DIM
SGU_CHUNK = 128
N_DIL_HEADS = 8
DIL_WIDTH = N_DIL_HEADS * HEAD_DIM
DIL_PAIRS = ((128, 1), (512, 4), (2048, 16))
N_MEM_HEADS = 4
MEM_WIDTH = N_MEM_HEADS * HEAD_DIM
N_MEM = 256

MIX_WIDTH = SGU_WIDTH + DIL_WIDTH + MEM_WIDTH
IN_WIDTH = 2 * SGU_WIDTH + 3 * DIL_WIDTH + MEM_WIDTH
ROPE_THETA = 500000.0
ROPE_DIM = HEAD_DIM // 4
FFN_HIDDEN = -(-8 * D_MODEL // (3 * 256)) * 256
EPS = 1e-6
NEG_INF = -1e30

kernel_name = 'hybrid_sgu_dilated_memory_encoder'


def rmsnorm(x, g):
    xf = x.astype(jnp.float32)
    y = xf * lax.rsqrt(jnp.mean(xf * xf, axis=-1, keepdims=True) + EPS)
    return (y * g.astype(jnp.float32)).astype(x.dtype)


def rope_partial(t, positions):
    inv = ROPE_THETA ** (-jnp.arange(0, ROPE_DIM, 2, dtype=jnp.float32) / ROPE_DIM)
    ang = positions.astype(jnp.float32)[:, None] * inv[None, :]
    cos = jnp.cos(ang)[None, :, None, :]
    sin = jnp.sin(ang)[None, :, None, :]
    tf = t.astype(jnp.float32)
    x1 = tf[..., :ROPE_DIM // 2]
    x2 = tf[..., ROPE_DIM // 2:ROPE_DIM]
    out = jnp.concatenate([x1 * cos - x2 * sin, x2 * cos + x1 * sin, tf[..., ROPE_DIM:]], axis=-1)
    return out.astype(t.dtype)


def dilated_branch(q, k, v, window, dilation):
    B, S, H, Dh = q.shape
    half = window // (2 * dilation)
    L = S // dilation
    nb = -(-L // half)
    Lp = nb * half

    def residues(t):
        return t.reshape(B, L, dilation, H, Dh).transpose(0, 2, 1, 3, 4)

    qr = jnp.pad(residues(q), ((0, 0), (0, 0), (0, Lp - L), (0, 0), (0, 0)))
    qb = qr.reshape(B, dilation, nb, half, H, Dh)

    def key_blocks(t):
        tp = jnp.pad(residues(t), ((0, 0), (0, 0), (half, Lp - L + half), (0, 0), (0, 0)))
        parts = [tp[:, :, j * half:j * half + Lp].reshape(B, dilation, nb, half, H, Dh) for j in range(3)]
        return jnp.concatenate(parts, axis=3)

    kb = key_blocks(k)
    vb = key_blocks(v)
    s = jnp.einsum('brnqhd,brnkhd->brnhqk', qb, kb).astype(jnp.float32) * (Dh ** -0.5)
    qpos = jnp.arange(nb)[:, None] * half + jnp.arange(half)[None, :]
    kpos = jnp.arange(nb)[:, None] * half - half + jnp.arange(3 * half)[None, :]
    kp = kpos[:, None, :]
    valid = (jnp.abs(kp - qpos[:, :, None]) <= half) & (kp >= 0) & (kp < L)
    s = jnp.where(valid[None, None, :, None, :, :], s, NEG_INF)
    m = jnp.max(s, axis=-1)
    p = jnp.exp(s - m[..., None])
    den = jnp.sum(p, axis=-1)
    num = jnp.einsum('brnhqk,brnkhd->brnqhd', p, vb.astype(jnp.float32))
    num = num.reshape(B, dilation, Lp, H, Dh)[:, :, :L].transpose(0, 2, 1, 3, 4).reshape(B, S, H, Dh)

    def stat(a):
        a = a.transpose(0, 1, 2, 4, 3).reshape(B, dilation, Lp, H)[:, :, :L]
        return a.transpose(0, 2, 1, 3).reshape(B, S, H)

    return num, stat(den), stat(m)


def dilated_attention(q, k, v):
    branches = [dilated_branch(q, k, v, w, d) for (w, d) in DIL_PAIRS]
    m_all = branches[0][2]
    for br in branches[1:]:
        m_all = jnp.maximum(m_all, br[2])
    numer = 0.0
    denom = 0.0
    for num, den, m in branches:
        scale = jnp.exp(m - m_all)
        numer = numer + scale[..., None] * num
        denom = denom + scale * den
    return numer / denom[..., None]


def layer(x, mem, pos, g_mix_norm, w_in, g_sgu, w_spatial, b_spatial, g_mem_norm, w_mem_kv,
          g_group_out, w_out, g_ffn_norm, w_gate_up, w_down):
    B, S, _ = x.shape
    h = rmsnorm(x, g_mix_norm)
    z = h @ w_in
    o_q = 2 * SGU_WIDTH
    o_k = o_q + DIL_WIDTH
    o_v = o_k + DIL_WIDTH
    o_c = o_v + DIL_WIDTH

    za = jax.nn.gelu(z[..., :o_q])
    u = za[..., :SGU_WIDTH]
    vv = rmsnorm(za[..., SGU_WIDTH:].reshape(B, S, N_SGU_GROUPS, HEAD_DIM), g_sgu)
    vc = vv.reshape(B, S // SGU_CHUNK, SGU_CHUNK, N_SGU_GROUPS, HEAD_DIM)
    vs = jnp.einsum('gtp,bnpgc->bntgc', w_spatial, vc) + b_spatial.T[None, None, :, :, None]
    a_out = u * vs.reshape(B, S, SGU_WIDTH)

    qb = rope_partial(z[..., o_q:o_k].reshape(B, S, N_DIL_HEADS, HEAD_DIM), pos)
    kb = rope_partial(z[..., o_k:o_v].reshape(B, S, N_DIL_HEADS, HEAD_DIM), pos)
    vb = z[..., o_v:o_c].reshape(B, S, N_DIL_HEADS, HEAD_DIM)
    b_out = dilated_attention(qb, kb, vb).reshape(B, S, DIL_WIDTH).astype(x.dtype)

    qc = z[..., o_c:].reshape(B, S, N_MEM_HEADS, HEAD_DIM)
    kv = (rmsnorm(mem, g_mem_norm) @ w_mem_kv).reshape(B, N_MEM, 2, N_MEM_HEADS, HEAD_DIM)
    sc = jnp.einsum('bshd,bmhd->bhsm', qc, kv[:, :, 0]).astype(jnp.float32) * (HEAD_DIM ** -0.5)
    pc = jax.nn.softmax(sc, axis=-1)
    c_out = jnp.einsum('bhsm,bmhd->bshd', pc, kv[:, :, 1].astype(jnp.float32))
    c_out = c_out.reshape(B, S, MEM_WIDTH).astype(x.dtype)

    g1 = SGU_WIDTH
    g2 = SGU_WIDTH + DIL_WIDTH
    mix = jnp.concatenate([rmsnorm(a_out, g_group_out[:g1]),
                           rmsnorm(b_out, g_group_out[g1:g2]),
                           rmsnorm(c_out, g_group_out[g2:])], axis=-1)
    x = x + mix @ w_out

    gu = rmsnorm(x, g_ffn_norm) @ w_gate_up
    gate = gu[..., :FFN_HIDDEN]
    up = gu[..., FFN_HIDDEN:]
    x = x + (jax.nn.silu(gate) * up) @ w_down
    return x


def trunk(x, mem, g_mix_norm, w_in, g_sgu, w_spatial, b_spatial, g_mem_norm, w_mem_kv,
          g_group_out, w_out, g_ffn_norm, w_gate_up, w_down, g_final):
    pos = jnp.arange(x.shape[1], dtype=jnp.int32)
    for l in range(DEPTH):
        x = layer(x, mem, pos, g_mix_norm[l], w_in[l], g_sgu[l], w_spatial[l], b_spatial[l],
                  g_mem_norm[l], w_mem_kv[l], g_group_out[l], w_out[l], g_ffn_norm[l],
                  w_gate_up[l], w_down[l])
    return rmsnorm(x, g_final)


def setup_inputs(seed: int = 0) -> dict:
    key = jax.random.key(seed)
    ks = jax.random.split(key, 20)
    f32 = jnp.float32

    def nrm(k, shape, scale):
        return jax.random.normal(k, shape, f32) * scale

    def gain(k, shape):
        return 1.0 + 0.02 * jax.random.normal(k, shape, f32)

    return {
        'x_prompt': nrm(ks[0], (BATCH, SEQ, D_MODEL), 1.0),
        'x_sample': nrm(ks[1], (DEC_BATCH, DEC_SEQ, D_MODEL), 1.0),
        'mem_prompt': nrm(ks[2], (BATCH, N_MEM, D_MODEL), 1.0),
        'mem_sample': nrm(ks[3], (DEC_BATCH, N_MEM, D_MODEL), 1.0),
        'g_mix_norm': gain(ks[4], (DEPTH, D_MODEL)),
        'w_in': nrm(ks[5], (DEPTH, D_MODEL, IN_WIDTH), D_MODEL ** -0.5),
        'g_sgu': gain(ks[6], (DEPTH, N_SGU_GROUPS, HEAD_DIM)),
        'w_spatial': nrm(ks[7], (DEPTH, N_SGU_GROUPS, SGU_CHUNK, SGU_CHUNK), SGU_CHUNK ** -0.5),
        'b_spatial': 1.0 + 0.01 * jax.random.normal(ks[8], (DEPTH, N_SGU_GROUPS, SGU_CHUNK), f32),
        'g_mem_norm': gain(ks[9], (DEPTH, D_MODEL)),
        'w_mem_kv': nrm(ks[10], (DEPTH, D_MODEL, 2 * MEM_WIDTH), D_MODEL ** -0.5),
        'g_group_out': gain(ks[11], (DEPTH, MIX_WIDTH)),
        'w_out': nrm(ks[12], (DEPTH, MIX_WIDTH, D_MODEL), MIX_WIDTH ** -0.5),
        'g_ffn_norm': gain(ks[13], (DEPTH, D_MODEL)),
        'w_gate_up': nrm(ks[14], (DEPTH, D_MODEL, 2 * FFN_HIDDEN), D_MODEL ** -0.5),
        'w_down': nrm(ks[15], (DEPTH, FFN_HIDDEN, D_MODEL), FFN_HIDDEN ** -0.5),
        'g_final': gain(ks[16], (D_MODEL,)),
    }


def reference(x_prompt, x_sample, mem_prompt, mem_sample, g_mix_norm, w_in, g_sgu, w_spatial,
              b_spatial, g_mem_norm, w_mem_kv, g_group_out, w_out, g_ffn_norm, w_gate_up,
              w_down, g_final):
    y_prompt = trunk(x_prompt, mem_prompt, g_mix_norm, w_in, g_sgu, w_spatial, b_spatial,
                     g_mem_norm, w_mem_kv, g_group_out, w_out, g_ffn_norm, w_gate_up, w_down, g_final)
    y_sample = trunk(x_sample, mem_sample, g_mix_norm, w_in, g_sgu, w_spatial, b_spatial,
                     g_mem_norm, w_mem_kv, g_group_out, w_out, g_ffn_norm, w_gate_up, w_down, g_final)
    return (y_prompt, y_sample)
```

```python
import functools
import math

import jax
import jax.numpy as jnp
from jax import lax
from jax.experimental import pallas as pl
from jax.experimental.pallas import tpu as pltpu

F32 = jnp.float32
BF16 = jnp.bfloat16

D_MODEL = 2048
HEAD_DIM = 128
N_SGU_GROUPS = 4
SGU_WIDTH = N_SGU_GROUPS * HEAD_DIM
SGU_CHUNK = 128
N_DIL_HEADS = 8
DIL_WIDTH = N_DIL_HEADS * HEAD_DIM
DILATIONS = (1, 4, 16)
HALF = 64
N_MEM_HEADS = 4
MEM_WIDTH = N_MEM_HEADS * HEAD_DIM
N_MEM = 256
MIX_WIDTH = SGU_WIDTH + DIL_WIDTH + MEM_WIDTH
IN_WIDTH = 2 * SGU_WIDTH + 3 * DIL_WIDTH + MEM_WIDTH
ROPE_THETA = 500000.0
ROPE_DIM = HEAD_DIM // 4
ROPE_HALF = ROPE_DIM // 2
FFN_HIDDEN = 5632
EPS = 1e-6
NEG_INF = -1e30
ATTN_SCALE = HEAD_DIM ** -0.5

V7X_VMEM_BYTES = 64 * 1024 * 1024
VMEM_LIMIT = V7X_VMEM_BYTES - 8 * 1024 * 1024

IN_TILE_N = 512
IN_ROPE_TILES = (2, 6)
IN_Q_TILES_END = 4


def _params(*sem):
  return pltpu.CompilerParams(dimension_semantics=sem, vmem_limit_bytes=VMEM_LIMIT)


def _rms_scale(x):
  return lax.rsqrt(jnp.mean(x * x, axis=-1, keepdims=True) + EPS)


def _norm_matmul_body(x_ref, g_ref, w_ref, o_ref, h_ref):
  @pl.when(pl.program_id(1) == 0)
  def _():
    xf = x_ref[...]
    h_ref[...] = (xf * _rms_scale(xf) * g_ref[...]).astype(BF16)

  return jnp.dot(h_ref[...], w_ref[...], preferred_element_type=F32)


def _norm_matmul_kernel(x_ref, g_ref, w_ref, o_ref, h_ref):
  o_ref[...] = _norm_matmul_body(x_ref, g_ref, w_ref, o_ref, h_ref).astype(o_ref.dtype)


def _in_proj_kernel(x_ref, g_ref, w_ref, c_ref, s1_ref, s2_ref, o_ref, h_ref):
  j = pl.program_id(1)
  acc = _norm_matmul_body(x_ref, g_ref, w_ref, o_ref, h_ref)
  is_rope = (j >= IN_ROPE_TILES[0]) & (j < IN_ROPE_TILES[1])

  @pl.when(is_rope)
  def _():
    scale = jnp.where(j < IN_Q_TILES_END, ATTN_SCALE, 1.0).astype(F32)
    c = c_ref[...] * scale
    s1 = s1_ref[...] * scale
    s2 = s2_ref[...] * scale
    for h in range(IN_TILE_N // HEAD_DIM):
      t = acc[:, h * HEAD_DIM:(h + 1) * HEAD_DIM]
      r = (t * c + pltpu.roll(t, ROPE_HALF, 1) * s1
           + pltpu.roll(t, HEAD_DIM - ROPE_HALF, 1) * s2)
      o_ref[:, h * HEAD_DIM:(h + 1) * HEAD_DIM] = r.astype(o_ref.dtype)

  @pl.when(jnp.logical_not(is_rope))
  def _():
    o_ref[...] = acc.astype(o_ref.dtype)


def _norm_matmul(x, g, w, *, tm, tn, rope=None, name):
  m, k = x.shape
  n = w.shape[1]
  in_specs = [
      pl.BlockSpec((tm, k), lambda i, j: (i, 0)),
      pl.BlockSpec((1, k), lambda i, j: (0, 0)),
      pl.BlockSpec((k, tn), lambda i, j: (0, j)),
  ]
  args = [x, g.reshape(1, k), w]
  if rope is None:
    body = _norm_matmul_kernel
  else:
    body = _in_proj_kernel
    in_specs += [pl.BlockSpec((tm, HEAD_DIM), lambda i, j: (i, 0))] * 3
    args += list(rope)
  return pl.pallas_call(
      body,
      grid=(m // tm, n // tn),
      in_specs=in_specs,
      out_specs=pl.BlockSpec((tm, tn), lambda i, j: (i, j)),
      out_shape=jax.ShapeDtypeStruct((m, n), BF16),
      scratch_shapes=[pltpu.VMEM((tm, k), BF16)],
      compiler_params=_params("parallel", "arbitrary"),
      name=name,
  )(*args)


def _sgu_kernel(zu_ref, zv_ref, gs_ref, w_ref, b_ref, go_ref, o_ref):
  rows = zu_ref.shape[0]
  for c in range(rows // SGU_CHUNK):
    rs = slice(c * SGU_CHUNK, (c + 1) * SGU_CHUNK)
    outs = []
    ssq = jnp.zeros((SGU_CHUNK, 1), F32)
    for g in range(N_SGU_GROUPS):
      cs = slice(g * HEAD_DIM, (g + 1) * HEAD_DIM)
      u = jax.nn.gelu(zu_ref[rs, cs].astype(F32))
      v = jax.nn.gelu(zv_ref[rs, cs].astype(F32))
      vv = v * _rms_scale(v) * gs_ref[:, cs]
      vs = jnp.dot(w_ref[g], vv.astype(BF16), preferred_element_type=F32) + b_ref[g]
      a = u * vs
      ssq = ssq + jnp.sum(a * a, axis=-1, keepdims=True)
      outs.append(a)
    scale = lax.rsqrt(ssq * (1.0 / SGU_WIDTH) + EPS)
    for g in range(N_SGU_GROUPS):
      cs = slice(g * HEAD_DIM, (g + 1) * HEAD_DIM)
      o_ref[rs, cs] = (outs[g] * scale * go_ref[:, cs]).astype(o_ref.dtype)


def _sgu(z, g_sgu, w_sp, b_sp, g_out, *, tb):
  tok = z.shape[0]
  return pl.pallas_call(
      _sgu_kernel,
      grid=(tok // tb,),
      in_specs=[
          pl.BlockSpec((tb, SGU_WIDTH), lambda i: (i, 0)),
          pl.BlockSpec((tb, SGU_WIDTH), lambda i: (i, 1)),
          pl.BlockSpec((1, SGU_WIDTH), lambda i: (0, 0)),
          pl.BlockSpec((N_SGU_GROUPS, SGU_CHUNK, SGU_CHUNK), lambda i: (0, 0, 0)),
          pl.BlockSpec((N_SGU_GROUPS, SGU_CHUNK, HEAD_DIM), lambda i: (0, 0, 0)),
          pl.BlockSpec((1, SGU_WIDTH), lambda i: (0, 0)),
      ],
      out_specs=pl.BlockSpec((tb, SGU_WIDTH), lambda i: (i, 0)),
      out_shape=jax.ShapeDtypeStruct((tok, SGU_WIDTH), BF16),
      compiler_params=_params("parallel"),
      name="sgu",
  )(z, z, g_sgu, w_sp, b_sp, g_out)


def _mem_attn_kernel(q_ref, kv_ref, go_ref, o_ref):
  outs = []
  ssq = jnp.zeros((q_ref.shape[0], 1), F32)
  for h in range(N_MEM_HEADS):
    cs = slice(h * HEAD_DIM, (h + 1) * HEAD_DIM)
    q = q_ref[:, cs]
    k = kv_ref[:, cs]
    v = kv_ref[:, MEM_WIDTH + h * HEAD_DIM:MEM_WIDTH + (h + 1) * HEAD_DIM]
    s = lax.dot_general(q, k, (((1,), (1,)), ((), ())), preferred_element_type=F32) * ATTN_SCALE
    m = jnp.max(s, axis=-1, keepdims=True)
    p = jnp.exp(s - m)
    den = jnp.sum(p, axis=-1, keepdims=True)
    o = jnp.dot(p.astype(BF16), v, preferred_element_type=F32) / den
    ssq = ssq + jnp.sum(o * o, axis=-1, keepdims=True)
    outs.append(o)
  scale = lax.rsqrt(ssq * (1.0 / MEM_WIDTH) + EPS)
  for h in range(N_MEM_HEADS):
    cs = slice(h * HEAD_DIM, (h + 1) * HEAD_DIM)
    o_ref[:, cs] = (outs[h] * scale * go_ref[:, cs]).astype(o_ref.dtype)


def _mem_attn(z, kv, g_out, *, tq, rows_per_mem, first_sample_tile):
  tok = z.shape[0]
  tiles_per_mem = rows_per_mem // tq

  def kv_map(i):
    return (jnp.maximum(i // tiles_per_mem - first_sample_tile + 1, 0), 0, 0)

  return pl.pallas_call(
      _mem_attn_kernel,
      grid=(tok // tq,),
      in_specs=[
          pl.BlockSpec((tq, MEM_WIDTH), lambda i: (i, IN_WIDTH // MEM_WIDTH - 1)),
          pl.BlockSpec((None, N_MEM, 2 * MEM_WIDTH), kv_map),
          pl.BlockSpec((1, MEM_WIDTH), lambda i: (0, 0)),
      ],
      out_specs=pl.BlockSpec((tq, MEM_WIDTH), lambda i: (i, 0)),
      out_shape=jax.ShapeDtypeStruct((tok, MEM_WIDTH), BF16),
      compiler_params=_params("parallel"),
      name="mem_attn",
  )(z, kv, g_out)


DIL_SUB = 2 * HALF
DIL_KEYS = DIL_SUB + 2 * HALF


def _dil_kernel(q_ref, kp_ref, kc_ref, kn_ref, vp_ref, vc_ref, vn_ref, o_ref, lse_ref,
                kbuf, vbuf, *, class_len_prompt, class_len_sample, prompt_rows):
  bq = q_ref.shape[0]
  row0 = pl.program_id(0) * bq
  in_prompt = row0 < prompt_rows
  clen = jnp.where(in_prompt, class_len_prompt, class_len_sample)
  pos = jnp.where(in_prompt, row0, row0 - prompt_rows) & (clen - 1)
  first = pos == 0
  last = pos + bq == clen

  kbuf[0:HALF, :] = kp_ref[...]
  kbuf[HALF:HALF + bq, :] = kc_ref[...]
  kbuf[HALF + bq:, :] = kn_ref[...]
  vbuf[0:HALF, :] = vp_ref[...]
  vbuf[HALF:HALF + bq, :] = vc_ref[...]
  vbuf[HALF + bq:, :] = vn_ref[...]

  r = lax.broadcasted_iota(jnp.int32, (DIL_SUB, DIL_KEYS), 0)
  c = lax.broadcasted_iota(jnp.int32, (DIL_SUB, DIL_KEYS), 1)
  band = jnp.where((c >= r) & (c <= r + 2 * HALF), 0.0, NEG_INF).astype(F32)
  lo = jnp.where(c < HALF, jnp.where(first, NEG_INF, 0.0), 0.0).astype(F32)
  hi = jnp.where(c >= DIL_KEYS - HALF, jnp.where(last, NEG_INF, 0.0), 0.0).astype(F32)
  nsub = bq // DIL_SUB
  biases = []
  for j in range(nsub):
    b = band
    if j == 0:
      b = b + lo
    if j == nsub - 1:
      b = b + hi
    biases.append(b)

  for h in range(N_DIL_HEADS):
    cs = slice(h * HEAD_DIM, (h + 1) * HEAD_DIM)
    for j in range(nsub):
      rs = slice(j * DIL_SUB, (j + 1) * DIL_SUB)
      ks = slice(j * DIL_SUB, j * DIL_SUB + DIL_KEYS)
      s = lax.dot_general(q_ref[rs, cs], kbuf[ks, cs], (((1,), (1,)), ((), ())),
                          preferred_element_type=F32) + biases[j]
      m = jnp.max(s, axis=-1, keepdims=True)
      p = jnp.exp(s - m)
      den = jnp.sum(p, axis=-1, keepdims=True)
      num = jnp.dot(p.astype(BF16), vbuf[ks, cs], preferred_element_type=F32)
      o_ref[rs, cs] = (num / den).astype(o_ref.dtype)
      lse_ref[rs, h:h + 1] = m + jnp.log(den)


def _dilated_branch(src, col0, *, d, bq, prompt_rows, s_prompt, s_sample):
  tok = src.shape[0]
  hb = bq // HALF
  n_half_blocks = tok // HALF
  body = functools.partial(
      _dil_kernel, class_len_prompt=s_prompt // d, class_len_sample=s_sample // d,
      prompt_rows=prompt_rows)

  def cur(c):
    return pl.BlockSpec((bq, DIL_WIDTH), lambda i: (i, c))

  def prev(c):
    return pl.BlockSpec((HALF, DIL_WIDTH), lambda i: (jnp.maximum(i * hb - 1, 0), c))

  def nxt(c):
    return pl.BlockSpec((HALF, DIL_WIDTH),
                        lambda i: (jnp.minimum((i + 1) * hb, n_half_blocks - 1), c))

  return pl.pallas_call(
      body,
      grid=(tok // bq,),
      in_specs=[cur(col0), prev(col0 + 1), cur(col0 + 1), nxt(col0 + 1),
                prev(col0 + 2), cur(col0 + 2), nxt(col0 + 2)],
      out_specs=[pl.BlockSpec((bq, DIL_WIDTH), lambda i: (i, 0)),
                 pl.BlockSpec((bq, N_DIL_HEADS), lambda i: (i, 0))],
      out_shape=[jax.ShapeDtypeStruct((tok, DIL_WIDTH), BF16),
                 jax.ShapeDtypeStruct((tok, N_DIL_HEADS), F32)],
      scratch_shapes=[pltpu.VMEM((bq + 2 * HALF, DIL_WIDTH), BF16),
                      pltpu.VMEM((bq + 2 * HALF, DIL_WIDTH), BF16)],
      compiler_params=_params("parallel"),
      name=f"dilated_d{d}",
  )(*([src] * 7))


def _out_proj_kernel(a_ref, o1_ref, o2_ref, o3_ref, l1_ref, l2_ref, l3_ref, c_ref, gb_ref,
                     w_ref, x_ref, y_ref, mix_ref):
  @pl.when(pl.program_id(1) == 0)
  def _():
    mix_ref[:, 0:SGU_WIDTH] = a_ref[...]
    mix_ref[:, SGU_WIDTH + DIL_WIDTH:] = c_ref[...]
    outs = []
    ssq = jnp.zeros((a_ref.shape[0], 1), F32)
    for h in range(N_DIL_HEADS):
      cs = slice(h * HEAD_DIM, (h + 1) * HEAD_DIM)
      l1 = l1_ref[:, h:h + 1]
      l2 = l2_ref[:, h:h + 1]
      l3 = l3_ref[:, h:h + 1]
      m = jnp.maximum(jnp.maximum(l1, l2), l3)
      w1 = jnp.exp(l1 - m)
      w2 = jnp.exp(l2 - m)
      w3 = jnp.exp(l3 - m)
      b = (w1 * o1_ref[:, cs].astype(F32) + w2 * o2_ref[:, cs].astype(F32)
           + w3 * o3_ref[:, cs].astype(F32)) / (w1 + w2 + w3)
      ssq = ssq + jnp.sum(b * b, axis=-1, keepdims=True)
      outs.append(b)
    scale = lax.rsqrt(ssq * (1.0 / DIL_WIDTH) + EPS)
    for h in range(N_DIL_HEADS):
      cs = slice(h * HEAD_DIM, (h + 1) * HEAD_DIM)
      mix_ref[:, SGU_WIDTH + h * HEAD_DIM:SGU_WIDTH + (h + 1) * HEAD_DIM] = (
          outs[h] * scale * gb_ref[:, cs]).astype(mix_ref.dtype)

  y_ref[...] = x_ref[...] + jnp.dot(mix_ref[...], w_ref[...], preferred_element_type=F32)


def _out_proj(a_n, o_list, lse_list, c_n, g_b, w_out, x, *, tm, tn):
  tok = x.shape[0]
  row = lambda w: pl.BlockSpec((tm, w), lambda i, j: (i, 0))
  return pl.pallas_call(
      _out_proj_kernel,
      grid=(tok // tm, D_MODEL // tn),
      in_specs=[row(SGU_WIDTH), row(DIL_WIDTH), row(DIL_WIDTH), row(DIL_WIDTH),
                row(N_DIL_HEADS), row(N_DIL_HEADS), row(N_DIL_HEADS), row(MEM_WIDTH),
                pl.BlockSpec((1, DIL_WIDTH), lambda i, j: (0, 0)),
                pl.BlockSpec((MIX_WIDTH, tn), lambda i, j: (0, j)),
                pl.BlockSpec((tm, tn), lambda i, j: (i, j))],
      out_specs=pl.BlockSpec((tm, tn), lambda i, j: (i, j)),
      out_shape=jax.ShapeDtypeStruct((tok, D_MODEL), F32),
      scratch_shapes=[pltpu.VMEM((tm, MIX_WIDTH), BF16)],
      compiler_params=_params("parallel", "arbitrary"),
      name="out_proj",
  )(a_n, *o_list, *lse_list, c_n, g_b, w_out, x)


def _ffn_kernel(x_ref, g_ref, wg_ref, wu_ref, wd_ref, gf_ref, y_ref, h_ref, *, final_norm):
  k = pl.program_id(1)

  @pl.when(k == 0)
  def _():
    xf = x_ref[...]
    h_ref[...] = (xf * _rms_scale(xf) * g_ref[...]).astype(BF16)
    y_ref[...] = xf

  h = h_ref[...]
  gate = jnp.dot(h, wg_ref[...], preferred_element_type=F32)
  up = jnp.dot(h, wu_ref[...], preferred_element_type=F32)
  act = (jax.nn.silu(gate) * up).astype(BF16)
  y_ref[...] += jnp.dot(act, wd_ref[...], preferred_element_type=F32)

  if final_norm:
    @pl.when(k == pl.num_programs(1) - 1)
    def _():
      y = y_ref[...]
      y_ref[...] = y * _rms_scale(y) * gf_ref[...]


def _ffn(x, g, w_gu, w_down, g_final, *, tm, th, final_norm):
  tok = x.shape[0]
  nk = FFN_HIDDEN // th
  return pl.pallas_call(
      functools.partial(_ffn_kernel, final_norm=final_norm),
      grid=(tok // tm, nk),
      in_specs=[
          pl.BlockSpec((tm, D_MODEL), lambda i, k: (i, 0)),
          pl.BlockSpec((1, D_MODEL), lambda i, k: (0, 0)),
          pl.BlockSpec((D_MODEL, th), lambda i, k: (0, k)),
          pl.BlockSpec((D_MODEL, th), lambda i, k: (0, nk + k)),
          pl.BlockSpec((th, D_MODEL), lambda i, k: (k, 0)),
          pl.BlockSpec((1, D_MODEL), lambda i, k: (0, 0)),
      ],
      out_specs=pl.BlockSpec((tm, D_MODEL), lambda i, k: (i, 0)),
      out_shape=jax.ShapeDtypeStruct((tok, D_MODEL), F32),
      scratch_shapes=[pltpu.VMEM((tm, D_MODEL), BF16)],
      compiler_params=_params("parallel", "arbitrary"),
      name="ffn",
  )(x, g, w_gu, w_gu, w_down, g_final)


def _to_classes(a, d, prompt_rows, n_sample, s_sample):
  if d == 1:
    return a
  w = a.shape[1]
  p = a[:prompt_rows].reshape(prompt_rows // d, d, w).transpose(1, 0, 2).reshape(prompt_rows, w)
  s = a[prompt_rows:].reshape(n_sample, s_sample // d, d, w).transpose(0, 2, 1, 3)
  return jnp.concatenate([p, s.reshape(n_sample * s_sample, w)], axis=0)


def _from_classes(a, d, prompt_rows, n_sample, s_sample):
  if d == 1:
    return a
  w = a.shape[1]
  p = a[:prompt_rows].reshape(d, prompt_rows // d, w).transpose(1, 0, 2).reshape(prompt_rows, w)
  s = a[prompt_rows:].reshape(n_sample, d, s_sample // d, w).transpose(0, 2, 1, 3)
  return jnp.concatenate([p, s.reshape(n_sample * s_sample, w)], axis=0)


def _rope_tables(positions):
  inv = ROPE_THETA ** (-jnp.arange(0, ROPE_DIM, 2, dtype=F32) / ROPE_DIM)
  ang = positions.astype(F32)[:, None] * inv[None, :]
  cos, sin = jnp.cos(ang), jnp.sin(ang)
  n = positions.shape[0]
  rest = HEAD_DIM - ROPE_DIM
  c = jnp.concatenate([cos, cos, jnp.ones((n, rest), F32)], axis=1)
  s1 = jnp.concatenate([jnp.zeros((n, ROPE_HALF), F32), sin, jnp.zeros((n, rest), F32)], axis=1)
  s2 = jnp.concatenate([-sin, jnp.zeros((n, HEAD_DIM - ROPE_HALF), F32)], axis=1)
  return c, s1, s2


def kernel(x_prompt, x_sample, mem_prompt, mem_sample, g_mix_norm, w_in, g_sgu, w_spatial,
           b_spatial, g_mem_norm, w_mem_kv, g_group_out, w_out, g_ffn_norm, w_gate_up,
           w_down, g_final):
  n_prompt, s_prompt, _ = x_prompt.shape
  n_sample, s_sample, _ = x_sample.shape
  assert n_prompt == 1 and s_prompt % s_sample == 0
  prompt_rows = n_prompt * s_prompt
  depth = w_in.shape[0]

  x = jnp.concatenate([x_prompt.reshape(prompt_rows, D_MODEL),
                       x_sample.reshape(n_sample * s_sample, D_MODEL)], axis=0)
  mem = jnp.concatenate([mem_prompt, mem_sample], axis=0)
  n_mems = mem.shape[0]
  mem = mem.reshape(n_mems * N_MEM, D_MODEL)
  positions = jnp.concatenate([jnp.arange(s_prompt, dtype=jnp.int32)] * n_prompt
                              + [jnp.arange(s_sample, dtype=jnp.int32)] * n_sample)
  rope = _rope_tables(positions)
  layout = dict(prompt_rows=prompt_rows, n_sample=n_sample, s_sample=s_sample)
  bq = min(256, s_sample // DILATIONS[-1])

  for l in range(depth):
    z = _norm_matmul(x, g_mix_norm[l], w_in[l].astype(BF16), tm=512, tn=IN_TILE_N, rope=rope,
                     name="in_proj")
    kv = _norm_matmul(mem, g_mem_norm[l], w_mem_kv[l].astype(BF16), tm=N_MEM, tn=MEM_WIDTH,
                      name="mem_kv").reshape(n_mems, N_MEM, 2 * MEM_WIDTH)
    g_out = g_group_out[l].reshape(1, MIX_WIDTH)
    a_n = _sgu(z, g_sgu[l].reshape(1, SGU_WIDTH), w_spatial[l].astype(BF16),
               jnp.broadcast_to(b_spatial[l][:, :, None], (N_SGU_GROUPS, SGU_CHUNK, HEAD_DIM)),
               g_out[:, :SGU_WIDTH], tb=512)
    c_n = _mem_attn(z, kv, g_out[:, SGU_WIDTH + DIL_WIDTH:], tq=512, rows_per_mem=s_sample,
                    first_sample_tile=prompt_rows // s_sample)
    o_list, lse_list = [], []
    for d in DILATIONS:
      if d == 1:
        src, col0 = z, 2 * SGU_WIDTH // DIL_WIDTH
      else:
        src = _to_classes(z[:, 2 * SGU_WIDTH:2 * SGU_WIDTH + 3 * DIL_WIDTH], d, **layout)
        col0 = 0
      o, lse = _dilated_branch(src, col0, d=d, bq=bq, prompt_rows=prompt_rows,
                               s_prompt=s_prompt, s_sample=s_sample)
      o_list.append(_from_classes(o, d, **layout))
      lse_list.append(_from_classes(lse, d, **layout))
    x = _out_proj(a_n, o_list, lse_list, c_n, g_out[:, SGU_WIDTH:SGU_WIDTH + DIL_WIDTH],
                  w_out[l].astype(BF16), x, tm=512, tn=512)
    x = _ffn(x, g_ffn_norm[l].reshape(1, D_MODEL), w_gate_up[l].astype(BF16),
             w_down[l].astype(BF16), g_final.reshape(1, D_MODEL), tm=512, th=512,
             final_norm=(l == depth - 1))

  y_prompt = x[:prompt_rows].reshape(n_prompt, s_prompt, D_MODEL)
  y_sample = x[prompt_rows:].reshape(n_sample, s_sample, D_MODEL)
  return (y_prompt, y_sample)
```

```python
import functools

import jax
import jax.numpy as jnp
from jax import lax
from jax.experimental import pallas as pl
from jax.experimental.pallas import tpu as pltpu

F32 = jnp.float32
BF16 = jnp.bfloat16

D_MODEL = 2048
HEAD_DIM = 128
N_SGU_GROUPS = 4
SGU_WIDTH = N_SGU_GROUPS * HEAD_DIM
SGU_CHUNK = 128
N_DIL_HEADS = 8
DIL_WIDTH = N_DIL_HEADS * HEAD_DIM
DILATIONS = (1, 4, 16)
HALF = 64
N_MEM_HEADS = 4
MEM_WIDTH = N_MEM_HEADS * HEAD_DIM
N_MEM = 256
MIX_WIDTH = SGU_WIDTH + DIL_WIDTH + MEM_WIDTH
IN_WIDTH = 2 * SGU_WIDTH + 3 * DIL_WIDTH + MEM_WIDTH
ROPE_THETA = 500000.0
ROPE_DIM = HEAD_DIM // 4
ROPE_HALF = ROPE_DIM // 2
FFN_HIDDEN = 5632
EPS = 1e-6
NEG_INF = -1e30
ATTN_SCALE = HEAD_DIM ** -0.5

V7X_VMEM_BYTES = 64 * 1024 * 1024
VMEM_LIMIT = V7X_VMEM_BYTES - 8 * 1024 * 1024

IN_TILE_N = 512
IN_TILE_HEADS = IN_TILE_N // HEAD_DIM
IN_QKV_TILES = (2, 8)
IN_ROPE_END = 6
IN_Q_END = 4
QKV_WIDTH = 3 * DIL_WIDTH
D4, D16 = DILATIONS[1], DILATIONS[2]
D16_PER_D4 = D16 // D4

TM_IN_PROJ = 1024
TM_OUT_PROJ = 1024
TN_OUT_PROJ = 1024
TM_FFN = 1024
TH_FFN = 512
TM_MERGE = 512
TB_SGU = 512
TQ_MEM = 512
BQ_DIL = 256


def _params(*sem):
  return pltpu.CompilerParams(dimension_semantics=sem, vmem_limit_bytes=VMEM_LIMIT)


def _rms_scale(x):
  return lax.rsqrt(jnp.mean(x * x, axis=-1, keepdims=True) + EPS)


def _norm_matmul_body(x_ref, g_ref, w_ref, h_ref):
  @pl.when(pl.program_id(1) == 0)
  def _():
    xf = x_ref[...]
    h_ref[...] = (xf * _rms_scale(xf) * g_ref[...]).astype(BF16)

  return jnp.dot(h_ref[...], w_ref[...], preferred_element_type=F32)


def _norm_matmul_kernel(x_ref, g_ref, w_ref, o_ref, h_ref):
  o_ref[...] = _norm_matmul_body(x_ref, g_ref, w_ref, h_ref).astype(o_ref.dtype)


def _norm_matmul(x, g, w, *, tm, tn, name):
  m, k = x.shape
  n = w.shape[1]
  return pl.pallas_call(
      _norm_matmul_kernel,
      grid=(m // tm, n // tn),
      in_specs=[pl.BlockSpec((tm, k), lambda i, j: (i, 0)),
                pl.BlockSpec((1, k), lambda i, j: (0, 0)),
                pl.BlockSpec((k, tn), lambda i, j: (0, j))],
      out_specs=pl.BlockSpec((tm, tn), lambda i, j: (i, j)),
      out_shape=jax.ShapeDtypeStruct((m, n), BF16),
      scratch_shapes=[pltpu.VMEM((tm, k), BF16)],
      compiler_params=_params("parallel", "arbitrary"),
      name=name,
  )(x, g.reshape(1, k), w)


def _in_proj_kernel(x_ref, g_ref, w_ref, c_ref, s1_ref, s2_ref, z_ref, c4_ref, c16_ref,
                    h_ref, slab, slab4):
  j = pl.program_id(1)
  acc = _norm_matmul_body(x_ref, g_ref, w_ref, h_ref)
  tm = acc.shape[0]
  is_qkv = (j >= IN_QKV_TILES[0]) & (j < IN_QKV_TILES[1])

  @pl.when(is_qkv)
  def _():
    is_rope = j < IN_ROPE_END
    scale = jnp.where(j < IN_Q_END, ATTN_SCALE, 1.0).astype(F32)
    c = jnp.where(is_rope, c_ref[...] * scale, 1.0)
    s1 = jnp.where(is_rope, s1_ref[...] * scale, 0.0)
    s2 = jnp.where(is_rope, s2_ref[...] * scale, 0.0)
    for h in range(IN_TILE_HEADS):
      hs = slice(h * HEAD_DIM, (h + 1) * HEAD_DIM)
      t = acc[:, hs]
      t = t * c + pltpu.roll(t, ROPE_HALF, 1) * s1 + pltpu.roll(t, HEAD_DIM - ROPE_HALF, 1) * s2
      z_ref[:, hs] = t.astype(z_ref.dtype)
      slab[h] = t
      for rho in range(D4):
        t4 = slab.at[h][pl.ds(rho, tm // D4, stride=D4), :]
        c4_ref[rho, :, hs] = t4.astype(c4_ref.dtype)
        slab4[h, rho] = t4
        for q in range(D16_PER_D4):
          t16 = slab4.at[h, rho][pl.ds(q, tm // D16, stride=D16_PER_D4), :]
          c16_ref[D4 * q + rho, :, hs] = t16.astype(c16_ref.dtype)

  @pl.when(jnp.logical_not(is_qkv))
  def _():
    z_ref[...] = acc.astype(z_ref.dtype)


def _in_proj(x, g, w, rope, *, tm):
  tok, k = x.shape
  tn = IN_TILE_N
  n_qkv_tiles = QKV_WIDTH // tn

  def qkv_tile(j):
    return jnp.clip(j - IN_QKV_TILES[0], 0, n_qkv_tiles - 1)

  return pl.pallas_call(
      _in_proj_kernel,
      grid=(tok // tm, IN_WIDTH // tn),
      in_specs=[pl.BlockSpec((tm, k), lambda i, j: (i, 0)),
                pl.BlockSpec((1, k), lambda i, j: (0, 0)),
                pl.BlockSpec((k, tn), lambda i, j: (0, j))]
      + [pl.BlockSpec((tm, HEAD_DIM), lambda i, j: (i, 0))] * 3,
      out_specs=[pl.BlockSpec((tm, tn), lambda i, j: (i, j)),
                 pl.BlockSpec((D4, tm // D4, tn), lambda i, j: (0, i, qkv_tile(j))),
                 pl.BlockSpec((D16, tm // D16, tn), lambda i, j: (0, i, qkv_tile(j)))],
      out_shape=[jax.ShapeDtypeStruct((tok, IN_WIDTH), BF16),
                 jax.ShapeDtypeStruct((D4, tok // D4, QKV_WIDTH), BF16),
                 jax.ShapeDtypeStruct((D16, tok // D16, QKV_WIDTH), BF16)],
      scratch_shapes=[pltpu.VMEM((tm, k), BF16),
                      pltpu.VMEM((IN_TILE_HEADS, tm, HEAD_DIM), F32),
                      pltpu.VMEM((IN_TILE_HEADS, D4, tm // D4, HEAD_DIM), F32)],
      compiler_params=_params("parallel", "arbitrary"),
      name="in_proj",
  )(x, g.reshape(1, k), w, *rope)


def _sgu_kernel(zu_ref, zv_ref, gs_ref, w_ref, b_ref, go_ref, o_ref):
  rows = zu_ref.shape[0]
  for c in range(rows // SGU_CHUNK):
    rs = slice(c * SGU_CHUNK, (c + 1) * SGU_CHUNK)
    outs = []
    ssq = jnp.zeros((SGU_CHUNK, 1), F32)
    for g in range(N_SGU_GROUPS):
      cs = slice(g * HEAD_DIM, (g + 1) * HEAD_DIM)
      u = jax.nn.gelu(zu_ref[rs, cs].astype(F32))
      v = jax.nn.gelu(zv_ref[rs, cs].astype(F32))
      vv = v * _rms_scale(v) * gs_ref[:, cs]
      vs = jnp.dot(w_ref[g], vv.astype(BF16), preferred_element_type=F32) + b_ref[g]
      a = u * vs
      ssq = ssq + jnp.sum(a * a, axis=-1, keepdims=True)
      outs.append(a)
    scale = lax.rsqrt(ssq * (1.0 / SGU_WIDTH) + EPS)
    for g in range(N_SGU_GROUPS):
      cs = slice(g * HEAD_DIM, (g + 1) * HEAD_DIM)
      o_ref[rs, cs] = (outs[g] * scale * go_ref[:, cs]).astype(o_ref.dtype)


def _sgu(z, g_sgu, w_sp, b_sp, g_out, *, tb):
  tok = z.shape[0]
  return pl.pallas_call(
      _sgu_kernel,
      grid=(tok // tb,),
      in_specs=[
          pl.BlockSpec((tb, SGU_WIDTH), lambda i: (i, 0)),
          pl.BlockSpec((tb, SGU_WIDTH), lambda i: (i, 1)),
          pl.BlockSpec((1, SGU_WIDTH), lambda i: (0, 0)),
          pl.BlockSpec((N_SGU_GROUPS, SGU_CHUNK, SGU_CHUNK), lambda i: (0, 0, 0)),
          pl.BlockSpec((N_SGU_GROUPS, SGU_CHUNK, HEAD_DIM), lambda i: (0, 0, 0)),
          pl.BlockSpec((1, SGU_WIDTH), lambda i: (0, 0)),
      ],
      out_specs=pl.BlockSpec((tb, SGU_WIDTH), lambda i: (i, 0)),
      out_shape=jax.ShapeDtypeStruct((tok, SGU_WIDTH), BF16),
      compiler_params=_params("parallel"),
      name="sgu",
  )(z, z, g_sgu, w_sp, b_sp, g_out)


def _mem_attn_kernel(q_ref, kv_ref, go_ref, o_ref):
  outs = []
  ssq = jnp.zeros((q_ref.shape[0], 1), F32)
  for h in range(N_MEM_HEADS):
    cs = slice(h * HEAD_DIM, (h + 1) * HEAD_DIM)
    q = q_ref[:, cs]
    k = kv_ref[:, cs]
    v = kv_ref[:, MEM_WIDTH + h * HEAD_DIM:MEM_WIDTH + (h + 1) * HEAD_DIM]
    s = lax.dot_general(q, k, (((1,), (1,)), ((), ())), preferred_element_type=F32) * ATTN_SCALE
    m = jnp.max(s, axis=-1, keepdims=True)
    p = jnp.exp(s - m)
    den = jnp.sum(p, axis=-1, keepdims=True)
    o = jnp.dot(p.astype(BF16), v, preferred_element_type=F32) / den
    ssq = ssq + jnp.sum(o * o, axis=-1, keepdims=True)
    outs.append(o)
  scale = lax.rsqrt(ssq * (1.0 / MEM_WIDTH) + EPS)
  for h in range(N_MEM_HEADS):
    cs = slice(h * HEAD_DIM, (h + 1) * HEAD_DIM)
    o_ref[:, cs] = (outs[h] * scale * go_ref[:, cs]).astype(o_ref.dtype)


def _mem_attn(z, kv, g_out, *, tq, rows_per_mem, first_sample_tile):
  tok = z.shape[0]
  tiles_per_mem = rows_per_mem // tq

  def kv_map(i):
    return (jnp.maximum(i // tiles_per_mem - first_sample_tile + 1, 0), 0, 0)

  return pl.pallas_call(
      _mem_attn_kernel,
      grid=(tok // tq,),
      in_specs=[
          pl.BlockSpec((tq, MEM_WIDTH), lambda i: (i, IN_WIDTH // MEM_WIDTH - 1)),
          pl.BlockSpec((None, N_MEM, 2 * MEM_WIDTH), kv_map),
          pl.BlockSpec((1, MEM_WIDTH), lambda i: (0, 0)),
      ],
      out_specs=pl.BlockSpec((tq, MEM_WIDTH), lambda i: (i, 0)),
      out_shape=jax.ShapeDtypeStruct((tok, MEM_WIDTH), BF16),
      compiler_params=_params("parallel"),
      name="mem_attn",
  )(z, kv, g_out)


DIL_SUB = 2 * HALF
DIL_KEYS = DIL_SUB + 2 * HALF


def _dil_kernel(q_ref, kp_ref, kc_ref, kn_ref, vp_ref, vc_ref, vn_ref, o_ref, lse_ref,
                kbuf, vbuf, *, class_len_prompt, class_len_sample, prompt_rows):
  bq = q_ref.shape[0]
  row0 = pl.program_id(1) * bq
  in_prompt = row0 < prompt_rows
  clen = jnp.where(in_prompt, class_len_prompt, class_len_sample)
  pos = jnp.where(in_prompt, row0, row0 - prompt_rows) & (clen - 1)
  first = pos == 0
  last = pos + bq == clen

  kbuf[0:HALF, :] = kp_ref[...]
  kbuf[HALF:HALF + bq, :] = kc_ref[...]
  kbuf[HALF + bq:, :] = kn_ref[...]
  vbuf[0:HALF, :] = vp_ref[...]
  vbuf[HALF:HALF + bq, :] = vc_ref[...]
  vbuf[HALF + bq:, :] = vn_ref[...]

  r = lax.broadcasted_iota(jnp.int32, (DIL_SUB, DIL_KEYS), 0)
  c = lax.broadcasted_iota(jnp.int32, (DIL_SUB, DIL_KEYS), 1)
  band = jnp.where((c >= r) & (c <= r + 2 * HALF), 0.0, NEG_INF).astype(F32)
  lo = jnp.where(c < HALF, jnp.where(first, NEG_INF, 0.0), 0.0).astype(F32)
  hi = jnp.where(c >= DIL_KEYS - HALF, jnp.where(last, NEG_INF, 0.0), 0.0).astype(F32)
  lane = lax.broadcasted_iota(jnp.int32, (DIL_SUB, HEAD_DIM), 1)
  nsub = bq // DIL_SUB

  for j in range(nsub):
    bias = band
    if j == 0:
      bias = bias + lo
    if j == nsub - 1:
      bias = bias + hi
    rs = slice(j * DIL_SUB, (j + 1) * DIL_SUB)
    ks = slice(j * DIL_SUB, j * DIL_SUB + DIL_KEYS)
    lse_tile = jnp.zeros((DIL_SUB, HEAD_DIM), F32)
    for h in range(N_DIL_HEADS):
      cs = slice(h * HEAD_DIM, (h + 1) * HEAD_DIM)
      s = lax.dot_general(q_ref[rs, cs], kbuf[ks, cs], (((1,), (1,)), ((), ())),
                          preferred_element_type=F32) + bias
      m = jnp.max(s, axis=-1, keepdims=True)
      p = jnp.exp(s - m)
      den = jnp.sum(p, axis=-1, keepdims=True)
      num = jnp.dot(p.astype(BF16), vbuf[ks, cs], preferred_element_type=F32)
      o_ref[rs, cs] = (num / den).astype(o_ref.dtype)
      lse_tile = jnp.where(lane == h, m + jnp.log(den), lse_tile)
    lse_ref[rs, :] = lse_tile


def _dilated_branch(src, col0, *, d, bq, prompt_rows, s_prompt, s_sample):
  rows = src.shape[1]
  hb = bq // HALF
  n_half_blocks = rows // HALF
  body = functools.partial(
      _dil_kernel, class_len_prompt=s_prompt // d, class_len_sample=s_sample // d,
      prompt_rows=prompt_rows // d)

  def cur(c):
    return pl.BlockSpec((None, bq, DIL_WIDTH), lambda r, i: (r, i, c))

  def prev(c):
    return pl.BlockSpec((None, HALF, DIL_WIDTH), lambda r, i: (r, jnp.maximum(i * hb - 1, 0), c))

  def nxt(c):
    return pl.BlockSpec((None, HALF, DIL_WIDTH),
                        lambda r, i: (r, jnp.minimum((i + 1) * hb, n_half_blocks - 1), c))

  return pl.pallas_call(
      body,
      grid=(d, rows // bq),
      in_specs=[cur(col0), prev(col0 + 1), cur(col0 + 1), nxt(col0 + 1),
                prev(col0 + 2), cur(col0 + 2), nxt(col0 + 2)],
      out_specs=[pl.BlockSpec((None, bq, DIL_WIDTH), lambda r, i: (r, i, 0)),
                 pl.BlockSpec((None, bq, HEAD_DIM), lambda r, i: (r, i, 0))],
      out_shape=[jax.ShapeDtypeStruct((d, rows, DIL_WIDTH), BF16),
                 jax.ShapeDtypeStruct((d, rows, HEAD_DIM), F32)],
      scratch_shapes=[pltpu.VMEM((bq + 2 * HALF, DIL_WIDTH), BF16),
                      pltpu.VMEM((bq + 2 * HALF, DIL_WIDTH), BF16)],
      compiler_params=_params("parallel", "parallel"),
      name=f"dilated_d{d}",
  )(*([src] * 7))


def _merge_kernel(o1_ref, o4_ref, o16_ref, l1_ref, l4_ref, l16_ref, gb_ref, b_ref,
                  s4, s16, ls4, ls16, bs):
  tm = o1_ref.shape[0]
  for rho in range(D4):
    ls4[pl.ds(rho, tm // D4, stride=D4), :] = l4_ref[rho]
  for r in range(D16):
    ls16[pl.ds(r, tm // D16, stride=D16), :] = l16_ref[r]

  ssq = jnp.zeros((tm, 1), F32)
  for h in range(N_DIL_HEADS):
    cs = slice(h * HEAD_DIM, (h + 1) * HEAD_DIM)
    for rho in range(D4):
      s4.at[h][pl.ds(rho, tm // D4, stride=D4), :] = o4_ref[rho, :, cs].astype(F32)
    for r in range(D16):
      s16.at[h][pl.ds(r, tm // D16, stride=D16), :] = o16_ref[r, :, cs].astype(F32)
    l1 = l1_ref[:, h:h + 1]
    l2 = ls4[:, h:h + 1]
    l3 = ls16[:, h:h + 1]
    m = jnp.maximum(jnp.maximum(l1, l2), l3)
    w1 = jnp.exp(l1 - m)
    w2 = jnp.exp(l2 - m)
    w3 = jnp.exp(l3 - m)
    b = (w1 * o1_ref[:, cs].astype(F32) + w2 * s4[h] + w3 * s16[h]) / (w1 + w2 + w3)
    ssq = ssq + jnp.sum(b * b, axis=-1, keepdims=True)
    bs[h] = b
  scale = lax.rsqrt(ssq * (1.0 / DIL_WIDTH) + EPS)
  for h in range(N_DIL_HEADS):
    cs = slice(h * HEAD_DIM, (h + 1) * HEAD_DIM)
    b_ref[:, cs] = (bs[h] * scale * gb_ref[:, cs]).astype(b_ref.dtype)


def _merge(o_list, lse_list, g_b, *, tm):
  o1, o4, o16 = o_list
  l1, l4, l16 = lse_list
  tok = o1.shape[1]
  head_scratch = pltpu.VMEM((N_DIL_HEADS, tm, HEAD_DIM), F32)
  return pl.pallas_call(
      _merge_kernel,
      grid=(tok // tm,),
      in_specs=[pl.BlockSpec((None, tm, DIL_WIDTH), lambda i: (0, i, 0)),
                pl.BlockSpec((D4, tm // D4, DIL_WIDTH), lambda i: (0, i, 0)),
                pl.BlockSpec((D16, tm // D16, DIL_WIDTH), lambda i: (0, i, 0)),
                pl.BlockSpec((None, tm, HEAD_DIM), lambda i: (0, i, 0)),
                pl.BlockSpec((D4, tm // D4, HEAD_DIM), lambda i: (0, i, 0)),
                pl.BlockSpec((D16, tm // D16, HEAD_DIM), lambda i: (0, i, 0)),
                pl.BlockSpec((1, DIL_WIDTH), lambda i: (0, 0))],
      out_specs=pl.BlockSpec((tm, DIL_WIDTH), lambda i: (i, 0)),
      out_shape=jax.ShapeDtypeStruct((tok, DIL_WIDTH), BF16),
      scratch_shapes=[head_scratch, head_scratch,
                      pltpu.VMEM((tm, HEAD_DIM), F32), pltpu.VMEM((tm, HEAD_DIM), F32),
                      head_scratch],
      compiler_params=_params("parallel"),
      name="merge",
  )(o1, o4, o16, l1, l4, l16, g_b)


def _out_proj_kernel(a_ref, b_ref, c_ref, w_ref, x_ref, y_ref):
  b0 = SGU_WIDTH
  c0 = SGU_WIDTH + DIL_WIDTH
  acc = jnp.dot(a_ref[...], w_ref[0:b0, :], preferred_element_type=F32)
  acc += jnp.dot(b_ref[...], w_ref[b0:c0, :], preferred_element_type=F32)
  acc += jnp.dot(c_ref[...], w_ref[c0:, :], preferred_element_type=F32)
  y_ref[...] = x_ref[...] + acc


def _out_proj(a_n, b_n, c_n, w_out, x, *, tm, tn):
  tok = x.shape[0]
  row = lambda w: pl.BlockSpec((tm, w), lambda i, j: (i, 0))
  return pl.pallas_call(
      _out_proj_kernel,
      grid=(tok // tm, D_MODEL // tn),
      in_specs=[row(SGU_WIDTH), row(DIL_WIDTH), row(MEM_WIDTH),
                pl.BlockSpec((MIX_WIDTH, tn), lambda i, j: (0, j)),
                pl.BlockSpec((tm, tn), lambda i, j: (i, j))],
      out_specs=pl.BlockSpec((tm, tn), lambda i, j: (i, j)),
      out_shape=jax.ShapeDtypeStruct((tok, D_MODEL), F32),
      compiler_params=_params("parallel", "arbitrary"),
      name="out_proj",
  )(a_n, b_n, c_n, w_out, x)


def _ffn_kernel(x_ref, g_ref, wg_ref, wu_ref, wd_ref, gf_ref, y_ref, h_ref, *, final_norm):
  k = pl.program_id(1)

  @pl.when(k == 0)
  def _():
    xf = x_ref[...]
    h_ref[...] = (xf * _rms_scale(xf) * g_ref[...]).astype(BF16)
    y_ref[...] = xf

  h = h_ref[...]
  gate = jnp.dot(h, wg_ref[...], preferred_element_type=F32)
  up = jnp.dot(h, wu_ref[...], preferred_element_type=F32)
  act = (jax.nn.silu(gate) * up).astype(BF16)
  y_ref[...] += jnp.dot(act, wd_ref[...], preferred_element_type=F32)

  if final_norm:
    @pl.when(k == pl.num_programs(1) - 1)
    def _():
      y = y_ref[...]
      y_ref[...] = y * _rms_scale(y) * gf_ref[...]


def _ffn(x, g, w_gu, w_down, g_final, *, tm, th, final_norm):
  tok = x.shape[0]
  nk = FFN_HIDDEN // th
  return pl.pallas_call(
      functools.partial(_ffn_kernel, final_norm=final_norm),
      grid=(tok // tm, nk),
      in_specs=[
          pl.BlockSpec((tm, D_MODEL), lambda i, k: (i, 0), pipeline_mode=pl.Buffered(1)),
          pl.BlockSpec((1, D_MODEL), lambda i, k: (0, 0)),
          pl.BlockSpec((D_MODEL, th), lambda i, k: (0, k)),
          pl.BlockSpec((D_MODEL, th), lambda i, k: (0, nk + k)),
          pl.BlockSpec((th, D_MODEL), lambda i, k: (k, 0)),
          pl.BlockSpec((1, D_MODEL), lambda i, k: (0, 0)),
      ],
      out_specs=pl.BlockSpec((tm, D_MODEL), lambda i, k: (i, 0)),
      out_shape=jax.ShapeDtypeStruct((tok, D_MODEL), F32),
      scratch_shapes=[pltpu.VMEM((tm, D_MODEL), BF16)],
      compiler_params=_params("parallel", "arbitrary"),
      name="ffn",
  )(x, g, w_gu, w_gu, w_down, g_final)


def _rope_tables(positions):
  inv = ROPE_THETA ** (-jnp.arange(0, ROPE_DIM, 2, dtype=F32) / ROPE_DIM)
  ang = positions.astype(F32)[:, None] * inv[None, :]
  cos, sin = jnp.cos(ang), jnp.sin(ang)
  n = positions.shape[0]
  rest = HEAD_DIM - ROPE_DIM
  c = jnp.concatenate([cos, cos, jnp.ones((n, rest), F32)], axis=1)
  s1 = jnp.concatenate([jnp.zeros((n, ROPE_HALF), F32), sin, jnp.zeros((n, rest), F32)], axis=1)
  s2 = jnp.concatenate([-sin, jnp.zeros((n, HEAD_DIM - ROPE_HALF), F32)], axis=1)
  return c, s1, s2


def kernel(x_prompt, x_sample, mem_prompt, mem_sample, g_mix_norm, w_in, g_sgu, w_spatial,
           b_spatial, g_mem_norm, w_mem_kv, g_group_out, w_out, g_ffn_norm, w_gate_up,
           w_down, g_final):
  n_prompt, s_prompt, _ = x_prompt.shape
  n_sample, s_sample, _ = x_sample.shape
  assert n_prompt == 1 and s_prompt % s_sample == 0
  assert s_sample % (D16 * BQ_DIL) == 0 and s_sample % TM_IN_PROJ == 0
  prompt_rows = n_prompt * s_prompt
  depth = w_in.shape[0]

  x = jnp.concatenate([x_prompt.reshape(prompt_rows, D_MODEL),
                       x_sample.reshape(n_sample * s_sample, D_MODEL)], axis=0)
  tok = x.shape[0]
  mem = jnp.concatenate([mem_prompt, mem_sample], axis=0)
  n_mems = mem.shape[0]
  mem = mem.reshape(n_mems * N_MEM, D_MODEL)
  positions = jnp.concatenate([jnp.arange(s_prompt, dtype=jnp.int32)] * n_prompt
                              + [jnp.arange(s_sample, dtype=jnp.int32)] * n_sample)
  rope = _rope_tables(positions)
  seqs = dict(prompt_rows=prompt_rows, s_prompt=s_prompt, s_sample=s_sample)

  for l in range(depth):
    z, c4, c16 = _in_proj(x, g_mix_norm[l], w_in[l].astype(BF16), rope, tm=TM_IN_PROJ)
    kv = _norm_matmul(mem, g_mem_norm[l], w_mem_kv[l].astype(BF16), tm=N_MEM, tn=MEM_WIDTH,
                      name="mem_kv").reshape(n_mems, N_MEM, 2 * MEM_WIDTH)
    g_out = g_group_out[l].reshape(1, MIX_WIDTH)
    a_n = _sgu(z, g_sgu[l].reshape(1, SGU_WIDTH), w_spatial[l].astype(BF16),
               jnp.broadcast_to(b_spatial[l][:, :, None], (N_SGU_GROUPS, SGU_CHUNK, HEAD_DIM)),
               g_out[:, :SGU_WIDTH], tb=TB_SGU)
    c_n = _mem_attn(z, kv, g_out[:, SGU_WIDTH + DIL_WIDTH:], tq=TQ_MEM, rows_per_mem=s_sample,
                    first_sample_tile=prompt_rows // s_sample)
    o_list, lse_list = [], []
    for d, src, col0 in ((1, z.reshape(1, tok, IN_WIDTH), 2 * SGU_WIDTH // DIL_WIDTH),
                         (D4, c4, 0), (D16, c16, 0)):
      o, lse = _dilated_branch(src, col0, d=d, bq=BQ_DIL, **seqs)
      o_list.append(o)
      lse_list.append(lse)
    b_n = _merge(o_list, lse_list, g_out[:, SGU_WIDTH:SGU_WIDTH + DIL_WIDTH], tm=TM_MERGE)
    x = _out_proj(a_n, b_n, c_n, w_out[l].astype(BF16), x, tm=TM_OUT_PROJ, tn=TN_OUT_PROJ)
    x = _ffn(x, g_ffn_norm[l].reshape(1, D_MODEL), w_gate_up[l].astype(BF16),
             w_down[l].astype(BF16), g_final.reshape(1, D_MODEL), tm=TM_FFN, th=TH_FFN,
             final_norm=(l == depth - 1))

  y_prompt = x[:prompt_rows].reshape(n_prompt, s_prompt, D_MODEL)
  y_sample = x[prompt_rows:].reshape(n_sample, s_sample, D_MODEL)
  return (y_prompt, y_sample)
```

```python
import functools
import math

import jax
import jax.numpy as jnp
from jax import lax
from jax.experimental import pallas as pl
from jax.experimental.pallas import tpu as pltpu

F32 = jnp.float32
BF16 = jnp.bfloat16

D_MODEL = 2048
HEAD_DIM = 128
N_SGU_GROUPS = 4
SGU_WIDTH = N_SGU_GROUPS * HEAD_DIM
SGU_CHUNK = 128
N_DIL_HEADS = 8
DIL_WIDTH = N_DIL_HEADS * HEAD_DIM
DILATIONS = (1, 4, 16)
HALF = 64
N_MEM_HEADS = 4
MEM_WIDTH = N_MEM_HEADS * HEAD_DIM
N_MEM = 256
MIX_WIDTH = SGU_WIDTH + DIL_WIDTH + MEM_WIDTH
IN_WIDTH = 2 * SGU_WIDTH + 3 * DIL_WIDTH + MEM_WIDTH
ROPE_THETA = 500000.0
ROPE_DIM = HEAD_DIM // 4
ROPE_HALF = ROPE_DIM // 2
FFN_HIDDEN = 5632
EPS = 1e-6
NEG_INF = -1e30
ATTN_SCALE = HEAD_DIM ** -0.5
DIL_Q_SCALE = ATTN_SCALE * math.log2(math.e)

V7X_VMEM_BYTES = 64 * 1024 * 1024
VMEM_LIMIT = V7X_VMEM_BYTES - 8 * 1024 * 1024

IN_CHUNK_N = 256
IN_SLABS = 4
IN_PROJ_VMEM_LIMIT = V7X_VMEM_BYTES - 3 * 1024 * 1024
QKV_WIDTH = 3 * DIL_WIDTH
D4, D16 = DILATIONS[1], DILATIONS[2]
D16_PER_D4 = D16 // D4

TM_IN_PROJ = 512
TM_OUT_PROJ = 1024
TN_OUT_PROJ = 1024
TM_FFN = 1024
TH_FFN = 512
TM_MERGE = 512
TB_SGU = 512
TQ_MEM = 512
BQ_DIL = 256


def _params(*sem, vmem=VMEM_LIMIT):
  return pltpu.CompilerParams(dimension_semantics=sem, vmem_limit_bytes=vmem)


def _rms_scale(x):
  return lax.rsqrt(jnp.mean(x * x, axis=-1, keepdims=True) + EPS)


def _norm_matmul_kernel(x_ref, g_ref, w_ref, o_ref, h_ref):
  @pl.when(pl.program_id(1) == 0)
  def _():
    xf = x_ref[...]
    h_ref[...] = (xf * _rms_scale(xf) * g_ref[...]).astype(BF16)

  o_ref[...] = jnp.dot(h_ref[...], w_ref[...], preferred_element_type=F32).astype(o_ref.dtype)


def _norm_matmul(x, g, w, layer, *, tm, tn, name):
  m, k = x.shape
  n = w.shape[2]
  return pl.pallas_call(
      _norm_matmul_kernel,
      grid=(m // tm, n // tn),
      in_specs=[pl.BlockSpec((tm, k), lambda i, j: (i, 0)),
                pl.BlockSpec((1, k), lambda i, j: (0, 0)),
                pl.BlockSpec((None, k, tn), lambda i, j: (layer, 0, j))],
      out_specs=pl.BlockSpec((tm, tn), lambda i, j: (i, j)),
      out_shape=jax.ShapeDtypeStruct((m, n), BF16),
      scratch_shapes=[pltpu.VMEM((tm, k), BF16)],
      compiler_params=_params("parallel", "arbitrary"),
      name=name,
  )(x, g.reshape(1, k), w)


def _in_col_kind(col):
  bounds = ((SGU_WIDTH, "u"), (2 * SGU_WIDTH, "v"), (2 * SGU_WIDTH + DIL_WIDTH, "q"),
            (2 * SGU_WIDTH + 2 * DIL_WIDTH, "k"), (2 * SGU_WIDTH + 3 * DIL_WIDTH, "vb"))
  for end, kind in bounds:
    if col < end:
      return kind
  return "qc"


def _in_proj_kernel(x_ref, g_ref, w_ref, c_ref, s1_ref, s2_ref, z_ref, c4_ref, c16_ref,
                    h_ref, slab, slab4):
  tm = x_ref.shape[0]
  xf = x_ref[...]
  h_ref[...] = (xf * _rms_scale(xf) * g_ref[...]).astype(BF16)
  c, s1, s2 = c_ref[...], s1_ref[...], s2_ref[...]
  tables = {"k": (c, s1, s2), "q": (c * DIL_Q_SCALE, s1 * DIL_Q_SCALE, s2 * DIL_Q_SCALE)}
  n_slabs = slab.shape[0]
  heads_per_chunk = IN_CHUNK_N // HEAD_DIM
  qkv0 = 2 * SGU_WIDTH
  for chunk in range(IN_WIDTH // IN_CHUNK_N):
    col0 = chunk * IN_CHUNK_N
    acc = jnp.dot(h_ref[...], w_ref[:, col0:col0 + IN_CHUNK_N], preferred_element_type=F32)
    for hh in range(heads_per_chunk):
      col = col0 + hh * HEAD_DIM
      kind = _in_col_kind(col)
      t = acc[:, hh * HEAD_DIM:(hh + 1) * HEAD_DIM]
      if kind in tables:
        tc, ts1, ts2 = tables[kind]
        t = (t * tc + pltpu.roll(t, ROPE_HALF, 1) * ts1
             + pltpu.roll(t, HEAD_DIM - ROPE_HALF, 1) * ts2)
      z_ref[:, col:col + HEAD_DIM] = t.astype(z_ref.dtype)
      if kind not in ("q", "k", "vb"):
        continue
      cs = slice(col - qkv0, col - qkv0 + HEAD_DIM)
      sl = (chunk * heads_per_chunk + hh) % n_slabs
      slab[sl] = t
      for rho in range(D4):
        t4 = slab.at[sl][pl.ds(rho, tm // D4, stride=D4), :]
        c4_ref[rho, :, cs] = t4.astype(c4_ref.dtype)
        slab4[sl, rho] = t4
        for q in range(D16_PER_D4):
          t16 = slab4.at[sl, rho][pl.ds(q, tm // D16, stride=D16_PER_D4), :]
          c16_ref[D4 * q + rho, :, cs] = t16.astype(c16_ref.dtype)


def _in_proj(x, g, w, layer, rope, *, tm):
  tok, k = x.shape
  return pl.pallas_call(
      _in_proj_kernel,
      grid=(tok // tm,),
      in_specs=[pl.BlockSpec((tm, k), lambda i: (i, 0)),
                pl.BlockSpec((1, k), lambda i: (0, 0)),
                pl.BlockSpec((None, k, IN_WIDTH), lambda i: (layer, 0, 0),
                             pipeline_mode=pl.Buffered(1))]
      + [pl.BlockSpec((tm, HEAD_DIM), lambda i: (i, 0))] * 3,
      out_specs=[pl.BlockSpec((tm, IN_WIDTH), lambda i: (i, 0)),
                 pl.BlockSpec((D4, tm // D4, QKV_WIDTH), lambda i: (0, i, 0)),
                 pl.BlockSpec((D16, tm // D16, QKV_WIDTH), lambda i: (0, i, 0))],
      out_shape=[jax.ShapeDtypeStruct((tok, IN_WIDTH), BF16),
                 jax.ShapeDtypeStruct((D4, tok // D4, QKV_WIDTH), BF16),
                 jax.ShapeDtypeStruct((D16, tok // D16, QKV_WIDTH), BF16)],
      scratch_shapes=[pltpu.VMEM((tm, k), BF16),
                      pltpu.VMEM((IN_SLABS, tm, HEAD_DIM), F32),
                      pltpu.VMEM((IN_SLABS, D4, tm // D4, HEAD_DIM), F32)],
      compiler_params=_params("parallel", vmem=IN_PROJ_VMEM_LIMIT),
      name="in_proj",
  )(x, g.reshape(1, k), w, *rope)


def _sgu_kernel(zu_ref, zv_ref, gs_ref, w_ref, b_ref, go_ref, o_ref):
  rows = zu_ref.shape[0]
  for c in range(rows // SGU_CHUNK):
    rs = slice(c * SGU_CHUNK, (c + 1) * SGU_CHUNK)
    outs = []
    ssq = jnp.zeros((SGU_CHUNK, 1), F32)
    for g in range(N_SGU_GROUPS):
      cs = slice(g * HEAD_DIM, (g + 1) * HEAD_DIM)
      u = jax.nn.gelu(zu_ref[rs, cs].astype(F32))
      v = jax.nn.gelu(zv_ref[rs, cs].astype(F32))
      vv = v * _rms_scale(v) * gs_ref[:, cs]
      vs = jnp.dot(w_ref[g], vv.astype(BF16), preferred_element_type=F32) + b_ref[g]
      a = u * vs
      ssq = ssq + jnp.sum(a * a, axis=-1, keepdims=True)
      outs.append(a)
    scale = lax.rsqrt(ssq * (1.0 / SGU_WIDTH) + EPS)
    for g in range(N_SGU_GROUPS):
      cs = slice(g * HEAD_DIM, (g + 1) * HEAD_DIM)
      o_ref[rs, cs] = (outs[g] * scale * go_ref[:, cs]).astype(o_ref.dtype)


def _sgu(z, g_sgu, w_sp, layer, b_sp, g_out, *, tb):
  tok = z.shape[0]
  return pl.pallas_call(
      _sgu_kernel,
      grid=(tok // tb,),
      in_specs=[
          pl.BlockSpec((tb, SGU_WIDTH), lambda i: (i, 0)),
          pl.BlockSpec((tb, SGU_WIDTH), lambda i: (i, 1)),
          pl.BlockSpec((1, SGU_WIDTH), lambda i: (0, 0)),
          pl.BlockSpec((None, N_SGU_GROUPS, SGU_CHUNK, SGU_CHUNK), lambda i: (layer, 0, 0, 0)),
          pl.BlockSpec((N_SGU_GROUPS, SGU_CHUNK, HEAD_DIM), lambda i: (0, 0, 0)),
          pl.BlockSpec((1, SGU_WIDTH), lambda i: (0, 0)),
      ],
      out_specs=pl.BlockSpec((tb, SGU_WIDTH), lambda i: (i, 0)),
      out_shape=jax.ShapeDtypeStruct((tok, SGU_WIDTH), BF16),
      compiler_params=_params("parallel"),
      name="sgu",
  )(z, z, g_sgu, w_sp, b_sp, g_out)


def _mem_attn_kernel(q_ref, kv_ref, go_ref, o_ref):
  outs = []
  ssq = jnp.zeros((q_ref.shape[0], 1), F32)
  for h in range(N_MEM_HEADS):
    cs = slice(h * HEAD_DIM, (h + 1) * HEAD_DIM)
    q = q_ref[:, cs]
    k = kv_ref[:, cs]
    v = kv_ref[:, MEM_WIDTH + h * HEAD_DIM:MEM_WIDTH + (h + 1) * HEAD_DIM]
    s = lax.dot_general(q, k, (((1,), (1,)), ((), ())), preferred_element_type=F32) * ATTN_SCALE
    m = jnp.max(s, axis=-1, keepdims=True)
    p = jnp.exp(s - m)
    den = jnp.sum(p, axis=-1, keepdims=True)
    o = jnp.dot(p.astype(BF16), v, preferred_element_type=F32) / den
    ssq = ssq + jnp.sum(o * o, axis=-1, keepdims=True)
    outs.append(o)
  scale = lax.rsqrt(ssq * (1.0 / MEM_WIDTH) + EPS)
  for h in range(N_MEM_HEADS):
    cs = slice(h * HEAD_DIM, (h + 1) * HEAD_DIM)
    o_ref[:, cs] = (outs[h] * scale * go_ref[:, cs]).astype(o_ref.dtype)


def _mem_attn(z, kv, g_out, *, tq, rows_per_mem, first_sample_tile):
  tok = z.shape[0]
  tiles_per_mem = rows_per_mem // tq

  def kv_map(i):
    return (jnp.maximum(i // tiles_per_mem - first_sample_tile + 1, 0), 0, 0)

  return pl.pallas_call(
      _mem_attn_kernel,
      grid=(tok // tq,),
      in_specs=[
          pl.BlockSpec((tq, MEM_WIDTH), lambda i: (i, IN_WIDTH // MEM_WIDTH - 1)),
          pl.BlockSpec((None, N_MEM, 2 * MEM_WIDTH), kv_map),
          pl.BlockSpec((1, MEM_WIDTH), lambda i: (0, 0)),
      ],
      out_specs=pl.BlockSpec((tq, MEM_WIDTH), lambda i: (i, 0)),
      out_shape=jax.ShapeDtypeStruct((tok, MEM_WIDTH), BF16),
      compiler_params=_params("parallel"),
      name="mem_attn",
  )(z, kv, g_out)


DIL_SUB = 2 * HALF
DIL_KEYS = DIL_SUB + 2 * HALF


def _dil_kernel(q_ref, kp_ref, kc_ref, kn_ref, vp_ref, vc_ref, vn_ref, o_ref, lse_ref,
                kbuf, vbuf, *, class_len_prompt, class_len_sample, prompt_rows):
  bq = q_ref.shape[0]
  row0 = pl.program_id(1) * bq
  in_prompt = row0 < prompt_rows
  clen = jnp.where(in_prompt, class_len_prompt, class_len_sample)
  pos = jnp.where(in_prompt, row0, row0 - prompt_rows) & (clen - 1)
  first = pos == 0
  last = pos + bq == clen

  kbuf[0:HALF, :] = kp_ref[...]
  kbuf[HALF:HALF + bq, :] = kc_ref[...]
  kbuf[HALF + bq:, :] = kn_ref[...]
  vbuf[0:HALF, :] = vp_ref[...]
  vbuf[HALF:HALF + bq, :] = vc_ref[...]
  vbuf[HALF + bq:, :] = vn_ref[...]

  r = lax.broadcasted_iota(jnp.int32, (DIL_SUB, DIL_KEYS), 0)
  c = lax.broadcasted_iota(jnp.int32, (DIL_SUB, DIL_KEYS), 1)
  band = jnp.where((c >= r) & (c <= r + 2 * HALF), 0.0, NEG_INF).astype(F32)
  lo = jnp.where(c < HALF, jnp.where(first, NEG_INF, 0.0), 0.0).astype(F32)
  hi = jnp.where(c >= DIL_KEYS - HALF, jnp.where(last, NEG_INF, 0.0), 0.0).astype(F32)
  lane = lax.broadcasted_iota(jnp.int32, (DIL_SUB, HEAD_DIM), 1)
  nsub = bq // DIL_SUB

  for j in range(nsub):
    bias = band
    if j == 0:
      bias = bias + lo
    if j == nsub - 1:
      bias = bias + hi
    rs = slice(j * DIL_SUB, (j + 1) * DIL_SUB)
    ks = slice(j * DIL_SUB, j * DIL_SUB + DIL_KEYS)
    lse_tile = jnp.zeros((DIL_SUB, HEAD_DIM), F32)
    for h in range(N_DIL_HEADS):
      cs = slice(h * HEAD_DIM, (h + 1) * HEAD_DIM)
      s = lax.dot_general(q_ref[rs, cs], kbuf[ks, cs], (((1,), (1,)), ((), ())),
                          preferred_element_type=F32) + bias
      m = jnp.max(s, axis=-1, keepdims=True)
      p = jnp.exp2(s - m)
      den = jnp.sum(p, axis=-1, keepdims=True)
      num = jnp.dot(p.astype(BF16), vbuf[ks, cs], preferred_element_type=F32)
      o_ref[rs, cs] = (num / den).astype(o_ref.dtype)
      lse_tile = jnp.where(lane == h, m + jnp.log2(den), lse_tile)
    lse_ref[rs, :] = lse_tile


def _dilated_branch(src, col0, *, d, bq, prompt_rows, s_prompt, s_sample):
  rows = src.shape[1]
  hb = bq // HALF
  n_half_blocks = rows // HALF
  body = functools.partial(
      _dil_kernel, class_len_prompt=s_prompt // d, class_len_sample=s_sample // d,
      prompt_rows=prompt_rows // d)

  def cur(c):
    return pl.BlockSpec((None, bq, DIL_WIDTH), lambda r, i: (r, i, c))

  def prev(c):
    return pl.BlockSpec((None, HALF, DIL_WIDTH), lambda r, i: (r, jnp.maximum(i * hb - 1, 0), c))

  def nxt(c):
    return pl.BlockSpec((None, HALF, DIL_WIDTH),
                        lambda r, i: (r, jnp.minimum((i + 1) * hb, n_half_blocks - 1), c))

  return pl.pallas_call(
      body,
      grid=(d, rows // bq),
      in_specs=[cur(col0), prev(col0 + 1), cur(col0 + 1), nxt(col0 + 1),
                prev(col0 + 2), cur(col0 + 2), nxt(col0 + 2)],
      out_specs=[pl.BlockSpec((None, bq, DIL_WIDTH), lambda r, i: (r, i, 0)),
                 pl.BlockSpec((None, bq, HEAD_DIM), lambda r, i: (r, i, 0))],
      out_shape=[jax.ShapeDtypeStruct((d, rows, DIL_WIDTH), BF16),
                 jax.ShapeDtypeStruct((d, rows, HEAD_DIM), F32)],
      scratch_shapes=[pltpu.VMEM((bq + 2 * HALF, DIL_WIDTH), BF16),
                      pltpu.VMEM((bq + 2 * HALF, DIL_WIDTH), BF16)],
      compiler_params=_params("parallel", "parallel"),
      name=f"dilated_d{d}",
  )(*([src] * 7))


def _merge_kernel(o1_ref, o4_ref, o16_ref, l1_ref, l4_ref, l16_ref, e_ref, gb_ref, b_ref,
                  s4, s16, ls4, ls16, bs):
  tm = o1_ref.shape[0]
  for rho in range(D4):
    ls4[pl.ds(rho, tm // D4, stride=D4), :] = l4_ref[rho]
  for r in range(D16):
    ls16[pl.ds(r, tm // D16, stride=D16), :] = l16_ref[r]

  l1, l2, l3 = l1_ref[...], ls4[...], ls16[...]
  m = jnp.maximum(jnp.maximum(l1, l2), l3)
  w1, w2, w3 = jnp.exp2(l1 - m), jnp.exp2(l2 - m), jnp.exp2(l3 - m)
  inv = 1.0 / (w1 + w2 + w3)

  def over_head_lanes(w):
    wn = w * inv
    hi = wn.astype(BF16)
    lo = (wn - hi.astype(F32)).astype(BF16)
    return jnp.dot(jnp.concatenate([hi, lo], axis=1), e_ref[...], preferred_element_type=F32)

  wb1, wb2, wb3 = over_head_lanes(w1), over_head_lanes(w2), over_head_lanes(w3)

  ssq = jnp.zeros((tm, 1), F32)
  for h in range(N_DIL_HEADS):
    cs = slice(h * HEAD_DIM, (h + 1) * HEAD_DIM)
    for rho in range(D4):
      s4.at[h][pl.ds(rho, tm // D4, stride=D4), :] = o4_ref[rho, :, cs].astype(F32)
    for r in range(D16):
      s16.at[h][pl.ds(r, tm // D16, stride=D16), :] = o16_ref[r, :, cs].astype(F32)
    b = wb1[:, cs] * o1_ref[:, cs].astype(F32) + wb2[:, cs] * s4[h] + wb3[:, cs] * s16[h]
    ssq = ssq + jnp.sum(b * b, axis=-1, keepdims=True)
    bs[h] = b
  scale = jnp.broadcast_to(lax.rsqrt(ssq * (1.0 / DIL_WIDTH) + EPS), (tm, HEAD_DIM))
  for h in range(N_DIL_HEADS):
    cs = slice(h * HEAD_DIM, (h + 1) * HEAD_DIM)
    b_ref[:, cs] = (bs[h] * scale * gb_ref[:, cs]).astype(b_ref.dtype)


def _merge(o_list, lse_list, g_b, *, tm):
  o1, o4, o16 = o_list
  l1, l4, l16 = lse_list
  tok = o1.shape[1]
  head_of_lane = jnp.arange(DIL_WIDTH, dtype=jnp.int32) // HEAD_DIM
  selector = (jnp.arange(HEAD_DIM, dtype=jnp.int32)[:, None] == head_of_lane[None, :]).astype(BF16)
  selector = jnp.concatenate([selector, selector], axis=0)
  head_scratch = pltpu.VMEM((N_DIL_HEADS, tm, HEAD_DIM), F32)
  return pl.pallas_call(
      _merge_kernel,
      grid=(tok // tm,),
      in_specs=[pl.BlockSpec((None, tm, DIL_WIDTH), lambda i: (0, i, 0)),
                pl.BlockSpec((D4, tm // D4, DIL_WIDTH), lambda i: (0, i, 0)),
                pl.BlockSpec((D16, tm // D16, DIL_WIDTH), lambda i: (0, i, 0)),
                pl.BlockSpec((None, tm, HEAD_DIM), lambda i: (0, i, 0)),
                pl.BlockSpec((D4, tm // D4, HEAD_DIM), lambda i: (0, i, 0)),
                pl.BlockSpec((D16, tm // D16, HEAD_DIM), lambda i: (0, i, 0)),
                pl.BlockSpec((2 * HEAD_DIM, DIL_WIDTH), lambda i: (0, 0)),
                pl.BlockSpec((1, DIL_WIDTH), lambda i: (0, 0))],
      out_specs=pl.BlockSpec((tm, DIL_WIDTH), lambda i: (i, 0)),
      out_shape=jax.ShapeDtypeStruct((tok, DIL_WIDTH), BF16),
      scratch_shapes=[head_scratch, head_scratch,
                      pltpu.VMEM((tm, HEAD_DIM), F32), pltpu.VMEM((tm, HEAD_DIM), F32),
                      head_scratch],
      compiler_params=_params("parallel"),
      name="merge",
  )(o1, o4, o16, l1, l4, l16, selector, g_b)


def _out_proj_kernel(a_ref, b_ref, c_ref, w_ref, x_ref, y_ref):
  b0 = SGU_WIDTH
  c0 = SGU_WIDTH + DIL_WIDTH
  acc = jnp.dot(a_ref[...], w_ref[0:b0, :], preferred_element_type=F32)
  acc += jnp.dot(b_ref[...], w_ref[b0:c0, :], preferred_element_type=F32)
  acc += jnp.dot(c_ref[...], w_ref[c0:, :], preferred_element_type=F32)
  y_ref[...] = x_ref[...] + acc


def _out_proj(a_n, b_n, c_n, w_out, layer, x, *, tm, tn):
  tok = x.shape[0]
  row = lambda w: pl.BlockSpec((tm, w), lambda i, j: (i, 0))
  return pl.pallas_call(
      _out_proj_kernel,
      grid=(tok // tm, D_MODEL // tn),
      in_specs=[row(SGU_WIDTH), row(DIL_WIDTH), row(MEM_WIDTH),
                pl.BlockSpec((None, MIX_WIDTH, tn), lambda i, j: (layer, 0, j)),
                pl.BlockSpec((tm, tn), lambda i, j: (i, j))],
      out_specs=pl.BlockSpec((tm, tn), lambda i, j: (i, j)),
      out_shape=jax.ShapeDtypeStruct((tok, D_MODEL), F32),
      compiler_params=_params("parallel", "arbitrary"),
      name="out_proj",
  )(a_n, b_n, c_n, w_out, x)


def _ffn_kernel(x_ref, g_ref, wg_ref, wu_ref, wd_ref, gf_ref, y_ref, h_ref, *, final_norm):
  k = pl.program_id(1)

  @pl.when(k == 0)
  def _():
    xf = x_ref[...]
    h_ref[...] = (xf * _rms_scale(xf) * g_ref[...]).astype(BF16)
    y_ref[...] = xf

  h = h_ref[...]
  gate = jnp.dot(h, wg_ref[...], preferred_element_type=F32)
  up = jnp.dot(h, wu_ref[...], preferred_element_type=F32)
  act = (jax.nn.silu(gate) * up).astype(BF16)
  y_ref[...] += jnp.dot(act, wd_ref[...], preferred_element_type=F32)

  if final_norm:
    @pl.when(k == pl.num_programs(1) - 1)
    def _():
      y = y_ref[...]
      y_ref[...] = y * _rms_scale(y) * gf_ref[...]


def _ffn(x, g, w_gu, w_down, layer, g_final, *, tm, th, final_norm):
  tok = x.shape[0]
  nk = FFN_HIDDEN // th
  return pl.pallas_call(
      functools.partial(_ffn_kernel, final_norm=final_norm),
      grid=(tok // tm, nk),
      in_specs=[
          pl.BlockSpec((tm, D_MODEL), lambda i, k: (i, 0), pipeline_mode=pl.Buffered(1)),
          pl.BlockSpec((1, D_MODEL), lambda i, k: (0, 0)),
          pl.BlockSpec((None, D_MODEL, th), lambda i, k: (layer, 0, k)),
          pl.BlockSpec((None, D_MODEL, th), lambda i, k: (layer, 0, nk + k)),
          pl.BlockSpec((None, th, D_MODEL), lambda i, k: (layer, k, 0)),
          pl.BlockSpec((1, D_MODEL), lambda i, k: (0, 0)),
      ],
      out_specs=pl.BlockSpec((tm, D_MODEL), lambda i, k: (i, 0)),
      out_shape=jax.ShapeDtypeStruct((tok, D_MODEL), F32),
      scratch_shapes=[pltpu.VMEM((tm, D_MODEL), BF16)],
      compiler_params=_params("parallel", "arbitrary"),
      name="ffn",
  )(x, g, w_gu, w_gu, w_down, g_final)


def _rope_tables(positions):
  inv = ROPE_THETA ** (-jnp.arange(0, ROPE_DIM, 2, dtype=F32) / ROPE_DIM)
  ang = positions.astype(F32)[:, None] * inv[None, :]
  cos, sin = jnp.cos(ang), jnp.sin(ang)
  n = positions.shape[0]
  rest = HEAD_DIM - ROPE_DIM
  c = jnp.concatenate([cos, cos, jnp.ones((n, rest), F32)], axis=1)
  s1 = jnp.concatenate([jnp.zeros((n, ROPE_HALF), F32), sin, jnp.zeros((n, rest), F32)], axis=1)
  s2 = jnp.concatenate([-sin, jnp.zeros((n, HEAD_DIM - ROPE_HALF), F32)], axis=1)
  return c, s1, s2


def kernel(x_prompt, x_sample, mem_prompt, mem_sample, g_mix_norm, w_in, g_sgu, w_spatial,
           b_spatial, g_mem_norm, w_mem_kv, g_group_out, w_out, g_ffn_norm, w_gate_up,
           w_down, g_final):
  n_prompt, s_prompt, _ = x_prompt.shape
  n_sample, s_sample, _ = x_sample.shape
  assert n_prompt == 1 and s_prompt % s_sample == 0
  assert s_sample % (D16 * BQ_DIL) == 0 and s_sample % TM_OUT_PROJ == 0
  prompt_rows = n_prompt * s_prompt
  depth = w_in.shape[0]

  x = jnp.concatenate([x_prompt.reshape(prompt_rows, D_MODEL),
                       x_sample.reshape(n_sample * s_sample, D_MODEL)], axis=0)
  tok = x.shape[0]
  mem = jnp.concatenate([mem_prompt, mem_sample], axis=0)
  n_mems = mem.shape[0]
  mem = mem.reshape(n_mems * N_MEM, D_MODEL)
  positions = jnp.concatenate([jnp.arange(s_prompt, dtype=jnp.int32)] * n_prompt
                              + [jnp.arange(s_sample, dtype=jnp.int32)] * n_sample)
  rope = _rope_tables(positions)
  seqs = dict(prompt_rows=prompt_rows, s_prompt=s_prompt, s_sample=s_sample)
  w_in, w_spatial, w_mem_kv, w_out, w_gate_up, w_down = (
      w.astype(BF16) for w in (w_in, w_spatial, w_mem_kv, w_out, w_gate_up, w_down))

  for l in range(depth):
    z, c4, c16 = _in_proj(x, g_mix_norm[l], w_in, l, rope, tm=TM_IN_PROJ)
    kv = _norm_matmul(mem, g_mem_norm[l], w_mem_kv, l, tm=N_MEM, tn=MEM_WIDTH,
                      name="mem_kv").reshape(n_mems, N_MEM, 2 * MEM_WIDTH)
    g_out = g_group_out[l].reshape(1, MIX_WIDTH)
    a_n = _sgu(z, g_sgu[l].reshape(1, SGU_WIDTH), w_spatial, l,
               jnp.broadcast_to(b_spatial[l][:, :, None], (N_SGU_GROUPS, SGU_CHUNK, HEAD_DIM)),
               g_out[:, :SGU_WIDTH], tb=TB_SGU)
    c_n = _mem_attn(z, kv, g_out[:, SGU_WIDTH + DIL_WIDTH:], tq=TQ_MEM, rows_per_mem=s_sample,
                    first_sample_tile=prompt_rows // s_sample)
    o_list, lse_list = [], []
    for d, src, col0 in ((1, z.reshape(1, tok, IN_WIDTH), 2 * SGU_WIDTH // DIL_WIDTH),
                         (D4, c4, 0), (D16, c16, 0)):
      o, lse = _dilated_branch(src, col0, d=d, bq=BQ_DIL, **seqs)
      o_list.append(o)
      lse_list.append(lse)
    b_n = _merge(o_list, lse_list, g_out[:, SGU_WIDTH:SGU_WIDTH + DIL_WIDTH], tm=TM_MERGE)
    x = _out_proj(a_n, b_n, c_n, w_out, l, x, tm=TM_OUT_PROJ, tn=TN_OUT_PROJ)
    x = _ffn(x, g_ffn_norm[l].reshape(1, D_MODEL), w_gate_up, w_down, l,
             g_final.reshape(1, D_MODEL), tm=TM_FFN, th=TH_FFN, final_norm=(l == depth - 1))

  y_prompt = x[:prompt_rows].reshape(n_prompt, s_prompt, D_MODEL)
  y_sample = x[prompt_rows:].reshape(n_sample, s_sample, D_MODEL)
  return (y_prompt, y_sample)
```

```python
import functools
import math

import jax
import jax.numpy as jnp
from jax import lax
from jax.experimental import pallas as pl
from jax.experimental.pallas import tpu as pltpu

F32 = jnp.float32
BF16 = jnp.bfloat16

D_MODEL = 2048
HEAD_DIM = 128
N_SGU_GROUPS = 4
SGU_WIDTH = N_SGU_GROUPS * HEAD_DIM
SGU_CHUNK = 128
N_DIL_HEADS = 8
DIL_WIDTH = N_DIL_HEADS * HEAD_DIM
DILATIONS = (1, 4, 16)
HALF = 64
N_MEM_HEADS = 4
MEM_WIDTH = N_MEM_HEADS * HEAD_DIM
N_MEM = 256
MIX_WIDTH = SGU_WIDTH + DIL_WIDTH + MEM_WIDTH
IN_WIDTH = 2 * SGU_WIDTH + 3 * DIL_WIDTH + MEM_WIDTH
ROPE_THETA = 500000.0
ROPE_DIM = HEAD_DIM // 4
ROPE_HALF = ROPE_DIM // 2
FFN_HIDDEN = 5632
EPS = 1e-6
NEG_INF = -1e30
ATTN_SCALE = HEAD_DIM ** -0.5
DIL_Q_SCALE = ATTN_SCALE * math.log2(math.e)

V7X_VMEM_BYTES = 64 * 1024 * 1024
VMEM_LIMIT = V7X_VMEM_BYTES - 8 * 1024 * 1024

IN_CHUNK_N = 256
IN_SLABS = 4
IN_PROJ_VMEM_LIMIT = V7X_VMEM_BYTES - 3 * 1024 * 1024
IN_V0 = SGU_WIDTH
IN_QKV0 = 2 * SGU_WIDTH
IN_QC0 = IN_QKV0 + 3 * DIL_WIDTH
QKV_WIDTH = 3 * DIL_WIDTH
D4, D16 = DILATIONS[1], DILATIONS[2]
D16_PER_D4 = D16 // D4

TM_IN_PROJ = 512
TM_OUT_PROJ = 1024
TN_OUT_PROJ = 1024
TM_FFN = 1024
TH_FFN = 512
TM_MERGE = 512
BQ_DIL = 256


def _params(*sem, vmem=VMEM_LIMIT):
  return pltpu.CompilerParams(dimension_semantics=sem, vmem_limit_bytes=vmem)


def _rms_scale(x):
  return lax.rsqrt(jnp.mean(x * x, axis=-1, keepdims=True) + EPS)


def _norm_matmul_kernel(x_ref, g_ref, w_ref, o_ref, h_ref):
  @pl.when(pl.program_id(1) == 0)
  def _():
    xf = x_ref[...]
    h_ref[...] = (xf * _rms_scale(xf) * g_ref[...]).astype(BF16)

  o_ref[...] = jnp.dot(h_ref[...], w_ref[...], preferred_element_type=F32).astype(o_ref.dtype)


def _norm_matmul(x, g, w, layer, *, tm, tn, name):
  m, k = x.shape
  n = w.shape[2]
  return pl.pallas_call(
      _norm_matmul_kernel,
      grid=(m // tm, n // tn),
      in_specs=[pl.BlockSpec((tm, k), lambda i, j: (i, 0)),
                pl.BlockSpec((1, k), lambda i, j: (0, 0)),
                pl.BlockSpec((None, k, tn), lambda i, j: (layer, 0, j))],
      out_specs=pl.BlockSpec((tm, tn), lambda i, j: (i, j)),
      out_shape=jax.ShapeDtypeStruct((m, n), BF16),
      scratch_shapes=[pltpu.VMEM((tm, k), BF16)],
      compiler_params=_params("parallel", "arbitrary"),
      name=name,
  )(x, g.reshape(1, k), w)


def _in_col_kind(col):
  bounds = ((IN_V0, "u"), (IN_QKV0, "v"), (IN_QKV0 + DIL_WIDTH, "q"),
            (IN_QKV0 + 2 * DIL_WIDTH, "k"), (IN_QC0, "vb"))
  for end, kind in bounds:
    if col < end:
      return kind
  return "qc"


def _mem_attention(q_heads, kv_ref, go_ref, o_ref):
  outs = []
  ssq = jnp.zeros((q_heads[0].shape[0], 1), F32)
  for h, q in enumerate(q_heads):
    k = kv_ref[:, h * HEAD_DIM:(h + 1) * HEAD_DIM]
    v = kv_ref[:, MEM_WIDTH + h * HEAD_DIM:MEM_WIDTH + (h + 1) * HEAD_DIM]
    s = lax.dot_general(q, k, (((1,), (1,)), ((), ())), preferred_element_type=F32) * ATTN_SCALE
    m = jnp.max(s, axis=-1, keepdims=True)
    p = jnp.exp(s - m)
    den = jnp.sum(p, axis=-1, keepdims=True)
    o = jnp.dot(p.astype(BF16), v, preferred_element_type=F32) / den
    ssq = ssq + jnp.sum(o * o, axis=-1, keepdims=True)
    outs.append(o)
  scale = lax.rsqrt(ssq * (1.0 / MEM_WIDTH) + EPS)
  for h in range(N_MEM_HEADS):
    cs = slice(h * HEAD_DIM, (h + 1) * HEAD_DIM)
    o_ref[:, cs] = (outs[h] * scale * go_ref[:, cs]).astype(o_ref.dtype)


def _spatial_gating(uv_ref, gs_ref, w_ref, b_ref, go_ref, o_ref):
  for c in range(uv_ref.shape[0] // SGU_CHUNK):
    rs = slice(c * SGU_CHUNK, (c + 1) * SGU_CHUNK)
    outs = []
    ssq = jnp.zeros((SGU_CHUNK, 1), F32)
    for g in range(N_SGU_GROUPS):
      cs = slice(g * HEAD_DIM, (g + 1) * HEAD_DIM)
      v = uv_ref[rs, IN_V0 + g * HEAD_DIM:IN_V0 + (g + 1) * HEAD_DIM]
      vv = v * _rms_scale(v) * gs_ref[:, cs]
      vs = jnp.dot(w_ref[g], vv.astype(BF16), preferred_element_type=F32) + b_ref[g]
      a = uv_ref[rs, cs] * vs
      ssq = ssq + jnp.sum(a * a, axis=-1, keepdims=True)
      outs.append(a)
    scale = lax.rsqrt(ssq * (1.0 / SGU_WIDTH) + EPS)
    for g in range(N_SGU_GROUPS):
      cs = slice(g * HEAD_DIM, (g + 1) * HEAD_DIM)
      o_ref[rs, cs] = (outs[g] * scale * go_ref[:, cs]).astype(o_ref.dtype)


def _in_proj_kernel(x_ref, g_ref, w_ref, c_ref, s1_ref, s2_ref, kv_ref, gs_ref, wsp_ref, bsp_ref,
                    go_ref, z_ref, c4_ref, c16_ref, a_ref, cn_ref, h_ref, slab, slab4, uv_ref):
  tm = x_ref.shape[0]
  xf = x_ref[...]
  h_ref[...] = (xf * _rms_scale(xf) * g_ref[...]).astype(BF16)
  c, s1, s2 = c_ref[...], s1_ref[...], s2_ref[...]
  tables = {"k": (c, s1, s2), "q": (c * DIL_Q_SCALE, s1 * DIL_Q_SCALE, s2 * DIL_Q_SCALE)}
  n_slabs = slab.shape[0]
  heads_per_chunk = IN_CHUNK_N // HEAD_DIM
  n_chunks = IN_WIDTH // IN_CHUNK_N
  first_qc, first_qkv = IN_QC0 // IN_CHUNK_N, IN_QKV0 // IN_CHUNK_N
  order = (list(range(first_qc, n_chunks)) + list(range(first_qkv))
           + list(range(first_qkv, first_qc)))
  qc_heads = []
  for chunk in order:
    col0 = chunk * IN_CHUNK_N
    acc = jnp.dot(h_ref[...], w_ref[:, col0:col0 + IN_CHUNK_N], preferred_element_type=F32)
    for hh in range(heads_per_chunk):
      col = col0 + hh * HEAD_DIM
      kind = _in_col_kind(col)
      t = acc[:, hh * HEAD_DIM:(hh + 1) * HEAD_DIM]
      if kind == "qc":
        qc_heads.append(t.astype(BF16))
        continue
      if kind in ("u", "v"):
        uv_ref[:, col:col + HEAD_DIM] = jax.nn.gelu(t)
        continue
      if kind in tables:
        tc, ts1, ts2 = tables[kind]
        t = (t * tc + pltpu.roll(t, ROPE_HALF, 1) * ts1
             + pltpu.roll(t, HEAD_DIM - ROPE_HALF, 1) * ts2)
      cs = slice(col - IN_QKV0, col - IN_QKV0 + HEAD_DIM)
      z_ref[:, cs] = t.astype(z_ref.dtype)
      sl = (chunk * heads_per_chunk + hh) % n_slabs
      slab[sl] = t
      for rho in range(D4):
        t4 = slab.at[sl][pl.ds(rho, tm // D4, stride=D4), :]
        c4_ref[rho, :, cs] = t4.astype(c4_ref.dtype)
        slab4[sl, rho] = t4
        for q in range(D16_PER_D4):
          t16 = slab4.at[sl, rho][pl.ds(q, tm // D16, stride=D16_PER_D4), :]
          c16_ref[D4 * q + rho, :, cs] = t16.astype(c16_ref.dtype)
    if chunk == n_chunks - 1:
      _mem_attention(qc_heads, kv_ref, go_ref.at[:, SGU_WIDTH + DIL_WIDTH:], cn_ref)
    if chunk == first_qkv - 1:
      _spatial_gating(uv_ref, gs_ref, wsp_ref, bsp_ref, go_ref.at[:, :SGU_WIDTH], a_ref)


def _in_proj(x, g, w, layer, rope, kv, g_sgu, w_sp, b_sp, g_out, *, tm, rows_per_mem,
             first_sample_mem_tile):
  tok, k = x.shape
  tiles_per_mem = rows_per_mem // tm

  def kv_map(i):
    return (jnp.maximum(i // tiles_per_mem - first_sample_mem_tile + 1, 0), 0, 0)

  const = lambda shape: pl.BlockSpec(shape, lambda i: (0,) * len(shape))
  return pl.pallas_call(
      _in_proj_kernel,
      grid=(tok // tm,),
      in_specs=[pl.BlockSpec((tm, k), lambda i: (i, 0)),
                const((1, k)),
                pl.BlockSpec((None, k, IN_WIDTH), lambda i: (layer, 0, 0),
                             pipeline_mode=pl.Buffered(1))]
      + [pl.BlockSpec((tm, HEAD_DIM), lambda i: (i, 0))] * 3
      + [pl.BlockSpec((None, N_MEM, 2 * MEM_WIDTH), kv_map),
         const((1, SGU_WIDTH)),
         pl.BlockSpec((None, N_SGU_GROUPS, SGU_CHUNK, SGU_CHUNK), lambda i: (layer, 0, 0, 0)),
         const((N_SGU_GROUPS, SGU_CHUNK, HEAD_DIM)),
         const((1, MIX_WIDTH))],
      out_specs=[pl.BlockSpec((tm, QKV_WIDTH), lambda i: (i, 0)),
                 pl.BlockSpec((D4, tm // D4, QKV_WIDTH), lambda i: (0, i, 0)),
                 pl.BlockSpec((D16, tm // D16, QKV_WIDTH), lambda i: (0, i, 0)),
                 pl.BlockSpec((tm, SGU_WIDTH), lambda i: (i, 0)),
                 pl.BlockSpec((tm, MEM_WIDTH), lambda i: (i, 0))],
      out_shape=[jax.ShapeDtypeStruct((tok, QKV_WIDTH), BF16),
                 jax.ShapeDtypeStruct((D4, tok // D4, QKV_WIDTH), BF16),
                 jax.ShapeDtypeStruct((D16, tok // D16, QKV_WIDTH), BF16),
                 jax.ShapeDtypeStruct((tok, SGU_WIDTH), BF16),
                 jax.ShapeDtypeStruct((tok, MEM_WIDTH), BF16)],
      scratch_shapes=[pltpu.VMEM((tm, k), BF16),
                      pltpu.VMEM((IN_SLABS, tm, HEAD_DIM), F32),
                      pltpu.VMEM((IN_SLABS, D4, tm // D4, HEAD_DIM), F32),
                      pltpu.VMEM((tm, 2 * SGU_WIDTH), F32)],
      compiler_params=_params("parallel", vmem=IN_PROJ_VMEM_LIMIT),
      name="in_proj",
  )(x, g.reshape(1, k), w, *rope, kv, g_sgu, w_sp, b_sp, g_out)


DIL_SUB = 2 * HALF
DIL_KEYS = DIL_SUB + 2 * HALF


def _dil_kernel(q_ref, kp_ref, kc_ref, kn_ref, vp_ref, vc_ref, vn_ref, o_ref, lse_ref,
                kbuf, vbuf, *, class_len_prompt, class_len_sample, prompt_rows):
  bq = q_ref.shape[0]
  row0 = pl.program_id(1) * bq
  in_prompt = row0 < prompt_rows
  clen = jnp.where(in_prompt, class_len_prompt, class_len_sample)
  pos = jnp.where(in_prompt, row0, row0 - prompt_rows) & (clen - 1)
  first = pos == 0
  last = pos + bq == clen

  kbuf[0:HALF, :] = kp_ref[...]
  kbuf[HALF:HALF + bq, :] = kc_ref[...]
  kbuf[HALF + bq:, :] = kn_ref[...]
  vbuf[0:HALF, :] = vp_ref[...]
  vbuf[HALF:HALF + bq, :] = vc_ref[...]
  vbuf[HALF + bq:, :] = vn_ref[...]

  r = lax.broadcasted_iota(jnp.int32, (DIL_SUB, DIL_KEYS), 0)
  c = lax.broadcasted_iota(jnp.int32, (DIL_SUB, DIL_KEYS), 1)
  band = jnp.where((c >= r) & (c <= r + 2 * HALF), 0.0, NEG_INF).astype(F32)
  lo = jnp.where(c < HALF, jnp.where(first, NEG_INF, 0.0), 0.0).astype(F32)
  hi = jnp.where(c >= DIL_KEYS - HALF, jnp.where(last, NEG_INF, 0.0), 0.0).astype(F32)
  lane = lax.broadcasted_iota(jnp.int32, (DIL_SUB, HEAD_DIM), 1)
  nsub = bq // DIL_SUB

  for j in range(nsub):
    bias = band
    if j == 0:
      bias = bias + lo
    if j == nsub - 1:
      bias = bias + hi
    rs = slice(j * DIL_SUB, (j + 1) * DIL_SUB)
    ks = slice(j * DIL_SUB, j * DIL_SUB + DIL_KEYS)
    lse_tile = jnp.zeros((DIL_SUB, HEAD_DIM), F32)
    for h in range(N_DIL_HEADS):
      cs = slice(h * HEAD_DIM, (h + 1) * HEAD_DIM)
      s = lax.dot_general(q_ref[rs, cs], kbuf[ks, cs], (((1,), (1,)), ((), ())),
                          preferred_element_type=F32) + bias
      m = jnp.max(s, axis=-1, keepdims=True)
      p = jnp.exp2(s - m)
      den = jnp.sum(p, axis=-1, keepdims=True)
      num = jnp.dot(p.astype(BF16), vbuf[ks, cs], preferred_element_type=F32)
      o_ref[rs, cs] = (num / den).astype(o_ref.dtype)
      lse_tile = jnp.where(lane == h, m + jnp.log2(den), lse_tile)
    lse_ref[rs, :] = lse_tile


def _dilated_branch(src, *, d, bq, prompt_rows, s_prompt, s_sample):
  rows = src.shape[1]
  hb = bq // HALF
  n_half_blocks = rows // HALF
  body = functools.partial(
      _dil_kernel, class_len_prompt=s_prompt // d, class_len_sample=s_sample // d,
      prompt_rows=prompt_rows // d)

  def cur(c):
    return pl.BlockSpec((None, bq, DIL_WIDTH), lambda r, i: (r, i, c))

  def prev(c):
    return pl.BlockSpec((None, HALF, DIL_WIDTH), lambda r, i: (r, jnp.maximum(i * hb - 1, 0), c))

  def nxt(c):
    return pl.BlockSpec((None, HALF, DIL_WIDTH),
                        lambda r, i: (r, jnp.minimum((i + 1) * hb, n_half_blocks - 1), c))

  return pl.pallas_call(
      body,
      grid=(d, rows // bq),
      in_specs=[cur(0), prev(1), cur(1), nxt(1), prev(2), cur(2), nxt(2)],
      out_specs=[pl.BlockSpec((None, bq, DIL_WIDTH), lambda r, i: (r, i, 0)),
                 pl.BlockSpec((None, bq, HEAD_DIM), lambda r, i: (r, i, 0))],
      out_shape=[jax.ShapeDtypeStruct((d, rows, DIL_WIDTH), BF16),
                 jax.ShapeDtypeStruct((d, rows, HEAD_DIM), F32)],
      scratch_shapes=[pltpu.VMEM((bq + 2 * HALF, DIL_WIDTH), BF16),
                      pltpu.VMEM((bq + 2 * HALF, DIL_WIDTH), BF16)],
      compiler_params=_params("parallel", "parallel"),
      name=f"dilated_d{d}",
  )(*([src] * 7))


def _merge_kernel(o1_ref, o4_ref, o16_ref, l1_ref, l4_ref, l16_ref, e_ref, gb_ref, b_ref,
                  s4, s16, ls4, ls16, bs):
  tm = o1_ref.shape[0]
  for rho in range(D4):
    ls4[pl.ds(rho, tm // D4, stride=D4), :] = l4_ref[rho]
  for r in range(D16):
    ls16[pl.ds(r, tm // D16, stride=D16), :] = l16_ref[r]

  l1, l2, l3 = l1_ref[...], ls4[...], ls16[...]
  m = jnp.maximum(jnp.maximum(l1, l2), l3)
  w1, w2, w3 = jnp.exp2(l1 - m), jnp.exp2(l2 - m), jnp.exp2(l3 - m)
  inv = 1.0 / (w1 + w2 + w3)

  def over_head_lanes(w):
    wn = w * inv
    hi = wn.astype(BF16)
    lo = (wn - hi.astype(F32)).astype(BF16)
    return jnp.dot(jnp.concatenate([hi, lo], axis=1), e_ref[...], preferred_element_type=F32)

  wb1, wb2, wb3 = over_head_lanes(w1), over_head_lanes(w2), over_head_lanes(w3)

  ssq = jnp.zeros((tm, 1), F32)
  for h in range(N_DIL_HEADS):
    cs = slice(h * HEAD_DIM, (h + 1) * HEAD_DIM)
    for rho in range(D4):
      s4.at[h][pl.ds(rho, tm // D4, stride=D4), :] = o4_ref[rho, :, cs].astype(F32)
    for r in range(D16):
      s16.at[h][pl.ds(r, tm // D16, stride=D16), :] = o16_ref[r, :, cs].astype(F32)
    b = wb1[:, cs] * o1_ref[:, cs].astype(F32) + wb2[:, cs] * s4[h] + wb3[:, cs] * s16[h]
    ssq = ssq + jnp.sum(b * b, axis=-1, keepdims=True)
    bs[h] = b
  scale = jnp.broadcast_to(lax.rsqrt(ssq * (1.0 / DIL_WIDTH) + EPS), (tm, HEAD_DIM))
  for h in range(N_DIL_HEADS):
    cs = slice(h * HEAD_DIM, (h + 1) * HEAD_DIM)
    b_ref[:, cs] = (bs[h] * scale * gb_ref[:, cs]).astype(b_ref.dtype)


def _merge(o_list, lse_list, g_b, *, tm):
  o1, o4, o16 = o_list
  l1, l4, l16 = lse_list
  tok = o1.shape[1]
  head_of_lane = jnp.arange(DIL_WIDTH, dtype=jnp.int32) // HEAD_DIM
  selector = (jnp.arange(HEAD_DIM, dtype=jnp.int32)[:, None] == head_of_lane[None, :]).astype(BF16)
  selector = jnp.concatenate([selector, selector], axis=0)
  head_scratch = pltpu.VMEM((N_DIL_HEADS, tm, HEAD_DIM), F32)
  return pl.pallas_call(
      _merge_kernel,
      grid=(tok // tm,),
      in_specs=[pl.BlockSpec((None, tm, DIL_WIDTH), lambda i: (0, i, 0)),
                pl.BlockSpec((D4, tm // D4, DIL_WIDTH), lambda i: (0, i, 0)),
                pl.BlockSpec((D16, tm // D16, DIL_WIDTH), lambda i: (0, i, 0)),
                pl.BlockSpec((None, tm, HEAD_DIM), lambda i: (0, i, 0)),
                pl.BlockSpec((D4, tm // D4, HEAD_DIM), lambda i: (0, i, 0)),
                pl.BlockSpec((D16, tm // D16, HEAD_DIM), lambda i: (0, i, 0)),
                pl.BlockSpec((2 * HEAD_DIM, DIL_WIDTH), lambda i: (0, 0)),
                pl.BlockSpec((1, DIL_WIDTH), lambda i: (0, 0))],
      out_specs=pl.BlockSpec((tm, DIL_WIDTH), lambda i: (i, 0)),
      out_shape=jax.ShapeDtypeStruct((tok, DIL_WIDTH), BF16),
      scratch_shapes=[head_scratch, head_scratch,
                      pltpu.VMEM((tm, HEAD_DIM), F32), pltpu.VMEM((tm, HEAD_DIM), F32),
                      head_scratch],
      compiler_params=_params("parallel"),
      name="merge",
  )(o1, o4, o16, l1, l4, l16, selector, g_b)


def _out_proj_kernel(a_ref, b_ref, c_ref, w_ref, x_ref, y_ref):
  b0 = SGU_WIDTH
  c0 = SGU_WIDTH + DIL_WIDTH
  acc = jnp.dot(a_ref[...], w_ref[0:b0, :], preferred_element_type=F32)
  acc += jnp.dot(b_ref[...], w_ref[b0:c0, :], preferred_element_type=F32)
  acc += jnp.dot(c_ref[...], w_ref[c0:, :], preferred_element_type=F32)
  y_ref[...] = x_ref[...] + acc


def _out_proj(a_n, b_n, c_n, w_out, layer, x, *, tm, tn):
  tok = x.shape[0]
  row = lambda w: pl.BlockSpec((tm, w), lambda i, j: (i, 0))
  return pl.pallas_call(
      _out_proj_kernel,
      grid=(tok // tm, D_MODEL // tn),
      in_specs=[row(SGU_WIDTH), row(DIL_WIDTH), row(MEM_WIDTH),
                pl.BlockSpec((None, MIX_WIDTH, tn), lambda i, j: (layer, 0, j)),
                pl.BlockSpec((tm, tn), lambda i, j: (i, j))],
      out_specs=pl.BlockSpec((tm, tn), lambda i, j: (i, j)),
      out_shape=jax.ShapeDtypeStruct((tok, D_MODEL), F32),
      compiler_params=_params("parallel", "arbitrary"),
      name="out_proj",
  )(a_n, b_n, c_n, w_out, x)


def _ffn_kernel(x_ref, g_ref, wg_ref, wu_ref, wd_ref, gf_ref, y_ref, h_ref, *, final_norm):
  k = pl.program_id(1)

  @pl.when(k == 0)
  def _():
    xf = x_ref[...]
    h_ref[...] = (xf * _rms_scale(xf) * g_ref[...]).astype(BF16)
    y_ref[...] = xf

  h = h_ref[...]
  gate = jnp.dot(h, wg_ref[...], preferred_element_type=F32)
  up = jnp.dot(h, wu_ref[...], preferred_element_type=F32)
  act = (jax.nn.silu(gate) * up).astype(BF16)
  y_ref[...] += jnp.dot(act, wd_ref[...], preferred_element_type=F32)

  if final_norm:
    @pl.when(k == pl.num_programs(1) - 1)
    def _():
      y = y_ref[...]
      y_ref[...] = y * _rms_scale(y) * gf_ref[...]


def _ffn(x, g, w_gu, w_down, layer, g_final, *, tm, th, final_norm):
  tok = x.shape[0]
  nk = FFN_HIDDEN // th
  return pl.pallas_call(
      functools.partial(_ffn_kernel, final_norm=final_norm),
      grid=(tok // tm, nk),
      in_specs=[
          pl.BlockSpec((tm, D_MODEL), lambda i, k: (i, 0), pipeline_mode=pl.Buffered(1)),
          pl.BlockSpec((1, D_MODEL), lambda i, k: (0, 0)),
          pl.BlockSpec((None, D_MODEL, th), lambda i, k: (layer, 0, k)),
          pl.BlockSpec((None, D_MODEL, th), lambda i, k: (layer, 0, nk + k)),
          pl.BlockSpec((None, th, D_MODEL), lambda i, k: (layer, k, 0)),
          pl.BlockSpec((1, D_MODEL), lambda i, k: (0, 0)),
      ],
      out_specs=pl.BlockSpec((tm, D_MODEL), lambda i, k: (i, 0)),
      out_shape=jax.ShapeDtypeStruct((tok, D_MODEL), F32),
      scratch_shapes=[pltpu.VMEM((tm, D_MODEL), BF16)],
      compiler_params=_params("parallel", "arbitrary"),
      name="ffn",
  )(x, g, w_gu, w_gu, w_down, g_final)


def _rope_tables(positions):
  inv = ROPE_THETA ** (-jnp.arange(0, ROPE_DIM, 2, dtype=F32) / ROPE_DIM)
  ang = positions.astype(F32)[:, None] * inv[None, :]
  cos, sin = jnp.cos(ang), jnp.sin(ang)
  n = positions.shape[0]
  rest = HEAD_DIM - ROPE_DIM
  c = jnp.concatenate([cos, cos, jnp.ones((n, rest), F32)], axis=1)
  s1 = jnp.concatenate([jnp.zeros((n, ROPE_HALF), F32), sin, jnp.zeros((n, rest), F32)], axis=1)
  s2 = jnp.concatenate([-sin, jnp.zeros((n, HEAD_DIM - ROPE_HALF), F32)], axis=1)
  return c, s1, s2


def kernel(x_prompt, x_sample, mem_prompt, mem_sample, g_mix_norm, w_in, g_sgu, w_spatial,
           b_spatial, g_mem_norm, w_mem_kv, g_group_out, w_out, g_ffn_norm, w_gate_up,
           w_down, g_final):
  n_prompt, s_prompt, _ = x_prompt.shape
  n_sample, s_sample, _ = x_sample.shape
  assert n_prompt == 1 and s_prompt % s_sample == 0
  assert s_sample % (D16 * BQ_DIL) == 0 and s_sample % TM_OUT_PROJ == 0
  prompt_rows = n_prompt * s_prompt
  depth = w_in.shape[0]

  x = jnp.concatenate([x_prompt.reshape(prompt_rows, D_MODEL),
                       x_sample.reshape(n_sample * s_sample, D_MODEL)], axis=0)
  mem = jnp.concatenate([mem_prompt, mem_sample], axis=0)
  n_mems = mem.shape[0]
  mem = mem.reshape(n_mems * N_MEM, D_MODEL)
  positions = jnp.concatenate([jnp.arange(s_prompt, dtype=jnp.int32)] * n_prompt
                              + [jnp.arange(s_sample, dtype=jnp.int32)] * n_sample)
  rope = _rope_tables(positions)
  seqs = dict(prompt_rows=prompt_rows, s_prompt=s_prompt, s_sample=s_sample)
  w_in, w_spatial, w_mem_kv, w_out, w_gate_up, w_down = (
      w.astype(BF16) for w in (w_in, w_spatial, w_mem_kv, w_out, w_gate_up, w_down))

  for l in range(depth):
    kv = _norm_matmul(mem, g_mem_norm[l], w_mem_kv, l, tm=N_MEM, tn=MEM_WIDTH,
                      name="mem_kv").reshape(n_mems, N_MEM, 2 * MEM_WIDTH)
    g_out = g_group_out[l].reshape(1, MIX_WIDTH)
    b_sp = jnp.broadcast_to(b_spatial[l][:, :, None], (N_SGU_GROUPS, SGU_CHUNK, HEAD_DIM))
    zq, c4, c16, a_n, c_n = _in_proj(
        x, g_mix_norm[l], w_in, l, rope, kv, g_sgu[l].reshape(1, SGU_WIDTH), w_spatial, b_sp,
        g_out, tm=TM_IN_PROJ, rows_per_mem=s_sample,
        first_sample_mem_tile=prompt_rows // s_sample)
    o_list, lse_list = [], []
    for d, src in ((1, zq.reshape(1, *zq.shape)), (D4, c4), (D16, c16)):
      o, lse = _dilated_branch(src, d=d, bq=BQ_DIL, **seqs)
      o_list.append(o)
      lse_list.append(lse)
    b_n = _merge(o_list, lse_list, g_out[:, SGU_WIDTH:SGU_WIDTH + DIL_WIDTH], tm=TM_MERGE)
    x = _out_proj(a_n, b_n, c_n, w_out, l, x, tm=TM_OUT_PROJ, tn=TN_OUT_PROJ)
    x = _ffn(x, g_ffn_norm[l].reshape(1, D_MODEL), w_gate_up, w_down, l,
             g_final.reshape(1, D_MODEL), tm=TM_FFN, th=TH_FFN, final_norm=(l == depth - 1))

  y_prompt = x[:prompt_rows].reshape(n_prompt, s_prompt, D_MODEL)
  y_sample = x[prompt_rows:].reshape(n_sample, s_sample, D_MODEL)
  return (y_prompt, y_sample)
```

```python
import functools
import math

import jax
import jax.numpy as jnp
from jax import lax
from jax.experimental import pallas as pl
from jax.experimental.pallas import tpu as pltpu

F32 = jnp.float32
BF16 = jnp.bfloat16

D_MODEL = 2048
HEAD_DIM = 128
N_SGU_GROUPS = 4
SGU_WIDTH = N_SGU_GROUPS * HEAD_DIM
SGU_CHUNK = 128
N_DIL_HEADS = 8
DIL_WIDTH = N_DIL_HEADS * HEAD_DIM
DILATIONS = (1, 4, 16)
HALF = 64
N_MEM_HEADS = 4
MEM_WIDTH = N_MEM_HEADS * HEAD_DIM
N_MEM = 256
MIX_WIDTH = SGU_WIDTH + DIL_WIDTH + MEM_WIDTH
IN_WIDTH = 2 * SGU_WIDTH + 3 * DIL_WIDTH + MEM_WIDTH
ROPE_THETA = 500000.0
ROPE_DIM = HEAD_DIM // 4
ROPE_HALF = ROPE_DIM // 2
FFN_HIDDEN = 5632
EPS = 1e-6
NEG_INF = -1e30
ATTN_SCALE = HEAD_DIM ** -0.5
DIL_Q_SCALE = ATTN_SCALE * math.log2(math.e)

V7X_VMEM_BYTES = 64 * 1024 * 1024
VMEM_LIMIT = V7X_VMEM_BYTES - 8 * 1024 * 1024

IN_CHUNK_N = 256
IN_SLABS = 4
IN_PROJ_VMEM_LIMIT = V7X_VMEM_BYTES - 3 * 1024 * 1024
IN_V0 = SGU_WIDTH
IN_QKV0 = 2 * SGU_WIDTH
IN_QC0 = IN_QKV0 + 3 * DIL_WIDTH
QKV_WIDTH = 3 * DIL_WIDTH
D4, D16 = DILATIONS[1], DILATIONS[2]
D16_PER_D4 = D16 // D4

TM_IN_PROJ = 512
TM_MIX_OUT = 512
OUT_CHUNK_N = 256
TM_FFN = 1024
TH_FFN = 512
BQ_DIL_MAX = 512


def _params(*sem, vmem=VMEM_LIMIT):
  return pltpu.CompilerParams(dimension_semantics=sem, vmem_limit_bytes=vmem)


def _rms_scale(x):
  return lax.rsqrt(jnp.mean(x * x, axis=-1, keepdims=True) + EPS)


def _norm_matmul_kernel(x_ref, g_ref, w_ref, o_ref, h_ref):
  @pl.when(pl.program_id(1) == 0)
  def _():
    xf = x_ref[...]
    h_ref[...] = (xf * _rms_scale(xf) * g_ref[...]).astype(BF16)

  o_ref[...] = jnp.dot(h_ref[...], w_ref[...], preferred_element_type=F32).astype(o_ref.dtype)


def _norm_matmul(x, g, w, layer, *, tm, tn, name):
  m, k = x.shape
  n = w.shape[2]
  return pl.pallas_call(
      _norm_matmul_kernel,
      grid=(m // tm, n // tn),
      in_specs=[pl.BlockSpec((tm, k), lambda i, j: (i, 0)),
                pl.BlockSpec((1, k), lambda i, j: (0, 0)),
                pl.BlockSpec((None, k, tn), lambda i, j: (layer, 0, j))],
      out_specs=pl.BlockSpec((tm, tn), lambda i, j: (i, j)),
      out_shape=jax.ShapeDtypeStruct((m, n), BF16),
      scratch_shapes=[pltpu.VMEM((tm, k), BF16)],
      compiler_params=_params("parallel", "arbitrary"),
      name=name,
  )(x, g.reshape(1, k), w)


def _in_col_kind(col):
  bounds = ((IN_V0, "u"), (IN_QKV0, "v"), (IN_QKV0 + DIL_WIDTH, "q"),
            (IN_QKV0 + 2 * DIL_WIDTH, "k"), (IN_QC0, "vb"))
  for end, kind in bounds:
    if col < end:
      return kind
  return "qc"


def _mem_attention(q_heads, kv_ref, go_ref, o_ref):
  outs = []
  ssq = jnp.zeros((q_heads[0].shape[0], 1), F32)
  for h, q in enumerate(q_heads):
    k = kv_ref[:, h * HEAD_DIM:(h + 1) * HEAD_DIM]
    v = kv_ref[:, MEM_WIDTH + h * HEAD_DIM:MEM_WIDTH + (h + 1) * HEAD_DIM]
    s = lax.dot_general(q, k, (((1,), (1,)), ((), ())), preferred_element_type=F32) * ATTN_SCALE
    m = jnp.max(s, axis=-1, keepdims=True)
    p = jnp.exp(s - m)
    den = jnp.sum(p, axis=-1, keepdims=True)
    o = jnp.dot(p.astype(BF16), v, preferred_element_type=F32) / den
    ssq = ssq + jnp.sum(o * o, axis=-1, keepdims=True)
    outs.append(o)
  scale = lax.rsqrt(ssq * (1.0 / MEM_WIDTH) + EPS)
  for h in range(N_MEM_HEADS):
    cs = slice(h * HEAD_DIM, (h + 1) * HEAD_DIM)
    o_ref[:, cs] = (outs[h] * scale * go_ref[:, cs]).astype(o_ref.dtype)


def _spatial_gating(uv_ref, gs_ref, w_ref, b_ref, go_ref, o_ref):
  for c in range(uv_ref.shape[0] // SGU_CHUNK):
    rs = slice(c * SGU_CHUNK, (c + 1) * SGU_CHUNK)
    outs = []
    ssq = jnp.zeros((SGU_CHUNK, 1), F32)
    for g in range(N_SGU_GROUPS):
      cs = slice(g * HEAD_DIM, (g + 1) * HEAD_DIM)
      v = uv_ref[rs, IN_V0 + g * HEAD_DIM:IN_V0 + (g + 1) * HEAD_DIM]
      vv = v * _rms_scale(v) * gs_ref[:, cs]
      vs = jnp.dot(w_ref[g], vv.astype(BF16), preferred_element_type=F32) + b_ref[g]
      a = uv_ref[rs, cs] * vs
      ssq = ssq + jnp.sum(a * a, axis=-1, keepdims=True)
      outs.append(a)
    scale = lax.rsqrt(ssq * (1.0 / SGU_WIDTH) + EPS)
    for g in range(N_SGU_GROUPS):
      cs = slice(g * HEAD_DIM, (g + 1) * HEAD_DIM)
      o_ref[rs, cs] = (outs[g] * scale * go_ref[:, cs]).astype(o_ref.dtype)


def _in_proj_kernel(x_ref, g_ref, w_ref, c_ref, s1_ref, s2_ref, kv_ref, gs_ref, wsp_ref, bsp_ref,
                    go_ref, z_ref, c4_ref, c16_ref, a_ref, cn_ref, h_ref, slab, slab4, uv_ref):
  tm = x_ref.shape[0]
  xf = x_ref[...]
  h_ref[...] = (xf * _rms_scale(xf) * g_ref[...]).astype(BF16)
  c, s1, s2 = c_ref[...], s1_ref[...], s2_ref[...]
  tables = {"k": (c, s1, s2), "q": (c * DIL_Q_SCALE, s1 * DIL_Q_SCALE, s2 * DIL_Q_SCALE)}
  n_slabs = slab.shape[0]
  heads_per_chunk = IN_CHUNK_N // HEAD_DIM
  n_chunks = IN_WIDTH // IN_CHUNK_N
  first_qc, first_qkv = IN_QC0 // IN_CHUNK_N, IN_QKV0 // IN_CHUNK_N
  order = (list(range(first_qc, n_chunks)) + list(range(first_qkv))
           + list(range(first_qkv, first_qc)))
  qc_heads = []
  for chunk in order:
    col0 = chunk * IN_CHUNK_N
    acc = jnp.dot(h_ref[...], w_ref[:, col0:col0 + IN_CHUNK_N], preferred_element_type=F32)
    for hh in range(heads_per_chunk):
      col = col0 + hh * HEAD_DIM
      kind = _in_col_kind(col)
      t = acc[:, hh * HEAD_DIM:(hh + 1) * HEAD_DIM]
      if kind == "qc":
        qc_heads.append(t.astype(BF16))
        continue
      if kind in ("u", "v"):
        uv_ref[:, col:col + HEAD_DIM] = jax.nn.gelu(t)
        continue
      if kind in tables:
        tc, ts1, ts2 = tables[kind]
        t = (t * tc + pltpu.roll(t, ROPE_HALF, 1) * ts1
             + pltpu.roll(t, HEAD_DIM - ROPE_HALF, 1) * ts2)
      cs = slice(col - IN_QKV0, col - IN_QKV0 + HEAD_DIM)
      z_ref[:, cs] = t.astype(z_ref.dtype)
      sl = (chunk * heads_per_chunk + hh) % n_slabs
      slab[sl] = t
      for rho in range(D4):
        t4 = slab.at[sl][pl.ds(rho, tm // D4, stride=D4), :]
        c4_ref[rho, :, cs] = t4.astype(c4_ref.dtype)
        slab4[sl, rho] = t4
        for q in range(D16_PER_D4):
          t16 = slab4.at[sl, rho][pl.ds(q, tm // D16, stride=D16_PER_D4), :]
          c16_ref[D4 * q + rho, :, cs] = t16.astype(c16_ref.dtype)
    if chunk == n_chunks - 1:
      _mem_attention(qc_heads, kv_ref, go_ref.at[:, SGU_WIDTH + DIL_WIDTH:], cn_ref)
    if chunk == first_qkv - 1:
      _spatial_gating(uv_ref, gs_ref, wsp_ref, bsp_ref, go_ref.at[:, :SGU_WIDTH], a_ref)


def _in_proj(x, g, w, layer, rope, kv, g_sgu, w_sp, b_sp, g_out, *, tm, rows_per_mem,
             first_sample_mem_tile):
  tok, k = x.shape
  tiles_per_mem = rows_per_mem // tm

  def kv_map(i):
    return (jnp.maximum(i // tiles_per_mem - first_sample_mem_tile + 1, 0), 0, 0)

  const = lambda shape: pl.BlockSpec(shape, lambda i: (0,) * len(shape))
  return pl.pallas_call(
      _in_proj_kernel,
      grid=(tok // tm,),
      in_specs=[pl.BlockSpec((tm, k), lambda i: (i, 0)),
                const((1, k)),
                pl.BlockSpec((None, k, IN_WIDTH), lambda i: (layer, 0, 0),
                             pipeline_mode=pl.Buffered(1))]
      + [pl.BlockSpec((tm, HEAD_DIM), lambda i: (i, 0))] * 3
      + [pl.BlockSpec((None, N_MEM, 2 * MEM_WIDTH), kv_map),
         const((1, SGU_WIDTH)),
         pl.BlockSpec((None, N_SGU_GROUPS, SGU_CHUNK, SGU_CHUNK), lambda i: (layer, 0, 0, 0)),
         const((N_SGU_GROUPS, SGU_CHUNK, HEAD_DIM)),
         const((1, MIX_WIDTH))],
      out_specs=[pl.BlockSpec((tm, QKV_WIDTH), lambda i: (i, 0)),
                 pl.BlockSpec((D4, tm // D4, QKV_WIDTH), lambda i: (0, i, 0)),
                 pl.BlockSpec((D16, tm // D16, QKV_WIDTH), lambda i: (0, i, 0)),
                 pl.BlockSpec((tm, SGU_WIDTH), lambda i: (i, 0)),
                 pl.BlockSpec((tm, MEM_WIDTH), lambda i: (i, 0))],
      out_shape=[jax.ShapeDtypeStruct((tok, QKV_WIDTH), BF16),
                 jax.ShapeDtypeStruct((D4, tok // D4, QKV_WIDTH), BF16),
                 jax.ShapeDtypeStruct((D16, tok // D16, QKV_WIDTH), BF16),
                 jax.ShapeDtypeStruct((tok, SGU_WIDTH), BF16),
                 jax.ShapeDtypeStruct((tok, MEM_WIDTH), BF16)],
      scratch_shapes=[pltpu.VMEM((tm, k), BF16),
                      pltpu.VMEM((IN_SLABS, tm, HEAD_DIM), F32),
                      pltpu.VMEM((IN_SLABS, D4, tm // D4, HEAD_DIM), F32),
                      pltpu.VMEM((tm, 2 * SGU_WIDTH), F32)],
      compiler_params=_params("parallel", vmem=IN_PROJ_VMEM_LIMIT),
      name="in_proj",
  )(x, g.reshape(1, k), w, *rope, kv, g_sgu, w_sp, b_sp, g_out)


DIL_SUB = 2 * HALF
DIL_KEYS = DIL_SUB + 2 * HALF


def _dil_kernel(q_ref, kp_ref, kc_ref, kn_ref, vp_ref, vc_ref, vn_ref, o_ref, lse_ref,
                kbuf, vbuf, *, class_len_prompt, class_len_sample, prompt_rows):
  bq = q_ref.shape[0]
  row0 = pl.program_id(1) * bq
  in_prompt = row0 < prompt_rows
  clen = jnp.where(in_prompt, class_len_prompt, class_len_sample)
  pos = jnp.where(in_prompt, row0, row0 - prompt_rows) & (clen - 1)
  first = pos == 0
  last = pos + bq == clen

  kbuf[0:HALF, :] = kp_ref[...]
  kbuf[HALF:HALF + bq, :] = kc_ref[...]
  kbuf[HALF + bq:, :] = kn_ref[...]
  vbuf[0:HALF, :] = vp_ref[...]
  vbuf[HALF:HALF + bq, :] = vc_ref[...]
  vbuf[HALF + bq:, :] = vn_ref[...]

  r = lax.broadcasted_iota(jnp.int32, (DIL_SUB, DIL_KEYS), 0)
  c = lax.broadcasted_iota(jnp.int32, (DIL_SUB, DIL_KEYS), 1)
  band = jnp.where((c >= r) & (c <= r + 2 * HALF), 0.0, NEG_INF).astype(F32)
  lo = jnp.where(c < HALF, jnp.where(first, NEG_INF, 0.0), 0.0).astype(F32)
  hi = jnp.where(c >= DIL_KEYS - HALF, jnp.where(last, NEG_INF, 0.0), 0.0).astype(F32)
  lane = lax.broadcasted_iota(jnp.int32, (DIL_SUB, HEAD_DIM), 1)
  nsub = bq // DIL_SUB

  for j in range(nsub):
    bias = band
    if j == 0:
      bias = bias + lo
    if j == nsub - 1:
      bias = bias + hi
    rs = slice(j * DIL_SUB, (j + 1) * DIL_SUB)
    ks = slice(j * DIL_SUB, j * DIL_SUB + DIL_KEYS)
    lse_tile = jnp.zeros((DIL_SUB, HEAD_DIM), F32)
    for h in range(N_DIL_HEADS):
      cs = slice(h * HEAD_DIM, (h + 1) * HEAD_DIM)
      s = lax.dot_general(q_ref[rs, cs], kbuf[ks, cs], (((1,), (1,)), ((), ())),
                          preferred_element_type=F32) + bias
      m = jnp.max(s, axis=-1, keepdims=True)
      p = jnp.exp2(s - m)
      den = jnp.sum(p, axis=-1, keepdims=True)
      num = jnp.dot(p.astype(BF16), vbuf[ks, cs], preferred_element_type=F32)
      o_ref[rs, cs] = (num / den).astype(o_ref.dtype)
      lse_tile = jnp.where(lane == h, m + jnp.log2(den), lse_tile)
    lse_ref[rs, :] = lse_tile


def _dilated_branch(src, *, d, bq, prompt_rows, s_prompt, s_sample):
  rows = src.shape[1]
  hb = bq // HALF
  n_half_blocks = rows // HALF
  body = functools.partial(
      _dil_kernel, class_len_prompt=s_prompt // d, class_len_sample=s_sample // d,
      prompt_rows=prompt_rows // d)

  def cur(c):
    return pl.BlockSpec((None, bq, DIL_WIDTH), lambda r, i: (r, i, c))

  def prev(c):
    return pl.BlockSpec((None, HALF, DIL_WIDTH), lambda r, i: (r, jnp.maximum(i * hb - 1, 0), c))

  def nxt(c):
    return pl.BlockSpec((None, HALF, DIL_WIDTH),
                        lambda r, i: (r, jnp.minimum((i + 1) * hb, n_half_blocks - 1), c))

  return pl.pallas_call(
      body,
      grid=(d, rows // bq),
      in_specs=[cur(0), prev(1), cur(1), nxt(1), prev(2), cur(2), nxt(2)],
      out_specs=[pl.BlockSpec((None, bq, DIL_WIDTH), lambda r, i: (r, i, 0)),
                 pl.BlockSpec((None, bq, HEAD_DIM), lambda r, i: (r, i, 0))],
      out_shape=[jax.ShapeDtypeStruct((d, rows, DIL_WIDTH), BF16),
                 jax.ShapeDtypeStruct((d, rows, HEAD_DIM), F32)],
      scratch_shapes=[pltpu.VMEM((bq + 2 * HALF, DIL_WIDTH), BF16),
                      pltpu.VMEM((bq + 2 * HALF, DIL_WIDTH), BF16)],
      compiler_params=_params("parallel", "parallel"),
      name=f"dilated_d{d}",
  )(*([src] * 7))


def _merge_tile(o1_ref, o4_ref, o16_ref, l1_ref, l4_ref, l16_ref, e_ref, gb_ref, b_ref,
                s4, s16, ls4, ls16, bs):
  tm = o1_ref.shape[0]
  for rho in range(D4):
    ls4[pl.ds(rho, tm // D4, stride=D4), :] = l4_ref[rho]
  for r in range(D16):
    ls16[pl.ds(r, tm // D16, stride=D16), :] = l16_ref[r]

  l1, l2, l3 = l1_ref[...], ls4[...], ls16[...]
  m = jnp.maximum(jnp.maximum(l1, l2), l3)
  w1, w2, w3 = jnp.exp2(l1 - m), jnp.exp2(l2 - m), jnp.exp2(l3 - m)
  inv = 1.0 / (w1 + w2 + w3)

  def over_head_lanes(w):
    wn = w * inv
    hi = wn.astype(BF16)
    lo = (wn - hi.astype(F32)).astype(BF16)
    return jnp.dot(jnp.concatenate([hi, lo], axis=1), e_ref[...], preferred_element_type=F32)

  wb1, wb2, wb3 = over_head_lanes(w1), over_head_lanes(w2), over_head_lanes(w3)

  ssq = jnp.zeros((tm, 1), F32)
  for h in range(N_DIL_HEADS):
    cs = slice(h * HEAD_DIM, (h + 1) * HEAD_DIM)
    for rho in range(D4):
      s4.at[h][pl.ds(rho, tm // D4, stride=D4), :] = o4_ref[rho, :, cs].astype(F32)
    for r in range(D16):
      s16.at[h][pl.ds(r, tm // D16, stride=D16), :] = o16_ref[r, :, cs].astype(F32)
    b = wb1[:, cs] * o1_ref[:, cs].astype(F32) + wb2[:, cs] * s4[h] + wb3[:, cs] * s16[h]
    ssq = ssq + jnp.sum(b * b, axis=-1, keepdims=True)
    bs[h] = b
  scale = jnp.broadcast_to(lax.rsqrt(ssq * (1.0 / DIL_WIDTH) + EPS), (tm, HEAD_DIM))
  for h in range(N_DIL_HEADS):
    cs = slice(h * HEAD_DIM, (h + 1) * HEAD_DIM)
    b_ref[:, cs] = (bs[h] * scale * gb_ref[:, cs]).astype(b_ref.dtype)


def _mix_out_kernel(o1_ref, o4_ref, o16_ref, l1_ref, l4_ref, l16_ref, e_ref, gb_ref,
                    a_ref, c_ref, w_ref, x_ref, y_ref, s4, s16, ls4, ls16, bs, bn):
  step = pl.program_id(0)
  b0 = SGU_WIDTH
  c0 = SGU_WIDTH + DIL_WIDTH

  def merge_into(slot):
    _merge_tile(o1_ref, o4_ref, o16_ref, l1_ref, l4_ref, l16_ref, e_ref, gb_ref, bn.at[slot],
                s4, s16, ls4, ls16, bs)

  @pl.when(step == 0)
  def _():
    merge_into(0)

  @pl.when(step > 0)
  def _():
    slot = step % 2
    merge_into(slot)
    b_prev = bn.at[1 - slot]
    for chunk in range(D_MODEL // OUT_CHUNK_N):
      cols = slice(chunk * OUT_CHUNK_N, (chunk + 1) * OUT_CHUNK_N)
      acc = jnp.dot(a_ref[...], w_ref[0:b0, cols], preferred_element_type=F32)
      acc += jnp.dot(b_prev[...], w_ref[b0:c0, cols], preferred_element_type=F32)
      acc += jnp.dot(c_ref[...], w_ref[c0:, cols], preferred_element_type=F32)
      y_ref[:, cols] = x_ref[:, cols] + acc


def _mix_out(o_list, lse_list, g_b, a_n, c_n, w_out, layer, x, *, tm):
  o1, o4, o16 = o_list
  l1, l4, l16 = lse_list
  tok = x.shape[0]
  n = tok // tm
  head_of_lane = jnp.arange(DIL_WIDTH, dtype=jnp.int32) // HEAD_DIM
  selector = (jnp.arange(HEAD_DIM, dtype=jnp.int32)[:, None] == head_of_lane[None, :]).astype(BF16)
  selector = jnp.concatenate([selector, selector], axis=0)
  head_scratch = pltpu.VMEM((N_DIL_HEADS, tm, HEAD_DIM), F32)
  ahead = lambda s: jnp.minimum(s, n - 1)
  behind = lambda s: jnp.maximum(s - 1, 0)
  const = lambda shape: pl.BlockSpec(shape, lambda s: (0,) * len(shape))
  row = lambda w: pl.BlockSpec((tm, w), lambda s: (behind(s), 0))
  return pl.pallas_call(
      _mix_out_kernel,
      grid=(n + 1,),
      in_specs=[pl.BlockSpec((None, tm, DIL_WIDTH), lambda s: (0, ahead(s), 0)),
                pl.BlockSpec((D4, tm // D4, DIL_WIDTH), lambda s: (0, ahead(s), 0)),
                pl.BlockSpec((D16, tm // D16, DIL_WIDTH), lambda s: (0, ahead(s), 0)),
                pl.BlockSpec((None, tm, HEAD_DIM), lambda s: (0, ahead(s), 0)),
                pl.BlockSpec((D4, tm // D4, HEAD_DIM), lambda s: (0, ahead(s), 0)),
                pl.BlockSpec((D16, tm // D16, HEAD_DIM), lambda s: (0, ahead(s), 0)),
                const((2 * HEAD_DIM, DIL_WIDTH)),
                const((1, DIL_WIDTH)),
                row(SGU_WIDTH), row(MEM_WIDTH),
                pl.BlockSpec((None, MIX_WIDTH, D_MODEL), lambda s: (layer, 0, 0),
                             pipeline_mode=pl.Buffered(1)),
                row(D_MODEL)],
      out_specs=row(D_MODEL),
      out_shape=jax.ShapeDtypeStruct((tok, D_MODEL), F32),
      scratch_shapes=[head_scratch, head_scratch,
                      pltpu.VMEM((tm, HEAD_DIM), F32), pltpu.VMEM((tm, HEAD_DIM), F32),
                      head_scratch, pltpu.VMEM((2, tm, DIL_WIDTH), BF16)],
      compiler_params=_params("arbitrary"),
      name="mix_out",
  )(o1, o4, o16, l1, l4, l16, selector, g_b, a_n, c_n, w_out, x)


def _ffn_kernel(x_ref, g_ref, wg_ref, wu_ref, wd_ref, gf_ref, y_ref, h_ref, *, final_norm):
  k = pl.program_id(1)

  @pl.when(k == 0)
  def _():
    xf = x_ref[...]
    h_ref[...] = (xf * _rms_scale(xf) * g_ref[...]).astype(BF16)
    y_ref[...] = xf

  h = h_ref[...]
  gate = jnp.dot(h, wg_ref[...], preferred_element_type=F32)
  up = jnp.dot(h, wu_ref[...], preferred_element_type=F32)
  act = (jax.nn.silu(gate) * up).astype(BF16)
  y_ref[...] += jnp.dot(act, wd_ref[...], preferred_element_type=F32)

  if final_norm:
    @pl.when(k == pl.num_programs(1) - 1)
    def _():
      y = y_ref[...]
      y_ref[...] = y * _rms_scale(y) * gf_ref[...]


def _ffn(x, g, w_gu, w_down, layer, g_final, *, tm, th, final_norm, row0=0, rows=None):
  tok = x.shape[0] if rows is None else rows
  tile0 = row0 // tm
  nk = FFN_HIDDEN // th
  return pl.pallas_call(
      functools.partial(_ffn_kernel, final_norm=final_norm),
      grid=(tok // tm, nk),
      in_specs=[
          pl.BlockSpec((tm, D_MODEL), lambda i, k: (tile0 + i, 0), pipeline_mode=pl.Buffered(1)),
          pl.BlockSpec((1, D_MODEL), lambda i, k: (0, 0)),
          pl.BlockSpec((None, D_MODEL, th), lambda i, k: (layer, 0, k)),
          pl.BlockSpec((None, D_MODEL, th), lambda i, k: (layer, 0, nk + k)),
          pl.BlockSpec((None, th, D_MODEL), lambda i, k: (layer, k, 0)),
          pl.BlockSpec((1, D_MODEL), lambda i, k: (0, 0)),
      ],
      out_specs=pl.BlockSpec((tm, D_MODEL), lambda i, k: (i, 0)),
      out_shape=jax.ShapeDtypeStruct((tok, D_MODEL), F32),
      scratch_shapes=[pltpu.VMEM((tm, D_MODEL), BF16)],
      compiler_params=_params("parallel", "arbitrary"),
      name="ffn",
  )(x, g, w_gu, w_gu, w_down, g_final)


def _rope_tables(positions):
  inv = ROPE_THETA ** (-jnp.arange(0, ROPE_DIM, 2, dtype=F32) / ROPE_DIM)
  ang = positions.astype(F32)[:, None] * inv[None, :]
  cos, sin = jnp.cos(ang), jnp.sin(ang)
  n = positions.shape[0]
  rest = HEAD_DIM - ROPE_DIM
  c = jnp.concatenate([cos, cos, jnp.ones((n, rest), F32)], axis=1)
  s1 = jnp.concatenate([jnp.zeros((n, ROPE_HALF), F32), sin, jnp.zeros((n, rest), F32)], axis=1)
  s2 = jnp.concatenate([-sin, jnp.zeros((n, HEAD_DIM - ROPE_HALF), F32)], axis=1)
  return c, s1, s2


def kernel(x_prompt, x_sample, mem_prompt, mem_sample, g_mix_norm, w_in, g_sgu, w_spatial,
           b_spatial, g_mem_norm, w_mem_kv, g_group_out, w_out, g_ffn_norm, w_gate_up,
           w_down, g_final):
  n_prompt, s_prompt, _ = x_prompt.shape
  n_sample, s_sample, _ = x_sample.shape
  assert n_prompt == 1 and s_prompt % s_sample == 0
  assert s_sample % (D16 * DIL_SUB) == 0 and s_sample % TM_FFN == 0
  prompt_rows = n_prompt * s_prompt
  depth = w_in.shape[0]

  x = jnp.concatenate([x_prompt.reshape(prompt_rows, D_MODEL),
                       x_sample.reshape(n_sample * s_sample, D_MODEL)], axis=0)
  mem = jnp.concatenate([mem_prompt, mem_sample], axis=0)
  n_mems = mem.shape[0]
  mem = mem.reshape(n_mems * N_MEM, D_MODEL)
  positions = jnp.concatenate([jnp.arange(s_prompt, dtype=jnp.int32)] * n_prompt
                              + [jnp.arange(s_sample, dtype=jnp.int32)] * n_sample)
  rope = _rope_tables(positions)
  seqs = dict(prompt_rows=prompt_rows, s_prompt=s_prompt, s_sample=s_sample)
  w_in, w_spatial, w_mem_kv, w_out, w_gate_up, w_down = (
      w.astype(BF16) for w in (w_in, w_spatial, w_mem_kv, w_out, w_gate_up, w_down))

  for l in range(depth):
    kv = _norm_matmul(mem, g_mem_norm[l], w_mem_kv, l, tm=N_MEM, tn=MEM_WIDTH,
                      name="mem_kv").reshape(n_mems, N_MEM, 2 * MEM_WIDTH)
    g_out = g_group_out[l].reshape(1, MIX_WIDTH)
    b_sp = jnp.broadcast_to(b_spatial[l][:, :, None], (N_SGU_GROUPS, SGU_CHUNK, HEAD_DIM))
    zq, c4, c16, a_n, c_n = _in_proj(
        x, g_mix_norm[l], w_in, l, rope, kv, g_sgu[l].reshape(1, SGU_WIDTH), w_spatial, b_sp,
        g_out, tm=TM_IN_PROJ, rows_per_mem=s_sample,
        first_sample_mem_tile=prompt_rows // s_sample)
    o_list, lse_list = [], []
    for d, src in ((1, zq.reshape(1, *zq.shape)), (D4, c4), (D16, c16)):
      o, lse = _dilated_branch(src, d=d, bq=min(BQ_DIL_MAX, s_sample // d), **seqs)
      o_list.append(o)
      lse_list.append(lse)
    x = _mix_out(o_list, lse_list, g_out[:, SGU_WIDTH:SGU_WIDTH + DIL_WIDTH], a_n, c_n, w_out, l, x,
                 tm=TM_MIX_OUT)
    ffn = functools.partial(_ffn, x, g_ffn_norm[l].reshape(1, D_MODEL), w_gate_up, w_down, l,
                            g_final.reshape(1, D_MODEL), tm=TM_FFN, th=TH_FFN)
    if l < depth - 1:
      x = ffn(final_norm=False)

  y_prompt = ffn(final_norm=True, row0=0, rows=prompt_rows)
  y_sample = ffn(final_norm=True, row0=prompt_rows, rows=n_sample * s_sample)
  return (y_prompt.reshape(n_prompt, s_prompt, D_MODEL),
          y_sample.reshape(n_sample, s_sample, D_MODEL))
```

```python
import functools
import math

import jax
import jax.numpy as jnp
from jax import lax
from jax.experimental import pallas as pl
from jax.experimental.pallas import tpu as pltpu

F32 = jnp.float32
BF16 = jnp.bfloat16

D_MODEL = 2048
HEAD_DIM = 128
N_SGU_GROUPS = 4
SGU_WIDTH = N_SGU_GROUPS * HEAD_DIM
SGU_CHUNK = 128
N_DIL_HEADS = 8
DIL_WIDTH = N_DIL_HEADS * HEAD_DIM
DILATIONS = (1, 4, 16)
HALF = 64
N_MEM_HEADS = 4
MEM_WIDTH = N_MEM_HEADS * HEAD_DIM
N_MEM = 256
MIX_WIDTH = SGU_WIDTH + DIL_WIDTH + MEM_WIDTH
IN_WIDTH = 2 * SGU_WIDTH + 3 * DIL_WIDTH + MEM_WIDTH
ROPE_THETA = 500000.0
ROPE_DIM = HEAD_DIM // 4
ROPE_HALF = ROPE_DIM // 2
FFN_HIDDEN = 5632
EPS = 1e-6
NEG_INF = -1e30
ATTN_SCALE = HEAD_DIM ** -0.5
DIL_Q_SCALE = ATTN_SCALE * math.log2(math.e)

V7X_VMEM_BYTES = 64 * 1024 * 1024
VMEM_LIMIT = V7X_VMEM_BYTES - 8 * 1024 * 1024

IN_CHUNK_N = 256
IN_SLABS = 4
IN_PROJ_VMEM_LIMIT = V7X_VMEM_BYTES - 3 * 1024 * 1024
IN_V0 = SGU_WIDTH
IN_QKV0 = 2 * SGU_WIDTH
IN_QC0 = IN_QKV0 + 3 * DIL_WIDTH
QKV_WIDTH = 3 * DIL_WIDTH
D4, D16 = DILATIONS[1], DILATIONS[2]
D16_PER_D4 = D16 // D4

TM_IN_PROJ = 512
TM_MIX_OUT = 512
OUT_CHUNK_N = 256
TM_FFN = 1024
TH_FFN = 512
DIL_ROWS_PER_STEP = 1024


def _params(*sem, vmem=VMEM_LIMIT):
  return pltpu.CompilerParams(dimension_semantics=sem, vmem_limit_bytes=vmem)


def _rms_scale(x):
  return lax.rsqrt(jnp.mean(x * x, axis=-1, keepdims=True) + EPS)


def _norm_matmul_kernel(x_ref, g_ref, w_ref, o_ref, h_ref):
  @pl.when(pl.program_id(2) == 0)
  def _():
    xf = x_ref[...]
    h_ref[...] = (xf * _rms_scale(xf) * g_ref[...]).astype(BF16)

  o_ref[...] = jnp.dot(h_ref[...], w_ref[...], preferred_element_type=F32).astype(o_ref.dtype)


def _norm_matmul_layers(x, g, w, *, tm, tn, name):
  m, k = x.shape
  depth, _, n = w.shape
  return pl.pallas_call(
      _norm_matmul_kernel,
      grid=(depth, m // tm, n // tn),
      in_specs=[pl.BlockSpec((tm, k), lambda l, i, j: (i, 0)),
                pl.BlockSpec((None, 1, k), lambda l, i, j: (l, 0, 0)),
                pl.BlockSpec((None, k, tn), lambda l, i, j: (l, 0, j))],
      out_specs=pl.BlockSpec((None, tm, tn), lambda l, i, j: (l, i, j)),
      out_shape=jax.ShapeDtypeStruct((depth, m, n), BF16),
      scratch_shapes=[pltpu.VMEM((tm, k), BF16)],
      compiler_params=_params("parallel", "parallel", "arbitrary"),
      name=name,
  )(x, g.reshape(depth, 1, k), w)


def _in_col_kind(col):
  bounds = ((IN_V0, "u"), (IN_QKV0, "v"), (IN_QKV0 + DIL_WIDTH, "q"),
            (IN_QKV0 + 2 * DIL_WIDTH, "k"), (IN_QC0, "vb"))
  for end, kind in bounds:
    if col < end:
      return kind
  return "qc"


def _mem_attention(q_heads, kv_ref, go_ref, o_ref):
  outs = []
  ssq = jnp.zeros((q_heads[0].shape[0], 1), F32)
  for h, q in enumerate(q_heads):
    k = kv_ref[:, h * HEAD_DIM:(h + 1) * HEAD_DIM]
    v = kv_ref[:, MEM_WIDTH + h * HEAD_DIM:MEM_WIDTH + (h + 1) * HEAD_DIM]
    s = lax.dot_general(q, k, (((1,), (1,)), ((), ())), preferred_element_type=F32) * ATTN_SCALE
    m = jnp.max(s, axis=-1, keepdims=True)
    p = jnp.exp(s - m)
    den = jnp.sum(p, axis=-1, keepdims=True)
    o = jnp.dot(p.astype(BF16), v, preferred_element_type=F32) / den
    ssq = ssq + jnp.sum(o * o, axis=-1, keepdims=True)
    outs.append(o)
  scale = lax.rsqrt(ssq * (1.0 / MEM_WIDTH) + EPS)
  for h in range(N_MEM_HEADS):
    cs = slice(h * HEAD_DIM, (h + 1) * HEAD_DIM)
    o_ref[:, cs] = (outs[h] * scale * go_ref[:, cs]).astype(o_ref.dtype)


def _spatial_gating(uv_ref, gs_ref, w_ref, b_ref, go_ref, o_ref):
  for c in range(uv_ref.shape[0] // SGU_CHUNK):
    rs = slice(c * SGU_CHUNK, (c + 1) * SGU_CHUNK)
    outs = []
    ssq = jnp.zeros((SGU_CHUNK, 1), F32)
    for g in range(N_SGU_GROUPS):
      cs = slice(g * HEAD_DIM, (g + 1) * HEAD_DIM)
      v = uv_ref[rs, IN_V0 + g * HEAD_DIM:IN_V0 + (g + 1) * HEAD_DIM]
      vv = v * _rms_scale(v) * gs_ref[:, cs]
      vs = jnp.dot(w_ref[g], vv.astype(BF16), preferred_element_type=F32) + b_ref[g]
      a = uv_ref[rs, cs] * vs
      ssq = ssq + jnp.sum(a * a, axis=-1, keepdims=True)
      outs.append(a)
    scale = lax.rsqrt(ssq * (1.0 / SGU_WIDTH) + EPS)
    for g in range(N_SGU_GROUPS):
      cs = slice(g * HEAD_DIM, (g + 1) * HEAD_DIM)
      o_ref[rs, cs] = (outs[g] * scale * go_ref[:, cs]).astype(o_ref.dtype)


def _in_proj_kernel(x_ref, g_ref, w_ref, c_ref, s1_ref, s2_ref, kv_ref, gs_ref, wsp_ref, bsp_ref,
                    go_ref, z_ref, c4_ref, c16_ref, a_ref, cn_ref, h_ref, slab, slab4, uv_ref):
  tm = x_ref.shape[0]
  xf = x_ref[...]
  h_ref[...] = (xf * _rms_scale(xf) * g_ref[...]).astype(BF16)
  c, s1, s2 = c_ref[...], s1_ref[...], s2_ref[...]
  tables = {"k": (c, s1, s2), "q": (c * DIL_Q_SCALE, s1 * DIL_Q_SCALE, s2 * DIL_Q_SCALE)}
  n_slabs = slab.shape[0]
  heads_per_chunk = IN_CHUNK_N // HEAD_DIM
  n_chunks = IN_WIDTH // IN_CHUNK_N
  first_qc, first_qkv = IN_QC0 // IN_CHUNK_N, IN_QKV0 // IN_CHUNK_N
  order = (list(range(first_qc, n_chunks)) + list(range(first_qkv))
           + list(range(first_qkv, first_qc)))
  qc_heads = []
  for chunk in order:
    col0 = chunk * IN_CHUNK_N
    acc = jnp.dot(h_ref[...], w_ref[:, col0:col0 + IN_CHUNK_N], preferred_element_type=F32)
    for hh in range(heads_per_chunk):
      col = col0 + hh * HEAD_DIM
      kind = _in_col_kind(col)
      t = acc[:, hh * HEAD_DIM:(hh + 1) * HEAD_DIM]
      if kind == "qc":
        qc_heads.append(t.astype(BF16))
        continue
      if kind in ("u", "v"):
        uv_ref[:, col:col + HEAD_DIM] = jax.nn.gelu(t)
        continue
      if kind in tables:
        tc, ts1, ts2 = tables[kind]
        t = (t * tc + pltpu.roll(t, ROPE_HALF, 1) * ts1
             + pltpu.roll(t, HEAD_DIM - ROPE_HALF, 1) * ts2)
      cs = slice(col - IN_QKV0, col - IN_QKV0 + HEAD_DIM)
      z_ref[:, cs] = t.astype(z_ref.dtype)
      sl = (chunk * heads_per_chunk + hh) % n_slabs
      slab[sl] = t
      for rho in range(D4):
        t4 = slab.at[sl][pl.ds(rho, tm // D4, stride=D4), :]
        c4_ref[rho, :, cs] = t4.astype(c4_ref.dtype)
        slab4[sl, rho] = t4
        for q in range(D16_PER_D4):
          t16 = slab4.at[sl, rho][pl.ds(q, tm // D16, stride=D16_PER_D4), :]
          c16_ref[D4 * q + rho, :, cs] = t16.astype(c16_ref.dtype)
    if chunk == n_chunks - 1:
      _mem_attention(qc_heads, kv_ref, go_ref.at[:, SGU_WIDTH + DIL_WIDTH:], cn_ref)
    if chunk == first_qkv - 1:
      _spatial_gating(uv_ref, gs_ref, wsp_ref, bsp_ref, go_ref.at[:, :SGU_WIDTH], a_ref)


def _in_proj(x, g, w, layer, rope, kv, g_sgu, w_sp, b_sp, g_out, *, tm, rows_per_mem,
             first_sample_mem_tile):
  tok, k = x.shape
  tiles_per_mem = rows_per_mem // tm

  def kv_map(i):
    return (layer, jnp.maximum(i // tiles_per_mem - first_sample_mem_tile + 1, 0), 0, 0)

  const = lambda shape: pl.BlockSpec(shape, lambda i: (0,) * len(shape))
  return pl.pallas_call(
      _in_proj_kernel,
      grid=(tok // tm,),
      in_specs=[pl.BlockSpec((tm, k), lambda i: (i, 0)),
                const((1, k)),
                pl.BlockSpec((None, k, IN_WIDTH), lambda i: (layer, 0, 0),
                             pipeline_mode=pl.Buffered(1))]
      + [pl.BlockSpec((tm, HEAD_DIM), lambda i: (i, 0))] * 3
      + [pl.BlockSpec((None, None, N_MEM, 2 * MEM_WIDTH), kv_map),
         const((1, SGU_WIDTH)),
         pl.BlockSpec((None, N_SGU_GROUPS, SGU_CHUNK, SGU_CHUNK), lambda i: (layer, 0, 0, 0)),
         const((N_SGU_GROUPS, SGU_CHUNK, HEAD_DIM)),
         const((1, MIX_WIDTH))],
      out_specs=[pl.BlockSpec((tm, QKV_WIDTH), lambda i: (i, 0)),
                 pl.BlockSpec((D4, tm // D4, QKV_WIDTH), lambda i: (0, i, 0)),
                 pl.BlockSpec((D16, tm // D16, QKV_WIDTH), lambda i: (0, i, 0)),
                 pl.BlockSpec((tm, SGU_WIDTH), lambda i: (i, 0)),
                 pl.BlockSpec((tm, MEM_WIDTH), lambda i: (i, 0))],
      out_shape=[jax.ShapeDtypeStruct((tok, QKV_WIDTH), BF16),
                 jax.ShapeDtypeStruct((D4, tok // D4, QKV_WIDTH), BF16),
                 jax.ShapeDtypeStruct((D16, tok // D16, QKV_WIDTH), BF16),
                 jax.ShapeDtypeStruct((tok, SGU_WIDTH), BF16),
                 jax.ShapeDtypeStruct((tok, MEM_WIDTH), BF16)],
      scratch_shapes=[pltpu.VMEM((tm, k), BF16),
                      pltpu.VMEM((IN_SLABS, tm, HEAD_DIM), F32),
                      pltpu.VMEM((IN_SLABS, D4, tm // D4, HEAD_DIM), F32),
                      pltpu.VMEM((tm, 2 * SGU_WIDTH), F32)],
      compiler_params=_params("parallel", vmem=IN_PROJ_VMEM_LIMIT),
      name="in_proj",
  )(x, g.reshape(1, k), w, *rope, kv, g_sgu, w_sp, b_sp, g_out)


DIL_SUB = 2 * HALF
DIL_KEYS = DIL_SUB + 2 * HALF
DIL_UNIT = 2 * DIL_SUB


def _dil_kernel(q_ref, kp_ref, kc_ref, kn_ref, vp_ref, vc_ref, vn_ref, o_ref, lse_ref,
                kbuf, vbuf, *, class_len_prompt, class_len_sample, prompt_rows):
  cps, bq = q_ref.shape[0], q_ref.shape[1]
  row0 = pl.program_id(1) * bq
  in_prompt = row0 < prompt_rows
  clen = jnp.where(in_prompt, class_len_prompt, class_len_sample)
  pos = jnp.where(in_prompt, row0, row0 - prompt_rows) & (clen - 1)
  first = pos == 0
  last = pos + bq == clen

  for cls in range(cps):
    kbuf[cls, 0:HALF, :] = kp_ref[cls]
    kbuf[cls, HALF:HALF + bq, :] = kc_ref[cls]
    kbuf[cls, HALF + bq:, :] = kn_ref[cls]
    vbuf[cls, 0:HALF, :] = vp_ref[cls]
    vbuf[cls, HALF:HALF + bq, :] = vc_ref[cls]
    vbuf[cls, HALF + bq:, :] = vn_ref[cls]

  r = lax.broadcasted_iota(jnp.int32, (DIL_SUB, DIL_KEYS), 0)
  c = lax.broadcasted_iota(jnp.int32, (DIL_SUB, DIL_KEYS), 1)
  band = jnp.where((c >= r) & (c <= r + 2 * HALF), 0.0, NEG_INF).astype(F32)
  lane = lax.broadcasted_iota(jnp.int32, (DIL_SUB, HEAD_DIM), 1)
  units = bq // DIL_UNIT

  def unit(u, carry):
    cls = u // units
    jj = u % units
    base = pl.multiple_of(jj * DIL_UNIT, DIL_UNIT)
    lo = jnp.where(first & (jj == 0), NEG_INF, 0.0).astype(F32)
    hi = jnp.where(last & (jj == units - 1), NEG_INF, 0.0).astype(F32)
    biases = (band + jnp.where(c < HALF, lo, 0.0), band + jnp.where(c >= DIL_KEYS - HALF, hi, 0.0))
    for j in range(DIL_UNIT // DIL_SUB):
      rs = pl.ds(base + j * DIL_SUB, DIL_SUB)
      ks = pl.ds(base + j * DIL_SUB, DIL_KEYS)
      lse_tile = jnp.zeros((DIL_SUB, HEAD_DIM), F32)
      for h in range(N_DIL_HEADS):
        cs = slice(h * HEAD_DIM, (h + 1) * HEAD_DIM)
        s = lax.dot_general(q_ref[cls, rs, cs], kbuf[cls, ks, cs], (((1,), (1,)), ((), ())),
                            preferred_element_type=F32) + biases[j]
        m = jnp.max(s, axis=-1, keepdims=True)
        p = jnp.exp2(s - m)
        den = jnp.sum(p, axis=-1, keepdims=True)
        num = jnp.dot(p.astype(BF16), vbuf[cls, ks, cs], preferred_element_type=F32)
        o_ref[cls, rs, cs] = (num / den).astype(o_ref.dtype)
        lse_tile = jnp.where(lane == h, m + jnp.log2(den), lse_tile)
      lse_ref[cls, rs, :] = lse_tile
    return carry

  lax.fori_loop(0, cps * units, unit, 0)


def _dilated_branch(src, *, d, bq, cps, prompt_rows, s_prompt, s_sample):
  rows = src.shape[1]
  hb = bq // HALF
  n_half_blocks = rows // HALF
  body = functools.partial(
      _dil_kernel, class_len_prompt=s_prompt // d, class_len_sample=s_sample // d,
      prompt_rows=prompt_rows // d)

  def cur(c):
    return pl.BlockSpec((cps, bq, DIL_WIDTH), lambda r, i: (r, i, c))

  def prev(c):
    return pl.BlockSpec((cps, HALF, DIL_WIDTH), lambda r, i: (r, jnp.maximum(i * hb - 1, 0), c))

  def nxt(c):
    return pl.BlockSpec((cps, HALF, DIL_WIDTH),
                        lambda r, i: (r, jnp.minimum((i + 1) * hb, n_half_blocks - 1), c))

  return pl.pallas_call(
      body,
      grid=(d // cps, rows // bq),
      in_specs=[cur(0), prev(1), cur(1), nxt(1), prev(2), cur(2), nxt(2)],
      out_specs=[pl.BlockSpec((cps, bq, DIL_WIDTH), lambda r, i: (r, i, 0)),
                 pl.BlockSpec((cps, bq, HEAD_DIM), lambda r, i: (r, i, 0))],
      out_shape=[jax.ShapeDtypeStruct((d, rows, DIL_WIDTH), BF16),
                 jax.ShapeDtypeStruct((d, rows, HEAD_DIM), F32)],
      scratch_shapes=[pltpu.VMEM((cps, bq + 2 * HALF, DIL_WIDTH), BF16),
                      pltpu.VMEM((cps, bq + 2 * HALF, DIL_WIDTH), BF16)],
      compiler_params=_params("parallel", "parallel"),
      name=f"dilated_d{d}",
  )(*([src] * 7))


def _merge_tile(o1_ref, o4_ref, o16_ref, l1_ref, l4_ref, l16_ref, e_ref, gb_ref, b_ref,
                s4, s16, ls4, ls16, bs):
  tm = o1_ref.shape[0]
  for rho in range(D4):
    ls4[pl.ds(rho, tm // D4, stride=D4), :] = l4_ref[rho]
  for r in range(D16):
    ls16[pl.ds(r, tm // D16, stride=D16), :] = l16_ref[r]

  l1, l2, l3 = l1_ref[...], ls4[...], ls16[...]
  m = jnp.maximum(jnp.maximum(l1, l2), l3)
  w1, w2, w3 = jnp.exp2(l1 - m), jnp.exp2(l2 - m), jnp.exp2(l3 - m)
  inv = 1.0 / (w1 + w2 + w3)

  def over_head_lanes(w):
    wn = w * inv
    hi = wn.astype(BF16)
    lo = (wn - hi.astype(F32)).astype(BF16)
    return jnp.dot(jnp.concatenate([hi, lo], axis=1), e_ref[...], preferred_element_type=F32)

  wb1, wb2, wb3 = over_head_lanes(w1), over_head_lanes(w2), over_head_lanes(w3)

  ssq = jnp.zeros((tm, 1), F32)
  for h in range(N_DIL_HEADS):
    cs = slice(h * HEAD_DIM, (h + 1) * HEAD_DIM)
    for rho in range(D4):
      s4.at[h][pl.ds(rho, tm // D4, stride=D4), :] = o4_ref[rho, :, cs].astype(F32)
    for r in range(D16):
      s16.at[h][pl.ds(r, tm // D16, stride=D16), :] = o16_ref[r, :, cs].astype(F32)
    b = wb1[:, cs] * o1_ref[:, cs].astype(F32) + wb2[:, cs] * s4[h] + wb3[:, cs] * s16[h]
    ssq = ssq + jnp.sum(b * b, axis=-1, keepdims=True)
    bs[h] = b
  scale = jnp.broadcast_to(lax.rsqrt(ssq * (1.0 / DIL_WIDTH) + EPS), (tm, HEAD_DIM))
  for h in range(N_DIL_HEADS):
    cs = slice(h * HEAD_DIM, (h + 1) * HEAD_DIM)
    b_ref[:, cs] = (bs[h] * scale * gb_ref[:, cs]).astype(b_ref.dtype)


def _mix_out_kernel(o1_ref, o4_ref, o16_ref, l1_ref, l4_ref, l16_ref, e_ref, gb_ref,
                    a_ref, c_ref, w_ref, x_ref, y_ref, s4, s16, ls4, ls16, bs, bn):
  step = pl.program_id(0)
  b0 = SGU_WIDTH
  c0 = SGU_WIDTH + DIL_WIDTH

  def merge_into(slot):
    _merge_tile(o1_ref, o4_ref, o16_ref, l1_ref, l4_ref, l16_ref, e_ref, gb_ref, bn.at[slot],
                s4, s16, ls4, ls16, bs)

  @pl.when(step == 0)
  def _():
    merge_into(0)

  @pl.when(step > 0)
  def _():
    slot = step % 2
    merge_into(slot)
    b_prev = bn.at[1 - slot]
    for chunk in range(D_MODEL // OUT_CHUNK_N):
      cols = slice(chunk * OUT_CHUNK_N, (chunk + 1) * OUT_CHUNK_N)
      acc = jnp.dot(a_ref[...], w_ref[0:b0, cols], preferred_element_type=F32)
      acc += jnp.dot(b_prev[...], w_ref[b0:c0, cols], preferred_element_type=F32)
      acc += jnp.dot(c_ref[...], w_ref[c0:, cols], preferred_element_type=F32)
      y_ref[:, cols] = x_ref[:, cols] + acc


def _mix_out(o_list, lse_list, g_b, a_n, c_n, w_out, layer, x, *, tm):
  o1, o4, o16 = o_list
  l1, l4, l16 = lse_list
  tok = x.shape[0]
  n = tok // tm
  head_of_lane = jnp.arange(DIL_WIDTH, dtype=jnp.int32) // HEAD_DIM
  selector = (jnp.arange(HEAD_DIM, dtype=jnp.int32)[:, None] == head_of_lane[None, :]).astype(BF16)
  selector = jnp.concatenate([selector, selector], axis=0)
  head_scratch = pltpu.VMEM((N_DIL_HEADS, tm, HEAD_DIM), F32)
  ahead = lambda s: jnp.minimum(s, n - 1)
  behind = lambda s: jnp.maximum(s - 1, 0)
  const = lambda shape: pl.BlockSpec(shape, lambda s: (0,) * len(shape))
  row = lambda w: pl.BlockSpec((tm, w), lambda s: (behind(s), 0))
  return pl.pallas_call(
      _mix_out_kernel,
      grid=(n + 1,),
      in_specs=[pl.BlockSpec((None, tm, DIL_WIDTH), lambda s: (0, ahead(s), 0)),
                pl.BlockSpec((D4, tm // D4, DIL_WIDTH), lambda s: (0, ahead(s), 0)),
                pl.BlockSpec((D16, tm // D16, DIL_WIDTH), lambda s: (0, ahead(s), 0)),
                pl.BlockSpec((None, tm, HEAD_DIM), lambda s: (0, ahead(s), 0)),
                pl.BlockSpec((D4, tm // D4, HEAD_DIM), lambda s: (0, ahead(s), 0)),
                pl.BlockSpec((D16, tm // D16, HEAD_DIM), lambda s: (0, ahead(s), 0)),
                const((2 * HEAD_DIM, DIL_WIDTH)),
                const((1, DIL_WIDTH)),
                row(SGU_WIDTH), row(MEM_WIDTH),
                pl.BlockSpec((None, MIX_WIDTH, D_MODEL), lambda s: (layer, 0, 0),
                             pipeline_mode=pl.Buffered(1)),
                row(D_MODEL)],
      out_specs=row(D_MODEL),
      out_shape=jax.ShapeDtypeStruct((tok, D_MODEL), F32),
      scratch_shapes=[head_scratch, head_scratch,
                      pltpu.VMEM((tm, HEAD_DIM), F32), pltpu.VMEM((tm, HEAD_DIM), F32),
                      head_scratch, pltpu.VMEM((2, tm, DIL_WIDTH), BF16)],
      compiler_params=_params("arbitrary"),
      name="mix_out",
  )(o1, o4, o16, l1, l4, l16, selector, g_b, a_n, c_n, w_out, x)


def _ffn_kernel(x_ref, g_ref, wg_ref, wu_ref, wd_ref, gf_ref, y_ref, h_ref, *, final_norm):
  k = pl.program_id(1)

  @pl.when(k == 0)
  def _():
    xf = x_ref[...]
    h_ref[...] = (xf * _rms_scale(xf) * g_ref[...]).astype(BF16)
    y_ref[...] = xf

  h = h_ref[...]
  gate = jnp.dot(h, wg_ref[...], preferred_element_type=F32)
  up = jnp.dot(h, wu_ref[...], preferred_element_type=F32)
  act = (jax.nn.silu(gate) * up).astype(BF16)
  y_ref[...] += jnp.dot(act, wd_ref[...], preferred_element_type=F32)

  if final_norm:
    @pl.when(k == pl.num_programs(1) - 1)
    def _():
      y = y_ref[...]
      y_ref[...] = y * _rms_scale(y) * gf_ref[...]


def _ffn(x, g, w_gu, w_down, layer, g_final, *, tm, th, final_norm, row0=0, rows=None):
  tok = x.shape[0] if rows is None else rows
  tile0 = row0 // tm
  nk = FFN_HIDDEN // th
  return pl.pallas_call(
      functools.partial(_ffn_kernel, final_norm=final_norm),
      grid=(tok // tm, nk),
      in_specs=[
          pl.BlockSpec((tm, D_MODEL), lambda i, k: (tile0 + i, 0), pipeline_mode=pl.Buffered(1)),
          pl.BlockSpec((1, D_MODEL), lambda i, k: (0, 0)),
          pl.BlockSpec((None, D_MODEL, th), lambda i, k: (layer, 0, k)),
          pl.BlockSpec((None, D_MODEL, th), lambda i, k: (layer, 0, nk + k)),
          pl.BlockSpec((None, th, D_MODEL), lambda i, k: (layer, k, 0)),
          pl.BlockSpec((1, D_MODEL), lambda i, k: (0, 0)),
      ],
      out_specs=pl.BlockSpec((tm, D_MODEL), lambda i, k: (i, 0)),
      out_shape=jax.ShapeDtypeStruct((tok, D_MODEL), F32),
      scratch_shapes=[pltpu.VMEM((tm, D_MODEL), BF16)],
      compiler_params=_params("parallel", "arbitrary"),
      name="ffn",
  )(x, g, w_gu, w_gu, w_down, g_final)


def _rope_tables(positions):
  inv = ROPE_THETA ** (-jnp.arange(0, ROPE_DIM, 2, dtype=F32) / ROPE_DIM)
  ang = positions.astype(F32)[:, None] * inv[None, :]
  cos, sin = jnp.cos(ang), jnp.sin(ang)
  n = positions.shape[0]
  rest = HEAD_DIM - ROPE_DIM
  c = jnp.concatenate([cos, cos, jnp.ones((n, rest), F32)], axis=1)
  s1 = jnp.concatenate([jnp.zeros((n, ROPE_HALF), F32), sin, jnp.zeros((n, rest), F32)], axis=1)
  s2 = jnp.concatenate([-sin, jnp.zeros((n, HEAD_DIM - ROPE_HALF), F32)], axis=1)
  return c, s1, s2


def kernel(x_prompt, x_sample, mem_prompt, mem_sample, g_mix_norm, w_in, g_sgu, w_spatial,
           b_spatial, g_mem_norm, w_mem_kv, g_group_out, w_out, g_ffn_norm, w_gate_up,
           w_down, g_final):
  n_prompt, s_prompt, _ = x_prompt.shape
  n_sample, s_sample, _ = x_sample.shape
  assert n_prompt == 1 and s_prompt % s_sample == 0
  assert s_sample % (D16 * DIL_SUB) == 0 and s_sample % TM_FFN == 0
  prompt_rows = n_prompt * s_prompt
  depth = w_in.shape[0]

  x = jnp.concatenate([x_prompt.reshape(prompt_rows, D_MODEL),
                       x_sample.reshape(n_sample * s_sample, D_MODEL)], axis=0)
  mem = jnp.concatenate([mem_prompt, mem_sample], axis=0)
  n_mems = mem.shape[0]
  mem = mem.reshape(n_mems * N_MEM, D_MODEL)
  positions = jnp.concatenate([jnp.arange(s_prompt, dtype=jnp.int32)] * n_prompt
                              + [jnp.arange(s_sample, dtype=jnp.int32)] * n_sample)
  rope = _rope_tables(positions)
  seqs = dict(prompt_rows=prompt_rows, s_prompt=s_prompt, s_sample=s_sample)
  w_in, w_spatial, w_mem_kv, w_out, w_gate_up, w_down = (
      w.astype(BF16) for w in (w_in, w_spatial, w_mem_kv, w_out, w_gate_up, w_down))

  kv = _norm_matmul_layers(mem, g_mem_norm, w_mem_kv, tm=N_MEM, tn=MEM_WIDTH, name="mem_kv")
  kv = kv.reshape(depth, n_mems, N_MEM, 2 * MEM_WIDTH)

  for l in range(depth):
    g_out = g_group_out[l].reshape(1, MIX_WIDTH)
    b_sp = jnp.broadcast_to(b_spatial[l][:, :, None], (N_SGU_GROUPS, SGU_CHUNK, HEAD_DIM))
    zq, c4, c16, a_n, c_n = _in_proj(
        x, g_mix_norm[l], w_in, l, rope, kv, g_sgu[l].reshape(1, SGU_WIDTH), w_spatial, b_sp,
        g_out, tm=TM_IN_PROJ, rows_per_mem=s_sample,
        first_sample_mem_tile=prompt_rows // s_sample)
    o_list, lse_list = [], []
    for d, src in ((1, zq.reshape(1, *zq.shape)), (D4, c4), (D16, c16)):
      bq = min(DIL_ROWS_PER_STEP, s_sample // d)
      o, lse = _dilated_branch(src, d=d, bq=bq, cps=min(d, DIL_ROWS_PER_STEP // bq), **seqs)
      o_list.append(o)
      lse_list.append(lse)
    x = _mix_out(o_list, lse_list, g_out[:, SGU_WIDTH:SGU_WIDTH + DIL_WIDTH], a_n, c_n, w_out, l, x,
                 tm=TM_MIX_OUT)
    ffn = functools.partial(_ffn, x, g_ffn_norm[l].reshape(1, D_MODEL), w_gate_up, w_down, l,
                            g_final.reshape(1, D_MODEL), tm=TM_FFN, th=TH_FFN)
    if l < depth - 1:
      x = ffn(final_norm=False)

  y_prompt = ffn(final_norm=True, row0=0, rows=prompt_rows)
  y_sample = ffn(final_norm=True, row0=prompt_rows, rows=n_sample * s_sample)
  return (y_prompt.reshape(n_prompt, s_prompt, D_MODEL),
          y_sample.reshape(n_sample, s_sample, D_MODEL))
```

```python
import functools
import math

import jax
import jax.numpy as jnp
from jax import lax
from jax.experimental import pallas as pl
from jax.experimental.pallas import tpu as pltpu

F32 = jnp.float32
BF16 = jnp.bfloat16

D_MODEL = 2048
HEAD_DIM = 128
N_SGU_GROUPS = 4
SGU_WIDTH = N_SGU_GROUPS * HEAD_DIM
SGU_CHUNK = 128
N_DIL_HEADS = 8
DIL_WIDTH = N_DIL_HEADS * HEAD_DIM
DILATIONS = (1, 4, 16)
HALF = 64
N_MEM_HEADS = 4
MEM_WIDTH = N_MEM_HEADS * HEAD_DIM
N_MEM = 256
MIX_WIDTH = SGU_WIDTH + DIL_WIDTH + MEM_WIDTH
IN_WIDTH = 2 * SGU_WIDTH + 3 * DIL_WIDTH + MEM_WIDTH
ROPE_THETA = 500000.0
ROPE_DIM = HEAD_DIM // 4
ROPE_HALF = ROPE_DIM // 2
FFN_HIDDEN = 5632
EPS = 1e-6
NEG_INF = -1e30
ATTN_SCALE = HEAD_DIM ** -0.5
DIL_Q_SCALE = ATTN_SCALE * math.log2(math.e)

V7X_VMEM_BYTES = 64 * 1024 * 1024
VMEM_LIMIT = V7X_VMEM_BYTES - 8 * 1024 * 1024

IN_CHUNK_N = 256
IN_SLABS = 4
BIG_TILE_VMEM_LIMIT = V7X_VMEM_BYTES - 3 * 1024 * 1024
IN_V0 = SGU_WIDTH
IN_QKV0 = 2 * SGU_WIDTH
IN_QC0 = IN_QKV0 + 3 * DIL_WIDTH
QKV_WIDTH = 3 * DIL_WIDTH
D4, D16 = DILATIONS[1], DILATIONS[2]
D16_PER_D4 = D16 // D4

TM_IN_PROJ = 512
TM_MIX_OUT = 512
OUT_CHUNK_N = 256
TM_FFN = 1024
TH_FFN = 512
DIL_ROWS_PER_STEP = 1024


def _params(*sem, vmem=VMEM_LIMIT):
  return pltpu.CompilerParams(dimension_semantics=sem, vmem_limit_bytes=vmem)


def _rms_scale(x):
  return lax.rsqrt(jnp.mean(x * x, axis=-1, keepdims=True) + EPS)


def _norm_matmul_kernel(x_ref, g_ref, w_ref, o_ref, h_ref):
  @pl.when(pl.program_id(2) == 0)
  def _():
    xf = x_ref[...]
    h_ref[...] = (xf * _rms_scale(xf) * g_ref[...]).astype(BF16)

  o_ref[...] = jnp.dot(h_ref[...], w_ref[...], preferred_element_type=F32).astype(o_ref.dtype)


def _norm_matmul_layers(x, g, w, *, tm, tn, name):
  m, k = x.shape
  depth, _, n = w.shape
  return pl.pallas_call(
      _norm_matmul_kernel,
      grid=(depth, m // tm, n // tn),
      in_specs=[pl.BlockSpec((tm, k), lambda l, i, j: (i, 0)),
                pl.BlockSpec((None, 1, k), lambda l, i, j: (l, 0, 0)),
                pl.BlockSpec((None, k, tn), lambda l, i, j: (l, 0, j))],
      out_specs=pl.BlockSpec((None, tm, tn), lambda l, i, j: (l, i, j)),
      out_shape=jax.ShapeDtypeStruct((depth, m, n), BF16),
      scratch_shapes=[pltpu.VMEM((tm, k), BF16)],
      compiler_params=_params("parallel", "parallel", "arbitrary"),
      name=name,
  )(x, g.reshape(depth, 1, k), w)


def _in_col_kind(col):
  bounds = ((IN_V0, "u"), (IN_QKV0, "v"), (IN_QKV0 + DIL_WIDTH, "q"),
            (IN_QKV0 + 2 * DIL_WIDTH, "k"), (IN_QC0, "vb"))
  for end, kind in bounds:
    if col < end:
      return kind
  return "qc"


def _mem_attention(q_heads, kv_ref, go_ref, o_ref):
  outs = []
  ssq = jnp.zeros((q_heads[0].shape[0], 1), F32)
  for h, q in enumerate(q_heads):
    k = kv_ref[:, h * HEAD_DIM:(h + 1) * HEAD_DIM]
    v = kv_ref[:, MEM_WIDTH + h * HEAD_DIM:MEM_WIDTH + (h + 1) * HEAD_DIM]
    s = lax.dot_general(q, k, (((1,), (1,)), ((), ())), preferred_element_type=F32) * ATTN_SCALE
    m = jnp.max(s, axis=-1, keepdims=True)
    p = jnp.exp(s - m)
    den = jnp.sum(p, axis=-1, keepdims=True)
    o = jnp.dot(p.astype(BF16), v, preferred_element_type=F32) / den
    ssq = ssq + jnp.sum(o * o, axis=-1, keepdims=True)
    outs.append(o)
  scale = lax.rsqrt(ssq * (1.0 / MEM_WIDTH) + EPS)
  for h in range(N_MEM_HEADS):
    cs = slice(h * HEAD_DIM, (h + 1) * HEAD_DIM)
    o_ref[:, cs] = (outs[h] * scale * go_ref[:, cs]).astype(o_ref.dtype)


def _spatial_gating(uv_ref, gs_ref, w_ref, b_ref, go_ref, o_ref):
  for c in range(uv_ref.shape[0] // SGU_CHUNK):
    rs = slice(c * SGU_CHUNK, (c + 1) * SGU_CHUNK)
    outs = []
    ssq = jnp.zeros((SGU_CHUNK, 1), F32)
    for g in range(N_SGU_GROUPS):
      cs = slice(g * HEAD_DIM, (g + 1) * HEAD_DIM)
      v = uv_ref[rs, IN_V0 + g * HEAD_DIM:IN_V0 + (g + 1) * HEAD_DIM]
      vv = v * _rms_scale(v) * gs_ref[:, cs]
      vs = jnp.dot(w_ref[g], vv.astype(BF16), preferred_element_type=F32) + b_ref[g]
      a = uv_ref[rs, cs] * vs
      ssq = ssq + jnp.sum(a * a, axis=-1, keepdims=True)
      outs.append(a)
    scale = lax.rsqrt(ssq * (1.0 / SGU_WIDTH) + EPS)
    for g in range(N_SGU_GROUPS):
      cs = slice(g * HEAD_DIM, (g + 1) * HEAD_DIM)
      o_ref[rs, cs] = (outs[g] * scale * go_ref[:, cs]).astype(o_ref.dtype)


def _in_proj_kernel(x_ref, g_ref, w_ref, c_ref, s1_ref, s2_ref, kv_ref, gs_ref, wsp_ref, bsp_ref,
                    go_ref, z_ref, c4_ref, c16_ref, a_ref, cn_ref, h_ref, slab, slab4, uv_ref):
  tm = x_ref.shape[0]
  xf = x_ref[...]
  h_ref[...] = (xf * _rms_scale(xf) * g_ref[...]).astype(BF16)
  c, s1, s2 = c_ref[...], s1_ref[...], s2_ref[...]
  tables = {"k": (c, s1, s2), "q": (c * DIL_Q_SCALE, s1 * DIL_Q_SCALE, s2 * DIL_Q_SCALE)}
  n_slabs = slab.shape[0]
  heads_per_chunk = IN_CHUNK_N // HEAD_DIM
  n_chunks = IN_WIDTH // IN_CHUNK_N
  first_qc, first_qkv = IN_QC0 // IN_CHUNK_N, IN_QKV0 // IN_CHUNK_N
  order = (list(range(first_qc, n_chunks)) + list(range(first_qkv))
           + list(range(first_qkv, first_qc)))
  qc_heads = []
  for chunk in order:
    col0 = chunk * IN_CHUNK_N
    acc = jnp.dot(h_ref[...], w_ref[:, col0:col0 + IN_CHUNK_N], preferred_element_type=F32)
    for hh in range(heads_per_chunk):
      col = col0 + hh * HEAD_DIM
      kind = _in_col_kind(col)
      t = acc[:, hh * HEAD_DIM:(hh + 1) * HEAD_DIM]
      if kind == "qc":
        qc_heads.append(t.astype(BF16))
        continue
      if kind in ("u", "v"):
        uv_ref[:, col:col + HEAD_DIM] = jax.nn.gelu(t)
        continue
      if kind in tables:
        tc, ts1, ts2 = tables[kind]
        t = (t * tc + pltpu.roll(t, ROPE_HALF, 1) * ts1
             + pltpu.roll(t, HEAD_DIM - ROPE_HALF, 1) * ts2)
      cs = slice(col - IN_QKV0, col - IN_QKV0 + HEAD_DIM)
      z_ref[:, cs] = t.astype(z_ref.dtype)
      sl = (chunk * heads_per_chunk + hh) % n_slabs
      slab[sl] = t
      for rho in range(D4):
        t4 = slab.at[sl][pl.ds(rho, tm // D4, stride=D4), :]
        c4_ref[rho, :, cs] = t4.astype(c4_ref.dtype)
        slab4[sl, rho] = t4
        for q in range(D16_PER_D4):
          t16 = slab4.at[sl, rho][pl.ds(q, tm // D16, stride=D16_PER_D4), :]
          c16_ref[D4 * q + rho, :, cs] = t16.astype(c16_ref.dtype)
    if chunk == n_chunks - 1:
      _mem_attention(qc_heads, kv_ref, go_ref.at[:, SGU_WIDTH + DIL_WIDTH:], cn_ref)
    if chunk == first_qkv - 1:
      _spatial_gating(uv_ref, gs_ref, wsp_ref, bsp_ref, go_ref.at[:, :SGU_WIDTH], a_ref)


def _in_proj(x, g, w, layer, rope, kv, g_sgu, w_sp, b_sp, g_out, *, tm, rows_per_mem,
             first_sample_mem_tile):
  tok, k = x.shape
  tiles_per_mem = rows_per_mem // tm

  def kv_map(i):
    return (layer, jnp.maximum(i // tiles_per_mem - first_sample_mem_tile + 1, 0), 0, 0)

  const = lambda shape: pl.BlockSpec(shape, lambda i: (0,) * len(shape))
  return pl.pallas_call(
      _in_proj_kernel,
      grid=(tok // tm,),
      in_specs=[pl.BlockSpec((tm, k), lambda i: (i, 0)),
                const((1, k)),
                pl.BlockSpec((None, k, IN_WIDTH), lambda i: (layer, 0, 0),
                             pipeline_mode=pl.Buffered(1))]
      + [pl.BlockSpec((tm, HEAD_DIM), lambda i: (i, 0))] * 3
      + [pl.BlockSpec((None, None, N_MEM, 2 * MEM_WIDTH), kv_map),
         const((1, SGU_WIDTH)),
         pl.BlockSpec((None, N_SGU_GROUPS, SGU_CHUNK, SGU_CHUNK), lambda i: (layer, 0, 0, 0)),
         const((N_SGU_GROUPS, SGU_CHUNK, HEAD_DIM)),
         const((1, MIX_WIDTH))],
      out_specs=[pl.BlockSpec((tm, QKV_WIDTH), lambda i: (i, 0)),
                 pl.BlockSpec((D4, tm // D4, QKV_WIDTH), lambda i: (0, i, 0)),
                 pl.BlockSpec((D16, tm // D16, QKV_WIDTH), lambda i: (0, i, 0)),
                 pl.BlockSpec((tm, SGU_WIDTH), lambda i: (i, 0)),
                 pl.BlockSpec((tm, MEM_WIDTH), lambda i: (i, 0))],
      out_shape=[jax.ShapeDtypeStruct((tok, QKV_WIDTH), BF16),
                 jax.ShapeDtypeStruct((D4, tok // D4, QKV_WIDTH), BF16),
                 jax.ShapeDtypeStruct((D16, tok // D16, QKV_WIDTH), BF16),
                 jax.ShapeDtypeStruct((tok, SGU_WIDTH), BF16),
                 jax.ShapeDtypeStruct((tok, MEM_WIDTH), BF16)],
      scratch_shapes=[pltpu.VMEM((tm, k), BF16),
                      pltpu.VMEM((IN_SLABS, tm, HEAD_DIM), F32),
                      pltpu.VMEM((IN_SLABS, D4, tm // D4, HEAD_DIM), F32),
                      pltpu.VMEM((tm, 2 * SGU_WIDTH), F32)],
      compiler_params=_params("parallel", vmem=BIG_TILE_VMEM_LIMIT),
      name="in_proj",
  )(x, g.reshape(1, k), w, *rope, kv, g_sgu, w_sp, b_sp, g_out)


DIL_SUB = 2 * HALF
DIL_KEYS = DIL_SUB + 2 * HALF


def _dil_kernel(q_ref, kp_ref, kc_ref, kn_ref, vp_ref, vc_ref, vn_ref, o_ref, lse_ref,
                kbuf, vbuf, *, class_len_prompt, class_len_sample, prompt_rows):
  cps, bq = q_ref.shape[0], q_ref.shape[1]
  row0 = pl.program_id(1) * bq
  in_prompt = row0 < prompt_rows
  clen = jnp.where(in_prompt, class_len_prompt, class_len_sample)
  pos = jnp.where(in_prompt, row0, row0 - prompt_rows) & (clen - 1)
  first = pos == 0
  last = pos + bq == clen

  for cls in range(cps):
    kbuf[cls, 0:HALF, :] = kp_ref[cls]
    kbuf[cls, HALF:HALF + bq, :] = kc_ref[cls]
    kbuf[cls, HALF + bq:, :] = kn_ref[cls]
    vbuf[cls, 0:HALF, :] = vp_ref[cls]
    vbuf[cls, HALF:HALF + bq, :] = vc_ref[cls]
    vbuf[cls, HALF + bq:, :] = vn_ref[cls]

  r = lax.broadcasted_iota(jnp.int32, (DIL_SUB, DIL_KEYS), 0)
  c = lax.broadcasted_iota(jnp.int32, (DIL_SUB, DIL_KEYS), 1)
  band = jnp.where((c >= r) & (c <= r + 2 * HALF), 0.0, NEG_INF).astype(F32)
  lane = lax.broadcasted_iota(jnp.int32, (DIL_SUB, HEAD_DIM), 1)
  lo = jnp.where(c < HALF, jnp.where(first, NEG_INF, 0.0), 0.0).astype(F32)
  hi = jnp.where(c >= DIL_KEYS - HALF, jnp.where(last, NEG_INF, 0.0), 0.0).astype(F32)
  nsub = bq // DIL_SUB
  biases = [band] * nsub
  biases[0] = biases[0] + lo
  biases[-1] = biases[-1] + hi

  for cls in range(cps):
    for j in range(nsub):
      rs = slice(j * DIL_SUB, (j + 1) * DIL_SUB)
      ks = slice(j * DIL_SUB, j * DIL_SUB + DIL_KEYS)
      lse_tile = jnp.zeros((DIL_SUB, HEAD_DIM), F32)
      for h in range(N_DIL_HEADS):
        cs = slice(h * HEAD_DIM, (h + 1) * HEAD_DIM)
        s = lax.dot_general(q_ref[cls, rs, cs], kbuf[cls, ks, cs], (((1,), (1,)), ((), ())),
                            preferred_element_type=F32) + biases[j]
        m = jnp.max(s, axis=-1, keepdims=True)
        p = jnp.exp2(s - m)
        den = jnp.sum(p, axis=-1, keepdims=True)
        num = jnp.dot(p.astype(BF16), vbuf[cls, ks, cs], preferred_element_type=F32)
        o_ref[cls, rs, cs] = (num / den).astype(o_ref.dtype)
        lse_tile = jnp.where(lane == h, m + jnp.log2(den), lse_tile)
      lse_ref[cls, rs, :] = lse_tile


def _dilated_branch(src, *, d, bq, cps, prompt_rows, s_prompt, s_sample):
  rows = src.shape[1]
  hb = bq // HALF
  n_half_blocks = rows // HALF
  body = functools.partial(
      _dil_kernel, class_len_prompt=s_prompt // d, class_len_sample=s_sample // d,
      prompt_rows=prompt_rows // d)

  def cur(c):
    return pl.BlockSpec((cps, bq, DIL_WIDTH), lambda r, i: (r, i, c))

  def prev(c):
    return pl.BlockSpec((cps, HALF, DIL_WIDTH), lambda r, i: (r, jnp.maximum(i * hb - 1, 0), c))

  def nxt(c):
    return pl.BlockSpec((cps, HALF, DIL_WIDTH),
                        lambda r, i: (r, jnp.minimum((i + 1) * hb, n_half_blocks - 1), c))

  return pl.pallas_call(
      body,
      grid=(d // cps, rows // bq),
      in_specs=[cur(0), prev(1), cur(1), nxt(1), prev(2), cur(2), nxt(2)],
      out_specs=[pl.BlockSpec((cps, bq, DIL_WIDTH), lambda r, i: (r, i, 0)),
                 pl.BlockSpec((cps, bq, HEAD_DIM), lambda r, i: (r, i, 0))],
      out_shape=[jax.ShapeDtypeStruct((d, rows, DIL_WIDTH), BF16),
                 jax.ShapeDtypeStruct((d, rows, HEAD_DIM), F32)],
      scratch_shapes=[pltpu.VMEM((cps, bq + 2 * HALF, DIL_WIDTH), BF16),
                      pltpu.VMEM((cps, bq + 2 * HALF, DIL_WIDTH), BF16)],
      compiler_params=_params("parallel", "parallel"),
      name=f"dilated_d{d}",
  )(*([src] * 7))


def _merge_tile(o1_ref, o4_ref, o16_ref, l1_ref, l4_ref, l16_ref, e_ref, gb_ref, b_ref,
                s4, s16, ls4, ls16, bs):
  tm = o1_ref.shape[0]
  for rho in range(D4):
    ls4[pl.ds(rho, tm // D4, stride=D4), :] = l4_ref[rho]
  for r in range(D16):
    ls16[pl.ds(r, tm // D16, stride=D16), :] = l16_ref[r]

  l1, l2, l3 = l1_ref[...], ls4[...], ls16[...]
  m = jnp.maximum(jnp.maximum(l1, l2), l3)
  w1, w2, w3 = jnp.exp2(l1 - m), jnp.exp2(l2 - m), jnp.exp2(l3 - m)
  inv = 1.0 / (w1 + w2 + w3)

  def over_head_lanes(w):
    wn = w * inv
    hi = wn.astype(BF16)
    lo = (wn - hi.astype(F32)).astype(BF16)
    return jnp.dot(jnp.concatenate([hi, lo], axis=1), e_ref[...], preferred_element_type=F32)

  wb1, wb2, wb3 = over_head_lanes(w1), over_head_lanes(w2), over_head_lanes(w3)

  ssq = jnp.zeros((tm, 1), F32)
  for h in range(N_DIL_HEADS):
    cs = slice(h * HEAD_DIM, (h + 1) * HEAD_DIM)
    for rho in range(D4):
      s4.at[h][pl.ds(rho, tm // D4, stride=D4), :] = o4_ref[rho, :, cs].astype(F32)
    for r in range(D16):
      s16.at[h][pl.ds(r, tm // D16, stride=D16), :] = o16_ref[r, :, cs].astype(F32)
    b = wb1[:, cs] * o1_ref[:, cs].astype(F32) + wb2[:, cs] * s4[h] + wb3[:, cs] * s16[h]
    ssq = ssq + jnp.sum(b * b, axis=-1, keepdims=True)
    bs[h] = b
  scale = jnp.broadcast_to(lax.rsqrt(ssq * (1.0 / DIL_WIDTH) + EPS), (tm, HEAD_DIM))
  for h in range(N_DIL_HEADS):
    cs = slice(h * HEAD_DIM, (h + 1) * HEAD_DIM)
    b_ref[:, cs] = (bs[h] * scale * gb_ref[:, cs]).astype(b_ref.dtype)


def _mix_out_kernel(o1_ref, o4_ref, o16_ref, l1_ref, l4_ref, l16_ref, e_ref, gb_ref,
                    a_ref, c_ref, w_ref, x_ref, y_ref, s4, s16, ls4, ls16, bs, bn):
  step = pl.program_id(0)
  b0 = SGU_WIDTH
  c0 = SGU_WIDTH + DIL_WIDTH

  def merge_into(slot):
    _merge_tile(o1_ref, o4_ref, o16_ref, l1_ref, l4_ref, l16_ref, e_ref, gb_ref, bn.at[slot],
                s4, s16, ls4, ls16, bs)

  @pl.when(step == 0)
  def _():
    merge_into(0)

  @pl.when(step > 0)
  def _():
    slot = step % 2
    merge_into(slot)
    b_prev = bn.at[1 - slot]
    for chunk in range(D_MODEL // OUT_CHUNK_N):
      cols = slice(chunk * OUT_CHUNK_N, (chunk + 1) * OUT_CHUNK_N)
      acc = jnp.dot(a_ref[...], w_ref[0:b0, cols], preferred_element_type=F32)
      acc += jnp.dot(b_prev[...], w_ref[b0:c0, cols], preferred_element_type=F32)
      acc += jnp.dot(c_ref[...], w_ref[c0:, cols], preferred_element_type=F32)
      y_ref[:, cols] = x_ref[:, cols] + acc


def _mix_out(o_list, lse_list, g_b, a_n, c_n, w_out, layer, x, *, tm):
  o1, o4, o16 = o_list
  l1, l4, l16 = lse_list
  tok = x.shape[0]
  n = tok // tm
  head_of_lane = jnp.arange(DIL_WIDTH, dtype=jnp.int32) // HEAD_DIM
  selector = (jnp.arange(HEAD_DIM, dtype=jnp.int32)[:, None] == head_of_lane[None, :]).astype(BF16)
  selector = jnp.concatenate([selector, selector], axis=0)
  head_scratch = pltpu.VMEM((N_DIL_HEADS, tm, HEAD_DIM), F32)
  ahead = lambda s: jnp.minimum(s, n - 1)
  behind = lambda s: jnp.maximum(s - 1, 0)
  const = lambda shape: pl.BlockSpec(shape, lambda s: (0,) * len(shape))
  row = lambda w: pl.BlockSpec((tm, w), lambda s: (behind(s), 0))
  return pl.pallas_call(
      _mix_out_kernel,
      grid=(n + 1,),
      in_specs=[pl.BlockSpec((None, tm, DIL_WIDTH), lambda s: (0, ahead(s), 0)),
                pl.BlockSpec((D4, tm // D4, DIL_WIDTH), lambda s: (0, ahead(s), 0)),
                pl.BlockSpec((D16, tm // D16, DIL_WIDTH), lambda s: (0, ahead(s), 0)),
                pl.BlockSpec((None, tm, HEAD_DIM), lambda s: (0, ahead(s), 0)),
                pl.BlockSpec((D4, tm // D4, HEAD_DIM), lambda s: (0, ahead(s), 0)),
                pl.BlockSpec((D16, tm // D16, HEAD_DIM), lambda s: (0, ahead(s), 0)),
                const((2 * HEAD_DIM, DIL_WIDTH)),
                const((1, DIL_WIDTH)),
                row(SGU_WIDTH), row(MEM_WIDTH),
                pl.BlockSpec((None, MIX_WIDTH, D_MODEL), lambda s: (layer, 0, 0),
                             pipeline_mode=pl.Buffered(1)),
                row(D_MODEL)],
      out_specs=row(D_MODEL),
      out_shape=jax.ShapeDtypeStruct((tok, D_MODEL), F32),
      scratch_shapes=[head_scratch, head_scratch,
                      pltpu.VMEM((tm, HEAD_DIM), F32), pltpu.VMEM((tm, HEAD_DIM), F32),
                      head_scratch, pltpu.VMEM((2, tm, DIL_WIDTH), BF16)],
      compiler_params=_params("arbitrary"),
      name="mix_out",
  )(o1, o4, o16, l1, l4, l16, selector, g_b, a_n, c_n, w_out, x)


def _ffn_kernel(x_ref, g_ref, wg_ref, wu_ref, wd_ref, gf_ref, y_ref, h_ref, *, final_norm):
  k = pl.program_id(1)

  @pl.when(k == 0)
  def _():
    xf = x_ref[...]
    h_ref[...] = (xf * _rms_scale(xf) * g_ref[...]).astype(BF16)
    y_ref[...] = xf

  h = h_ref[...]
  gate = jnp.dot(h, wg_ref[...], preferred_element_type=F32)
  up = jnp.dot(h, wu_ref[...], preferred_element_type=F32)
  act = (jax.nn.silu(gate) * up).astype(BF16)
  y_ref[...] += jnp.dot(act, wd_ref[...], preferred_element_type=F32)

  if final_norm:
    @pl.when(k == pl.num_programs(1) - 1)
    def _():
      y = y_ref[...]
      y_ref[...] = y * _rms_scale(y) * gf_ref[...]


def _ffn(x, g, w_gu, w_down, layer, g_final, *, tm, th, final_norm, row0=0, rows=None):
  tok = x.shape[0] if rows is None else rows
  tile0 = row0 // tm
  nk = FFN_HIDDEN // th
  return pl.pallas_call(
      functools.partial(_ffn_kernel, final_norm=final_norm),
      grid=(tok // tm, nk),
      in_specs=[
          pl.BlockSpec((tm, D_MODEL), lambda i, k: (tile0 + i, 0)),
          pl.BlockSpec((1, D_MODEL), lambda i, k: (0, 0)),
          pl.BlockSpec((None, D_MODEL, th), lambda i, k: (layer, 0, k)),
          pl.BlockSpec((None, D_MODEL, th), lambda i, k: (layer, 0, nk + k)),
          pl.BlockSpec((None, th, D_MODEL), lambda i, k: (layer, k, 0)),
          pl.BlockSpec((1, D_MODEL), lambda i, k: (0, 0)),
      ],
      out_specs=pl.BlockSpec((tm, D_MODEL), lambda i, k: (i, 0)),
      out_shape=jax.ShapeDtypeStruct((tok, D_MODEL), F32),
      scratch_shapes=[pltpu.VMEM((tm, D_MODEL), BF16)],
      compiler_params=_params("parallel", "arbitrary", vmem=BIG_TILE_VMEM_LIMIT),
      name="ffn",
  )(x, g, w_gu, w_gu, w_down, g_final)


def _rope_tables(positions):
  inv = ROPE_THETA ** (-jnp.arange(0, ROPE_DIM, 2, dtype=F32) / ROPE_DIM)
  ang = positions.astype(F32)[:, None] * inv[None, :]
  cos, sin = jnp.cos(ang), jnp.sin(ang)
  n = positions.shape[0]
  rest = HEAD_DIM - ROPE_DIM
  c = jnp.concatenate([cos, cos, jnp.ones((n, rest), F32)], axis=1)
  s1 = jnp.concatenate([jnp.zeros((n, ROPE_HALF), F32), sin, jnp.zeros((n, rest), F32)], axis=1)
  s2 = jnp.concatenate([-sin, jnp.zeros((n, HEAD_DIM - ROPE_HALF), F32)], axis=1)
  return c, s1, s2


def kernel(x_prompt, x_sample, mem_prompt, mem_sample, g_mix_norm, w_in, g_sgu, w_spatial,
           b_spatial, g_mem_norm, w_mem_kv, g_group_out, w_out, g_ffn_norm, w_gate_up,
           w_down, g_final):
  n_prompt, s_prompt, _ = x_prompt.shape
  n_sample, s_sample, _ = x_sample.shape
  assert n_prompt == 1 and s_prompt % s_sample == 0
  assert s_sample % (D16 * DIL_SUB) == 0 and s_sample % TM_FFN == 0
  prompt_rows = n_prompt * s_prompt
  depth = w_in.shape[0]

  x = jnp.concatenate([x_prompt.reshape(prompt_rows, D_MODEL),
                       x_sample.reshape(n_sample * s_sample, D_MODEL)], axis=0)
  mem = jnp.concatenate([mem_prompt, mem_sample], axis=0)
  n_mems = mem.shape[0]
  mem = mem.reshape(n_mems * N_MEM, D_MODEL)
  positions = jnp.concatenate([jnp.arange(s_prompt, dtype=jnp.int32)] * n_prompt
                              + [jnp.arange(s_sample, dtype=jnp.int32)] * n_sample)
  rope = _rope_tables(positions)
  seqs = dict(prompt_rows=prompt_rows, s_prompt=s_prompt, s_sample=s_sample)
  w_in, w_spatial, w_mem_kv, w_out, w_gate_up, w_down = (
      w.astype(BF16) for w in (w_in, w_spatial, w_mem_kv, w_out, w_gate_up, w_down))

  kv = _norm_matmul_layers(mem, g_mem_norm, w_mem_kv, tm=N_MEM, tn=MEM_WIDTH, name="mem_kv")
  kv = kv.reshape(depth, n_mems, N_MEM, 2 * MEM_WIDTH)

  for l in range(depth):
    g_out = g_group_out[l].reshape(1, MIX_WIDTH)
    b_sp = jnp.broadcast_to(b_spatial[l][:, :, None], (N_SGU_GROUPS, SGU_CHUNK, HEAD_DIM))
    zq, c4, c16, a_n, c_n = _in_proj(
        x, g_mix_norm[l], w_in, l, rope, kv, g_sgu[l].reshape(1, SGU_WIDTH), w_spatial, b_sp,
        g_out, tm=TM_IN_PROJ, rows_per_mem=s_sample,
        first_sample_mem_tile=prompt_rows // s_sample)
    o_list, lse_list = [], []
    for d, src in ((1, zq.reshape(1, *zq.shape)), (D4, c4), (D16, c16)):
      bq = min(DIL_ROWS_PER_STEP, s_sample // d)
      o, lse = _dilated_branch(src, d=d, bq=bq, cps=min(d, DIL_ROWS_PER_STEP // bq), **seqs)
      o_list.append(o)
      lse_list.append(lse)
    x = _mix_out(o_list, lse_list, g_out[:, SGU_WIDTH:SGU_WIDTH + DIL_WIDTH], a_n, c_n, w_out, l, x,
                 tm=TM_MIX_OUT)
    ffn = functools.partial(_ffn, x, g_ffn_norm[l].reshape(1, D_MODEL), w_gate_up, w_down, l,
                            g_final.reshape(1, D_MODEL), tm=TM_FFN, th=TH_FFN)
    if l < depth - 1:
      x = ffn(final_norm=False)

  y_prompt = ffn(final_norm=True, row0=0, rows=prompt_rows)
  y_sample = ffn(final_norm=True, row0=prompt_rows, rows=n_sample * s_sample)
  return (y_prompt.reshape(n_prompt, s_prompt, D_MODEL),
          y_sample.reshape(n_sample, s_sample, D_MODEL))
```

```python
import functools
import math

import jax
import jax.numpy as jnp
from jax import lax
from jax.experimental import pallas as pl
from jax.experimental.pallas import tpu as pltpu

F32 = jnp.float32
BF16 = jnp.bfloat16

D_MODEL = 2048
HEAD_DIM = 128
N_SGU_GROUPS = 4
SGU_WIDTH = N_SGU_GROUPS * HEAD_DIM
SGU_CHUNK = 128
N_DIL_HEADS = 8
DIL_WIDTH = N_DIL_HEADS * HEAD_DIM
DILATIONS = (1, 4, 16)
HALF = 64
N_MEM_HEADS = 4
MEM_WIDTH = N_MEM_HEADS * HEAD_DIM
N_MEM = 256
MIX_WIDTH = SGU_WIDTH + DIL_WIDTH + MEM_WIDTH
IN_WIDTH = 2 * SGU_WIDTH + 3 * DIL_WIDTH + MEM_WIDTH
ROPE_THETA = 500000.0
ROPE_DIM = HEAD_DIM // 4
ROPE_HALF = ROPE_DIM // 2
FFN_HIDDEN = 5632
EPS = 1e-6
NEG_INF = -1e30
ATTN_SCALE = HEAD_DIM ** -0.5
DIL_Q_SCALE = ATTN_SCALE * math.log2(math.e)

V7X_VMEM_BYTES = 64 * 1024 * 1024
VMEM_LIMIT = V7X_VMEM_BYTES - 8 * 1024 * 1024

IN_CHUNK_N = 256
IN_SLABS = 4
BIG_TILE_VMEM_LIMIT = V7X_VMEM_BYTES - 3 * 1024 * 1024
IN_V0 = SGU_WIDTH
IN_QKV0 = 2 * SGU_WIDTH
IN_QC0 = IN_QKV0 + 3 * DIL_WIDTH
QKV_WIDTH = 3 * DIL_WIDTH
D4, D16 = DILATIONS[1], DILATIONS[2]
D16_PER_D4 = D16 // D4

TM_IN_PROJ = 512
TM_MIX_OUT = 512
OUT_CHUNK_N = 256
TM_FFN = 1024
TH_FFN = 512
DIL_ROWS_PER_STEP = 1024


def _params(*sem, vmem=VMEM_LIMIT):
  return pltpu.CompilerParams(dimension_semantics=sem, vmem_limit_bytes=vmem)


def _rms_scale(x):
  return lax.rsqrt(jnp.mean(x * x, axis=-1, keepdims=True) + EPS)


def _norm_matmul_kernel(x_ref, g_ref, w_ref, o_ref, h_ref):
  @pl.when(pl.program_id(2) == 0)
  def _():
    xf = x_ref[...]
    h_ref[...] = (xf * _rms_scale(xf) * g_ref[...]).astype(BF16)

  o_ref[...] = jnp.dot(h_ref[...], w_ref[...], preferred_element_type=F32).astype(o_ref.dtype)


def _norm_matmul_layers(x, g, w, *, tm, tn, name):
  m, k = x.shape
  depth, _, n = w.shape
  return pl.pallas_call(
      _norm_matmul_kernel,
      grid=(depth, m // tm, n // tn),
      in_specs=[pl.BlockSpec((tm, k), lambda l, i, j: (i, 0)),
                pl.BlockSpec((None, 1, k), lambda l, i, j: (l, 0, 0)),
                pl.BlockSpec((None, k, tn), lambda l, i, j: (l, 0, j))],
      out_specs=pl.BlockSpec((None, tm, tn), lambda l, i, j: (l, i, j)),
      out_shape=jax.ShapeDtypeStruct((depth, m, n), BF16),
      scratch_shapes=[pltpu.VMEM((tm, k), BF16)],
      compiler_params=_params("parallel", "parallel", "arbitrary"),
      name=name,
  )(x, g.reshape(depth, 1, k), w)


def _in_col_kind(col):
  bounds = ((IN_V0, "u"), (IN_QKV0, "v"), (IN_QKV0 + DIL_WIDTH, "q"),
            (IN_QKV0 + 2 * DIL_WIDTH, "k"), (IN_QC0, "vb"))
  for end, kind in bounds:
    if col < end:
      return kind
  return "qc"


def _mem_attention(q_heads, kv_ref, go_ref, o_ref):
  outs = []
  ssq = jnp.zeros((q_heads[0].shape[0], 1), F32)
  for h, q in enumerate(q_heads):
    k = kv_ref[:, h * HEAD_DIM:(h + 1) * HEAD_DIM]
    v = kv_ref[:, MEM_WIDTH + h * HEAD_DIM:MEM_WIDTH + (h + 1) * HEAD_DIM]
    s = lax.dot_general(q, k, (((1,), (1,)), ((), ())), preferred_element_type=F32) * ATTN_SCALE
    m = jnp.max(s, axis=-1, keepdims=True)
    p = jnp.exp(s - m)
    den = jnp.sum(p, axis=-1, keepdims=True)
    o = jnp.dot(p.astype(BF16), v, preferred_element_type=F32) / den
    ssq = ssq + jnp.sum(o * o, axis=-1, keepdims=True)
    outs.append(o)
  scale = lax.rsqrt(ssq * (1.0 / MEM_WIDTH) + EPS)
  for h in range(N_MEM_HEADS):
    cs = slice(h * HEAD_DIM, (h + 1) * HEAD_DIM)
    o_ref[:, cs] = (outs[h] * scale * go_ref[:, cs]).astype(o_ref.dtype)


def _spatial_gating(uv_ref, gs_ref, w_ref, b_ref, go_ref, o_ref):
  for c in range(uv_ref.shape[0] // SGU_CHUNK):
    rs = slice(c * SGU_CHUNK, (c + 1) * SGU_CHUNK)
    outs = []
    ssq = jnp.zeros((SGU_CHUNK, 1), F32)
    for g in range(N_SGU_GROUPS):
      cs = slice(g * HEAD_DIM, (g + 1) * HEAD_DIM)
      v = uv_ref[rs, IN_V0 + g * HEAD_DIM:IN_V0 + (g + 1) * HEAD_DIM]
      vv = v * _rms_scale(v) * gs_ref[:, cs]
      vs = jnp.dot(w_ref[g], vv.astype(BF16), preferred_element_type=F32) + b_ref[g]
      a = uv_ref[rs, cs] * vs
      ssq = ssq + jnp.sum(a * a, axis=-1, keepdims=True)
      outs.append(a)
    scale = lax.rsqrt(ssq * (1.0 / SGU_WIDTH) + EPS)
    for g in range(N_SGU_GROUPS):
      cs = slice(g * HEAD_DIM, (g + 1) * HEAD_DIM)
      o_ref[rs, cs] = (outs[g] * scale * go_ref[:, cs]).astype(o_ref.dtype)


def _in_proj_kernel(x_ref, g_ref, w_ref, c_ref, s1_ref, s2_ref, kv_ref, gs_ref, wsp_ref, bsp_ref,
                    go_ref, z_ref, c4_ref, c16_ref, a_ref, cn_ref, h_ref, slab, slab4, uv_ref):
  tm = x_ref.shape[0]
  xf = x_ref[...]
  h_ref[...] = (xf * _rms_scale(xf) * g_ref[...]).astype(BF16)
  c, s1, s2 = c_ref[...], s1_ref[...], s2_ref[...]
  tables = {"k": (c, s1, s2), "q": (c * DIL_Q_SCALE, s1 * DIL_Q_SCALE, s2 * DIL_Q_SCALE)}
  n_slabs = slab.shape[0]
  heads_per_chunk = IN_CHUNK_N // HEAD_DIM
  n_chunks = IN_WIDTH // IN_CHUNK_N
  first_qc, first_qkv = IN_QC0 // IN_CHUNK_N, IN_QKV0 // IN_CHUNK_N
  order = (list(range(first_qc, n_chunks)) + list(range(first_qkv))
           + list(range(first_qkv, first_qc)))
  qc_heads = []
  for chunk in order:
    col0 = chunk * IN_CHUNK_N
    acc = jnp.dot(h_ref[...], w_ref[:, col0:col0 + IN_CHUNK_N], preferred_element_type=F32)
    for hh in range(heads_per_chunk):
      col = col0 + hh * HEAD_DIM
      kind = _in_col_kind(col)
      t = acc[:, hh * HEAD_DIM:(hh + 1) * HEAD_DIM]
      if kind == "qc":
        qc_heads.append(t.astype(BF16))
        continue
      if kind in ("u", "v"):
        uv_ref[:, col:col + HEAD_DIM] = jax.nn.gelu(t)
        continue
      if kind in tables:
        tc, ts1, ts2 = tables[kind]
        t = (t * tc + pltpu.roll(t, ROPE_HALF, 1) * ts1
             + pltpu.roll(t, HEAD_DIM - ROPE_HALF, 1) * ts2)
      cs = slice(col - IN_QKV0, col - IN_QKV0 + HEAD_DIM)
      z_ref[:, cs] = t.astype(z_ref.dtype)
      sl = (chunk * heads_per_chunk + hh) % n_slabs
      slab[sl] = t
      for rho in range(D4):
        t4 = slab.at[sl][pl.ds(rho, tm // D4, stride=D4), :]
        c4_ref[rho, :, cs] = t4.astype(c4_ref.dtype)
        slab4[sl, rho] = t4
        for q in range(D16_PER_D4):
          t16 = slab4.at[sl, rho][pl.ds(q, tm // D16, stride=D16_PER_D4), :]
          c16_ref[D4 * q + rho, :, cs] = t16.astype(c16_ref.dtype)
    if chunk == n_chunks - 1:
      _mem_attention(qc_heads, kv_ref, go_ref.at[:, SGU_WIDTH + DIL_WIDTH:], cn_ref)
    if chunk == first_qkv - 1:
      _spatial_gating(uv_ref, gs_ref, wsp_ref, bsp_ref, go_ref.at[:, :SGU_WIDTH], a_ref)


N_IN_PROJ_INPUTS = 11
N_IN_PROJ_OUTPUTS = 5


def _in_proj_kernel_into(*refs):
  n_in, n_out = N_IN_PROJ_INPUTS, N_IN_PROJ_OUTPUTS
  _in_proj_kernel(*refs[:n_in], *refs[n_in + n_out:])


def _in_proj(x, g, w, layer, rope, kv, g_sgu, w_sp, b_sp, g_out, *, tm, tok, row0, rows_per_mem,
             first_sample_mem_tile, into=None):
  k = x.shape[1]
  tile0 = row0 // tm
  tiles_per_mem = rows_per_mem // tm

  def kv_map(i):
    mem_tile = (tile0 + i) // tiles_per_mem
    return (layer, jnp.maximum(mem_tile - first_sample_mem_tile + 1, 0), 0, 0)

  const = lambda shape: pl.BlockSpec(shape, lambda i: (0,) * len(shape))
  in_specs = ([pl.BlockSpec((tm, k), lambda i: (i, 0)),
               const((1, k)),
               pl.BlockSpec((None, k, IN_WIDTH), lambda i: (layer, 0, 0),
                            pipeline_mode=pl.Buffered(1))]
              + [pl.BlockSpec((tm, HEAD_DIM), lambda i: (tile0 + i, 0))] * 3
              + [pl.BlockSpec((None, None, N_MEM, 2 * MEM_WIDTH), kv_map),
                 const((1, SGU_WIDTH)),
                 pl.BlockSpec((None, N_SGU_GROUPS, SGU_CHUNK, SGU_CHUNK),
                              lambda i: (layer, 0, 0, 0)),
                 const((N_SGU_GROUPS, SGU_CHUNK, HEAD_DIM)),
                 const((1, MIX_WIDTH))])
  args = [x, g.reshape(1, k), w, *rope, kv, g_sgu, w_sp, b_sp, g_out]
  assert len(in_specs) == N_IN_PROJ_INPUTS
  body, aliases = _in_proj_kernel, {}
  if into is not None:
    body = _in_proj_kernel_into
    in_specs += [pl.BlockSpec(memory_space=pl.ANY)] * N_IN_PROJ_OUTPUTS
    aliases = {N_IN_PROJ_INPUTS + j: j for j in range(N_IN_PROJ_OUTPUTS)}
    args += list(into)
  return pl.pallas_call(
      body,
      grid=(x.shape[0] // tm,),
      in_specs=in_specs,
      out_specs=[pl.BlockSpec((tm, QKV_WIDTH), lambda i: (tile0 + i, 0)),
                 pl.BlockSpec((D4, tm // D4, QKV_WIDTH), lambda i: (0, tile0 + i, 0)),
                 pl.BlockSpec((D16, tm // D16, QKV_WIDTH), lambda i: (0, tile0 + i, 0)),
                 pl.BlockSpec((tm, SGU_WIDTH), lambda i: (tile0 + i, 0)),
                 pl.BlockSpec((tm, MEM_WIDTH), lambda i: (tile0 + i, 0))],
      out_shape=[jax.ShapeDtypeStruct((tok, QKV_WIDTH), BF16),
                 jax.ShapeDtypeStruct((D4, tok // D4, QKV_WIDTH), BF16),
                 jax.ShapeDtypeStruct((D16, tok // D16, QKV_WIDTH), BF16),
                 jax.ShapeDtypeStruct((tok, SGU_WIDTH), BF16),
                 jax.ShapeDtypeStruct((tok, MEM_WIDTH), BF16)],
      scratch_shapes=[pltpu.VMEM((tm, k), BF16),
                      pltpu.VMEM((IN_SLABS, tm, HEAD_DIM), F32),
                      pltpu.VMEM((IN_SLABS, D4, tm // D4, HEAD_DIM), F32),
                      pltpu.VMEM((tm, 2 * SGU_WIDTH), F32)],
      input_output_aliases=aliases,
      compiler_params=_params("parallel", vmem=BIG_TILE_VMEM_LIMIT),
      name="in_proj",
  )(*args)


DIL_SUB = 2 * HALF
DIL_KEYS = DIL_SUB + 2 * HALF


def _dil_kernel(q_ref, kp_ref, kc_ref, kn_ref, vp_ref, vc_ref, vn_ref, o_ref, lse_ref,
                kbuf, vbuf, *, class_len_prompt, class_len_sample, prompt_rows):
  cps, bq = q_ref.shape[0], q_ref.shape[1]
  row0 = pl.program_id(1) * bq
  in_prompt = row0 < prompt_rows
  clen = jnp.where(in_prompt, class_len_prompt, class_len_sample)
  pos = jnp.where(in_prompt, row0, row0 - prompt_rows) & (clen - 1)
  first = pos == 0
  last = pos + bq == clen

  for cls in range(cps):
    kbuf[cls, 0:HALF, :] = kp_ref[cls]
    kbuf[cls, HALF:HALF + bq, :] = kc_ref[cls]
    kbuf[cls, HALF + bq:, :] = kn_ref[cls]
    vbuf[cls, 0:HALF, :] = vp_ref[cls]
    vbuf[cls, HALF:HALF + bq, :] = vc_ref[cls]
    vbuf[cls, HALF + bq:, :] = vn_ref[cls]

  r = lax.broadcasted_iota(jnp.int32, (DIL_SUB, DIL_KEYS), 0)
  c = lax.broadcasted_iota(jnp.int32, (DIL_SUB, DIL_KEYS), 1)
  band = jnp.where((c >= r) & (c <= r + 2 * HALF), 0.0, NEG_INF).astype(F32)
  lane = lax.broadcasted_iota(jnp.int32, (DIL_SUB, HEAD_DIM), 1)
  lo = jnp.where(c < HALF, jnp.where(first, NEG_INF, 0.0), 0.0).astype(F32)
  hi = jnp.where(c >= DIL_KEYS - HALF, jnp.where(last, NEG_INF, 0.0), 0.0).astype(F32)
  nsub = bq // DIL_SUB
  biases = [band] * nsub
  biases[0] = biases[0] + lo
  biases[-1] = biases[-1] + hi

  for cls in range(cps):
    for j in range(nsub):
      rs = slice(j * DIL_SUB, (j + 1) * DIL_SUB)
      ks = slice(j * DIL_SUB, j * DIL_SUB + DIL_KEYS)
      lse_tile = jnp.zeros((DIL_SUB, HEAD_DIM), F32)
      for h in range(N_DIL_HEADS):
        cs = slice(h * HEAD_DIM, (h + 1) * HEAD_DIM)
        s = lax.dot_general(q_ref[cls, rs, cs], kbuf[cls, ks, cs], (((1,), (1,)), ((), ())),
                            preferred_element_type=F32) + biases[j]
        m = jnp.max(s, axis=-1, keepdims=True)
        p = jnp.exp2(s - m)
        den = jnp.sum(p, axis=-1, keepdims=True)
        num = jnp.dot(p.astype(BF16), vbuf[cls, ks, cs], preferred_element_type=F32)
        o_ref[cls, rs, cs] = (num / den).astype(o_ref.dtype)
        lse_tile = jnp.where(lane == h, m + jnp.log2(den), lse_tile)
      lse_ref[cls, rs, :] = lse_tile


def _dilated_branch(src, *, d, bq, cps, prompt_rows, s_prompt, s_sample):
  rows = src.shape[1]
  hb = bq // HALF
  n_half_blocks = rows // HALF
  body = functools.partial(
      _dil_kernel, class_len_prompt=s_prompt // d, class_len_sample=s_sample // d,
      prompt_rows=prompt_rows // d)

  def cur(c):
    return pl.BlockSpec((cps, bq, DIL_WIDTH), lambda r, i: (r, i, c))

  def prev(c):
    return pl.BlockSpec((cps, HALF, DIL_WIDTH), lambda r, i: (r, jnp.maximum(i * hb - 1, 0), c))

  def nxt(c):
    return pl.BlockSpec((cps, HALF, DIL_WIDTH),
                        lambda r, i: (r, jnp.minimum((i + 1) * hb, n_half_blocks - 1), c))

  return pl.pallas_call(
      body,
      grid=(d // cps, rows // bq),
      in_specs=[cur(0), prev(1), cur(1), nxt(1), prev(2), cur(2), nxt(2)],
      out_specs=[pl.BlockSpec((cps, bq, DIL_WIDTH), lambda r, i: (r, i, 0)),
                 pl.BlockSpec((cps, bq, HEAD_DIM), lambda r, i: (r, i, 0))],
      out_shape=[jax.ShapeDtypeStruct((d, rows, DIL_WIDTH), BF16),
                 jax.ShapeDtypeStruct((d, rows, HEAD_DIM), F32)],
      scratch_shapes=[pltpu.VMEM((cps, bq + 2 * HALF, DIL_WIDTH), BF16),
                      pltpu.VMEM((cps, bq + 2 * HALF, DIL_WIDTH), BF16)],
      compiler_params=_params("parallel", "parallel"),
      name=f"dilated_d{d}",
  )(*([src] * 7))


def _merge_tile(o1_ref, o4_ref, o16_ref, l1_ref, l4_ref, l16_ref, e_ref, gb_ref, b_ref,
                s4, s16, ls4, ls16, bs):
  tm = o1_ref.shape[0]
  for rho in range(D4):
    ls4[pl.ds(rho, tm // D4, stride=D4), :] = l4_ref[rho]
  for r in range(D16):
    ls16[pl.ds(r, tm // D16, stride=D16), :] = l16_ref[r]

  l1, l2, l3 = l1_ref[...], ls4[...], ls16[...]
  m = jnp.maximum(jnp.maximum(l1, l2), l3)
  w1, w2, w3 = jnp.exp2(l1 - m), jnp.exp2(l2 - m), jnp.exp2(l3 - m)
  inv = 1.0 / (w1 + w2 + w3)

  def over_head_lanes(w):
    wn = w * inv
    hi = wn.astype(BF16)
    lo = (wn - hi.astype(F32)).astype(BF16)
    return jnp.dot(jnp.concatenate([hi, lo], axis=1), e_ref[...], preferred_element_type=F32)

  wb1, wb2, wb3 = over_head_lanes(w1), over_head_lanes(w2), over_head_lanes(w3)

  ssq = jnp.zeros((tm, 1), F32)
  for h in range(N_DIL_HEADS):
    cs = slice(h * HEAD_DIM, (h + 1) * HEAD_DIM)
    for rho in range(D4):
      s4.at[h][pl.ds(rho, tm // D4, stride=D4), :] = o4_ref[rho, :, cs].astype(F32)
    for r in range(D16):
      s16.at[h][pl.ds(r, tm // D16, stride=D16), :] = o16_ref[r, :, cs].astype(F32)
    b = wb1[:, cs] * o1_ref[:, cs].astype(F32) + wb2[:, cs] * s4[h] + wb3[:, cs] * s16[h]
    ssq = ssq + jnp.sum(b * b, axis=-1, keepdims=True)
    bs[h] = b
  scale = jnp.broadcast_to(lax.rsqrt(ssq * (1.0 / DIL_WIDTH) + EPS), (tm, HEAD_DIM))
  for h in range(N_DIL_HEADS):
    cs = slice(h * HEAD_DIM, (h + 1) * HEAD_DIM)
    b_ref[:, cs] = (bs[h] * scale * gb_ref[:, cs]).astype(b_ref.dtype)


def _mix_out_kernel(o1_ref, o4_ref, o16_ref, l1_ref, l4_ref, l16_ref, e_ref, gb_ref,
                    a_ref, c_ref, w_ref, *rest, prompt_tiles):
  if prompt_tiles is None:
    x_ref, y_ref, s4, s16, ls4, ls16, bs, bn = rest
    xs_ref = None
  else:
    x_ref, xs_ref, y_ref, s4, s16, ls4, ls16, bs, bn = rest
  step = pl.program_id(0)
  b0 = SGU_WIDTH
  c0 = SGU_WIDTH + DIL_WIDTH

  def merge_into(slot):
    _merge_tile(o1_ref, o4_ref, o16_ref, l1_ref, l4_ref, l16_ref, e_ref, gb_ref, bn.at[slot],
                s4, s16, ls4, ls16, bs)

  @pl.when(step == 0)
  def _():
    merge_into(0)

  @pl.when(step > 0)
  def _():
    slot = step % 2
    merge_into(slot)
    b_prev = bn.at[1 - slot]
    for chunk in range(D_MODEL // OUT_CHUNK_N):
      cols = slice(chunk * OUT_CHUNK_N, (chunk + 1) * OUT_CHUNK_N)
      acc = jnp.dot(a_ref[...], w_ref[0:b0, cols], preferred_element_type=F32)
      acc += jnp.dot(b_prev[...], w_ref[b0:c0, cols], preferred_element_type=F32)
      acc += jnp.dot(c_ref[...], w_ref[c0:, cols], preferred_element_type=F32)
      res = x_ref[:, cols]
      if xs_ref is not None:
        res = jnp.where(step - 1 < prompt_tiles, res, xs_ref[:, cols])
      y_ref[:, cols] = res + acc


def _mix_out(o_list, lse_list, g_b, a_n, c_n, w_out, layer, x, *, tm):
  o1, o4, o16 = o_list
  l1, l4, l16 = lse_list
  tok = a_n.shape[0]
  n = tok // tm
  head_of_lane = jnp.arange(DIL_WIDTH, dtype=jnp.int32) // HEAD_DIM
  selector = (jnp.arange(HEAD_DIM, dtype=jnp.int32)[:, None] == head_of_lane[None, :]).astype(BF16)
  selector = jnp.concatenate([selector, selector], axis=0)
  head_scratch = pltpu.VMEM((N_DIL_HEADS, tm, HEAD_DIM), F32)
  ahead = lambda s: jnp.minimum(s, n - 1)
  behind = lambda s: jnp.maximum(s - 1, 0)
  const = lambda shape: pl.BlockSpec(shape, lambda s: (0,) * len(shape))
  row = lambda w: pl.BlockSpec((tm, w), lambda s: (behind(s), 0))
  if isinstance(x, tuple):
    prompt_tiles = x[0].shape[0] // tm
    x_specs = [pl.BlockSpec((tm, D_MODEL), lambda s: (jnp.minimum(behind(s), prompt_tiles - 1), 0)),
               pl.BlockSpec((tm, D_MODEL), lambda s: (jnp.maximum(behind(s) - prompt_tiles, 0), 0))]
  else:
    prompt_tiles, x_specs, x = None, [row(D_MODEL)], (x,)
  return pl.pallas_call(
      functools.partial(_mix_out_kernel, prompt_tiles=prompt_tiles),
      grid=(n + 1,),
      in_specs=[pl.BlockSpec((None, tm, DIL_WIDTH), lambda s: (0, ahead(s), 0)),
                pl.BlockSpec((D4, tm // D4, DIL_WIDTH), lambda s: (0, ahead(s), 0)),
                pl.BlockSpec((D16, tm // D16, DIL_WIDTH), lambda s: (0, ahead(s), 0)),
                pl.BlockSpec((None, tm, HEAD_DIM), lambda s: (0, ahead(s), 0)),
                pl.BlockSpec((D4, tm // D4, HEAD_DIM), lambda s: (0, ahead(s), 0)),
                pl.BlockSpec((D16, tm // D16, HEAD_DIM), lambda s: (0, ahead(s), 0)),
                const((2 * HEAD_DIM, DIL_WIDTH)),
                const((1, DIL_WIDTH)),
                row(SGU_WIDTH), row(MEM_WIDTH),
                pl.BlockSpec((None, MIX_WIDTH, D_MODEL), lambda s: (layer, 0, 0),
                             pipeline_mode=pl.Buffered(1))] + x_specs,
      out_specs=row(D_MODEL),
      out_shape=jax.ShapeDtypeStruct((tok, D_MODEL), F32),
      scratch_shapes=[head_scratch, head_scratch,
                      pltpu.VMEM((tm, HEAD_DIM), F32), pltpu.VMEM((tm, HEAD_DIM), F32),
                      head_scratch, pltpu.VMEM((2, tm, DIL_WIDTH), BF16)],
      compiler_params=_params("arbitrary"),
      name="mix_out",
  )(o1, o4, o16, l1, l4, l16, selector, g_b, a_n, c_n, w_out, *x)


def _ffn_kernel(x_ref, g_ref, wg_ref, wu_ref, wd_ref, gf_ref, y_ref, h2_ref, *, final_norm):
  i, k = pl.program_id(0), pl.program_id(1)
  last = pl.num_programs(1) - 1
  slot = i % 2

  def normalize_into(s):
    xf = x_ref[...]
    h2_ref[s] = (xf * _rms_scale(xf) * g_ref[...]).astype(BF16)

  def down_projection():
    h = h2_ref[slot]
    gate = jnp.dot(h, wg_ref[...], preferred_element_type=F32)
    up = jnp.dot(h, wu_ref[...], preferred_element_type=F32)
    act = (jax.nn.silu(gate) * up).astype(BF16)
    return jnp.dot(act, wd_ref[...], preferred_element_type=F32)

  @pl.when((i == 0) & (k == 0))
  def _():
    normalize_into(0)

  @pl.when(k == 0)
  def _():
    y_ref[...] = x_ref[...] + down_projection()

  @pl.when((k > 0) & (k < last))
  def _():
    y_ref[...] += down_projection()

  @pl.when(k == last)
  def _():
    normalize_into(1 - slot)
    y = y_ref[...] + down_projection()
    if final_norm:
      y = y * _rms_scale(y) * gf_ref[...]
    y_ref[...] = y


def _ffn(x, g, w_gu, w_down, layer, g_final, *, tm, th, final_norm, row0=0, rows=None):
  tok = x.shape[0] if rows is None else rows
  tile0 = row0 // tm
  n = tok // tm
  nk = FFN_HIDDEN // th
  assert nk >= 2

  def x_map(i, k):
    return (tile0 + jnp.minimum(i + (k == nk - 1).astype(jnp.int32), n - 1), 0)

  return pl.pallas_call(
      functools.partial(_ffn_kernel, final_norm=final_norm),
      grid=(n, nk),
      in_specs=[
          pl.BlockSpec((tm, D_MODEL), x_map),
          pl.BlockSpec((1, D_MODEL), lambda i, k: (0, 0)),
          pl.BlockSpec((None, D_MODEL, th), lambda i, k: (layer, 0, k)),
          pl.BlockSpec((None, D_MODEL, th), lambda i, k: (layer, 0, nk + k)),
          pl.BlockSpec((None, th, D_MODEL), lambda i, k: (layer, k, 0)),
          pl.BlockSpec((1, D_MODEL), lambda i, k: (0, 0)),
      ],
      out_specs=pl.BlockSpec((tm, D_MODEL), lambda i, k: (i, 0)),
      out_shape=jax.ShapeDtypeStruct((tok, D_MODEL), F32),
      scratch_shapes=[pltpu.VMEM((2, tm, D_MODEL), BF16)],
      compiler_params=_params("arbitrary", "arbitrary", vmem=BIG_TILE_VMEM_LIMIT),
      name="ffn",
  )(x, g, w_gu, w_gu, w_down, g_final)


def _rope_tables(positions):
  inv = ROPE_THETA ** (-jnp.arange(0, ROPE_DIM, 2, dtype=F32) / ROPE_DIM)
  ang = positions.astype(F32)[:, None] * inv[None, :]
  cos, sin = jnp.cos(ang), jnp.sin(ang)
  n = positions.shape[0]
  rest = HEAD_DIM - ROPE_DIM
  c = jnp.concatenate([cos, cos, jnp.ones((n, rest), F32)], axis=1)
  s1 = jnp.concatenate([jnp.zeros((n, ROPE_HALF), F32), sin, jnp.zeros((n, rest), F32)], axis=1)
  s2 = jnp.concatenate([-sin, jnp.zeros((n, HEAD_DIM - ROPE_HALF), F32)], axis=1)
  return c, s1, s2


def kernel(x_prompt, x_sample, mem_prompt, mem_sample, g_mix_norm, w_in, g_sgu, w_spatial,
           b_spatial, g_mem_norm, w_mem_kv, g_group_out, w_out, g_ffn_norm, w_gate_up,
           w_down, g_final):
  n_prompt, s_prompt, _ = x_prompt.shape
  n_sample, s_sample, _ = x_sample.shape
  assert n_prompt == 1 and s_prompt % s_sample == 0
  assert s_sample % (D16 * DIL_SUB) == 0 and s_sample % TM_FFN == 0
  prompt_rows = n_prompt * s_prompt
  depth = w_in.shape[0]

  x = (x_prompt.reshape(prompt_rows, D_MODEL), x_sample.reshape(n_sample * s_sample, D_MODEL))
  tok = prompt_rows + n_sample * s_sample
  mem = jnp.concatenate([mem_prompt, mem_sample], axis=0)
  n_mems = mem.shape[0]
  mem = mem.reshape(n_mems * N_MEM, D_MODEL)
  positions = jnp.concatenate([jnp.arange(s_prompt, dtype=jnp.int32)] * n_prompt
                              + [jnp.arange(s_sample, dtype=jnp.int32)] * n_sample)
  rope = _rope_tables(positions)
  seqs = dict(prompt_rows=prompt_rows, s_prompt=s_prompt, s_sample=s_sample)
  w_in, w_spatial, w_mem_kv, w_out, w_gate_up, w_down = (
      w.astype(BF16) for w in (w_in, w_spatial, w_mem_kv, w_out, w_gate_up, w_down))

  kv = _norm_matmul_layers(mem, g_mem_norm, w_mem_kv, tm=N_MEM, tn=MEM_WIDTH, name="mem_kv")
  kv = kv.reshape(depth, n_mems, N_MEM, 2 * MEM_WIDTH)

  for l in range(depth):
    g_out = g_group_out[l].reshape(1, MIX_WIDTH)
    b_sp = jnp.broadcast_to(b_spatial[l][:, :, None], (N_SGU_GROUPS, SGU_CHUNK, HEAD_DIM))
    in_proj = functools.partial(
        _in_proj, g=g_mix_norm[l], w=w_in, layer=l, rope=rope, kv=kv,
        g_sgu=g_sgu[l].reshape(1, SGU_WIDTH), w_sp=w_spatial, b_sp=b_sp, g_out=g_out,
        tm=TM_IN_PROJ, tok=tok, rows_per_mem=s_sample,
        first_sample_mem_tile=prompt_rows // s_sample)
    if isinstance(x, tuple):
      mixed = in_proj(x[0], row0=0)
      mixed = in_proj(x[1], row0=prompt_rows, into=mixed)
    else:
      mixed = in_proj(x, row0=0)
    zq, c4, c16, a_n, c_n = mixed
    o_list, lse_list = [], []
    for d, src in ((1, zq.reshape(1, *zq.shape)), (D4, c4), (D16, c16)):
      bq = min(DIL_ROWS_PER_STEP, s_sample // d)
      o, lse = _dilated_branch(src, d=d, bq=bq, cps=min(d, DIL_ROWS_PER_STEP // bq), **seqs)
      o_list.append(o)
      lse_list.append(lse)
    x = _mix_out(o_list, lse_list, g_out[:, SGU_WIDTH:SGU_WIDTH + DIL_WIDTH], a_n, c_n, w_out, l, x,
                 tm=TM_MIX_OUT)
    ffn = functools.partial(_ffn, x, g_ffn_norm[l].reshape(1, D_MODEL), w_gate_up, w_down, l,
                            g_final.reshape(1, D_MODEL), tm=TM_FFN, th=TH_FFN)
    if l < depth - 1:
      x = ffn(final_norm=False)

  y_prompt = ffn(final_norm=True, row0=0, rows=prompt_rows)
  y_sample = ffn(final_norm=True, row0=prompt_rows, rows=n_sample * s_sample)
  return (y_prompt.reshape(n_prompt, s_prompt, D_MODEL),
          y_sample.reshape(n_sample, s_sample, D_MODEL))
```

```python
import functools
import math

import jax
import jax.numpy as jnp
from jax import lax
from jax.experimental import pallas as pl
from jax.experimental.pallas import tpu as pltpu

F32 = jnp.float32
BF16 = jnp.bfloat16

D_MODEL = 2048
HEAD_DIM = 128
N_SGU_GROUPS = 4
SGU_WIDTH = N_SGU_GROUPS * HEAD_DIM
SGU_CHUNK = 128
N_DIL_HEADS = 8
DIL_WIDTH = N_DIL_HEADS * HEAD_DIM
DILATIONS = (1, 4, 16)
HALF = 64
N_MEM_HEADS = 4
MEM_WIDTH = N_MEM_HEADS * HEAD_DIM
N_MEM = 256
MIX_WIDTH = SGU_WIDTH + DIL_WIDTH + MEM_WIDTH
IN_WIDTH = 2 * SGU_WIDTH + 3 * DIL_WIDTH + MEM_WIDTH
ROPE_THETA = 500000.0
ROPE_DIM = HEAD_DIM // 4
ROPE_HALF = ROPE_DIM // 2
FFN_HIDDEN = 5632
EPS = 1e-6
NEG_INF = -1e30
ATTN_SCALE = HEAD_DIM ** -0.5
DIL_Q_SCALE = ATTN_SCALE * math.log2(math.e)

V7X_VMEM_BYTES = 64 * 1024 * 1024
VMEM_LIMIT = V7X_VMEM_BYTES - 8 * 1024 * 1024

IN_CHUNK_N = 256
IN_SLABS = 4
BIG_TILE_VMEM_LIMIT = V7X_VMEM_BYTES - 3 * 1024 * 1024
IN_V0 = SGU_WIDTH
IN_QKV0 = 2 * SGU_WIDTH
IN_QC0 = IN_QKV0 + 3 * DIL_WIDTH
QKV_WIDTH = 3 * DIL_WIDTH
D4, D16 = DILATIONS[1], DILATIONS[2]
D16_PER_D4 = D16 // D4

TM_IN_PROJ = 512
TM_MIX_OUT = 512
OUT_CHUNK_N = 256
TM_FFN = 1024
TH_FFN = 512
DIL_ROWS_PER_STEP = 1024


def _params(*sem, vmem=VMEM_LIMIT):
  return pltpu.CompilerParams(dimension_semantics=sem, vmem_limit_bytes=vmem)


def _rms_scale(x):
  return lax.rsqrt(jnp.mean(x * x, axis=-1, keepdims=True) + EPS)


def _norm_matmul_kernel(x_ref, g_ref, w_ref, o_ref, h_ref):
  @pl.when(pl.program_id(2) == 0)
  def _():
    xf = x_ref[...]
    h_ref[...] = (xf * _rms_scale(xf) * g_ref[...]).astype(BF16)

  o_ref[...] = jnp.dot(h_ref[...], w_ref[...], preferred_element_type=F32).astype(o_ref.dtype)


def _norm_matmul_layers(x, g, w, *, tm, tn, name):
  m, k = x.shape
  depth, _, n = w.shape
  return pl.pallas_call(
      _norm_matmul_kernel,
      grid=(depth, m // tm, n // tn),
      in_specs=[pl.BlockSpec((tm, k), lambda l, i, j: (i, 0)),
                pl.BlockSpec((None, 1, k), lambda l, i, j: (l, 0, 0)),
                pl.BlockSpec((None, k, tn), lambda l, i, j: (l, 0, j))],
      out_specs=pl.BlockSpec((None, tm, tn), lambda l, i, j: (l, i, j)),
      out_shape=jax.ShapeDtypeStruct((depth, m, n), BF16),
      scratch_shapes=[pltpu.VMEM((tm, k), BF16)],
      compiler_params=_params("parallel", "parallel", "arbitrary"),
      name=name,
  )(x, g.reshape(depth, 1, k), w)


def _in_col_kind(col):
  bounds = ((IN_V0, "u"), (IN_QKV0, "v"), (IN_QKV0 + DIL_WIDTH, "q"),
            (IN_QKV0 + 2 * DIL_WIDTH, "k"), (IN_QC0, "vb"))
  for end, kind in bounds:
    if col < end:
      return kind
  return "qc"


def _mem_attention(q_heads, kv_ref, go_ref, o_ref):
  outs = []
  ssq = jnp.zeros((q_heads[0].shape[0], 1), F32)
  for h, q in enumerate(q_heads):
    k = kv_ref[:, h * HEAD_DIM:(h + 1) * HEAD_DIM]
    v = kv_ref[:, MEM_WIDTH + h * HEAD_DIM:MEM_WIDTH + (h + 1) * HEAD_DIM]
    s = lax.dot_general(q, k, (((1,), (1,)), ((), ())), preferred_element_type=F32) * ATTN_SCALE
    m = jnp.max(s, axis=-1, keepdims=True)
    p = jnp.exp(s - m)
    den = jnp.sum(p, axis=-1, keepdims=True)
    o = jnp.dot(p.astype(BF16), v, preferred_element_type=F32) / den
    ssq = ssq + jnp.sum(o * o, axis=-1, keepdims=True)
    outs.append(o)
  scale = lax.rsqrt(ssq * (1.0 / MEM_WIDTH) + EPS)
  for h in range(N_MEM_HEADS):
    cs = slice(h * HEAD_DIM, (h + 1) * HEAD_DIM)
    o_ref[:, cs] = (outs[h] * scale * go_ref[:, cs]).astype(o_ref.dtype)


def _spatial_gating(uv_ref, gs_ref, w_ref, b_ref, go_ref, o_ref):
  for c in range(uv_ref.shape[0] // SGU_CHUNK):
    rs = slice(c * SGU_CHUNK, (c + 1) * SGU_CHUNK)
    outs = []
    ssq = jnp.zeros((SGU_CHUNK, 1), F32)
    for g in range(N_SGU_GROUPS):
      cs = slice(g * HEAD_DIM, (g + 1) * HEAD_DIM)
      v = uv_ref[rs, IN_V0 + g * HEAD_DIM:IN_V0 + (g + 1) * HEAD_DIM]
      vv = v * _rms_scale(v) * gs_ref[:, cs]
      vs = jnp.dot(w_ref[g], vv.astype(BF16), preferred_element_type=F32) + b_ref[g]
      a = uv_ref[rs, cs] * vs
      ssq = ssq + jnp.sum(a * a, axis=-1, keepdims=True)
      outs.append(a)
    scale = lax.rsqrt(ssq * (1.0 / SGU_WIDTH) + EPS)
    for g in range(N_SGU_GROUPS):
      cs = slice(g * HEAD_DIM, (g + 1) * HEAD_DIM)
      o_ref[rs, cs] = (outs[g] * scale * go_ref[:, cs]).astype(o_ref.dtype)


def _in_proj_kernel(x_ref, g_ref, w_ref, c_ref, s1_ref, s2_ref, kv_ref, gs_ref, wsp_ref, bsp_ref,
                    go_ref, z_ref, c4_ref, c16_ref, a_ref, cn_ref, h_ref, slab, slab4, uv_ref):
  tm = x_ref.shape[0]
  xf = x_ref[...]
  h_ref[...] = (xf * _rms_scale(xf) * g_ref[...]).astype(BF16)
  c, s1, s2 = c_ref[...], s1_ref[...], s2_ref[...]
  tables = {"k": (c, s1, s2), "q": (c * DIL_Q_SCALE, s1 * DIL_Q_SCALE, s2 * DIL_Q_SCALE)}
  n_slabs = slab.shape[0]
  heads_per_chunk = IN_CHUNK_N // HEAD_DIM
  n_chunks = IN_WIDTH // IN_CHUNK_N
  first_qc, first_qkv = IN_QC0 // IN_CHUNK_N, IN_QKV0 // IN_CHUNK_N
  order = (list(range(first_qc, n_chunks)) + list(range(first_qkv))
           + list(range(first_qkv, first_qc)))
  qc_heads = []
  for chunk in order:
    col0 = chunk * IN_CHUNK_N
    acc = jnp.dot(h_ref[...], w_ref[:, col0:col0 + IN_CHUNK_N], preferred_element_type=F32)
    for hh in range(heads_per_chunk):
      col = col0 + hh * HEAD_DIM
      kind = _in_col_kind(col)
      t = acc[:, hh * HEAD_DIM:(hh + 1) * HEAD_DIM]
      if kind == "qc":
        qc_heads.append(t.astype(BF16))
        continue
      if kind in ("u", "v"):
        uv_ref[:, col:col + HEAD_DIM] = jax.nn.gelu(t)
        continue
      if kind in tables:
        tc, ts1, ts2 = tables[kind]
        t = (t * tc + pltpu.roll(t, ROPE_HALF, 1) * ts1
             + pltpu.roll(t, HEAD_DIM - ROPE_HALF, 1) * ts2)
      cs = slice(col - IN_QKV0, col - IN_QKV0 + HEAD_DIM)
      z_ref[:, cs] = t.astype(z_ref.dtype)
      sl = (chunk * heads_per_chunk + hh) % n_slabs
      slab[sl] = t
      for rho in range(D4):
        t4 = slab.at[sl][pl.ds(rho, tm // D4, stride=D4), :]
        c4_ref[rho, :, cs] = t4.astype(c4_ref.dtype)
        slab4[sl, rho] = t4
        for q in range(D16_PER_D4):
          t16 = slab4.at[sl, rho][pl.ds(q, tm // D16, stride=D16_PER_D4), :]
          c16_ref[D4 * q + rho, :, cs] = t16.astype(c16_ref.dtype)
    if chunk == n_chunks - 1:
      _mem_attention(qc_heads, kv_ref, go_ref.at[:, SGU_WIDTH + DIL_WIDTH:], cn_ref)
    if chunk == first_qkv - 1:
      _spatial_gating(uv_ref, gs_ref, wsp_ref, bsp_ref, go_ref.at[:, :SGU_WIDTH], a_ref)


N_IN_PROJ_INPUTS = 11
N_IN_PROJ_OUTPUTS = 5


def _in_proj_kernel_into(*refs):
  n_in, n_out = N_IN_PROJ_INPUTS, N_IN_PROJ_OUTPUTS
  _in_proj_kernel(*refs[:n_in], *refs[n_in + n_out:])


def _in_proj(x, g, w, layer, rope, kv, g_sgu, w_sp, b_sp, g_out, *, tm, tok, row0, rows_per_mem,
             first_sample_mem_tile, into=None):
  k = x.shape[1]
  tile0 = row0 // tm
  tiles_per_mem = rows_per_mem // tm

  def kv_map(i):
    mem_tile = (tile0 + i) // tiles_per_mem
    return (layer, jnp.maximum(mem_tile - first_sample_mem_tile + 1, 0), 0, 0)

  def pos_map(i):
    t = tile0 + i
    prompt_tiles = first_sample_mem_tile * tiles_per_mem
    return (jnp.where(t < prompt_tiles, t, (t - prompt_tiles) % tiles_per_mem), 0)

  const = lambda shape: pl.BlockSpec(shape, lambda i: (0,) * len(shape))
  in_specs = ([pl.BlockSpec((tm, k), lambda i: (i, 0)),
               const((1, k)),
               pl.BlockSpec((None, k, IN_WIDTH), lambda i: (layer, 0, 0),
                            pipeline_mode=pl.Buffered(1))]
              + [pl.BlockSpec((tm, HEAD_DIM), pos_map)] * 3
              + [pl.BlockSpec((None, None, N_MEM, 2 * MEM_WIDTH), kv_map),
                 const((1, SGU_WIDTH)),
                 pl.BlockSpec((None, N_SGU_GROUPS, SGU_CHUNK, SGU_CHUNK),
                              lambda i: (layer, 0, 0, 0)),
                 const((N_SGU_GROUPS, SGU_CHUNK, HEAD_DIM)),
                 const((1, MIX_WIDTH))])
  args = [x, g.reshape(1, k), w, *rope, kv, g_sgu, w_sp, b_sp, g_out]
  assert len(in_specs) == N_IN_PROJ_INPUTS
  body, aliases = _in_proj_kernel, {}
  if into is not None:
    body = _in_proj_kernel_into
    in_specs += [pl.BlockSpec(memory_space=pl.ANY)] * N_IN_PROJ_OUTPUTS
    aliases = {N_IN_PROJ_INPUTS + j: j for j in range(N_IN_PROJ_OUTPUTS)}
    args += list(into)
  return pl.pallas_call(
      body,
      grid=(x.shape[0] // tm,),
      in_specs=in_specs,
      out_specs=[pl.BlockSpec((tm, QKV_WIDTH), lambda i: (tile0 + i, 0)),
                 pl.BlockSpec((D4, tm // D4, QKV_WIDTH), lambda i: (0, tile0 + i, 0)),
                 pl.BlockSpec((D16, tm // D16, QKV_WIDTH), lambda i: (0, tile0 + i, 0)),
                 pl.BlockSpec((tm, SGU_WIDTH), lambda i: (tile0 + i, 0)),
                 pl.BlockSpec((tm, MEM_WIDTH), lambda i: (tile0 + i, 0))],
      out_shape=[jax.ShapeDtypeStruct((tok, QKV_WIDTH), BF16),
                 jax.ShapeDtypeStruct((D4, tok // D4, QKV_WIDTH), BF16),
                 jax.ShapeDtypeStruct((D16, tok // D16, QKV_WIDTH), BF16),
                 jax.ShapeDtypeStruct((tok, SGU_WIDTH), BF16),
                 jax.ShapeDtypeStruct((tok, MEM_WIDTH), BF16)],
      scratch_shapes=[pltpu.VMEM((tm, k), BF16),
                      pltpu.VMEM((IN_SLABS, tm, HEAD_DIM), F32),
                      pltpu.VMEM((IN_SLABS, D4, tm // D4, HEAD_DIM), F32),
                      pltpu.VMEM((tm, 2 * SGU_WIDTH), F32)],
      input_output_aliases=aliases,
      compiler_params=_params("parallel", vmem=BIG_TILE_VMEM_LIMIT),
      name="in_proj",
  )(*args)


DIL_SUB = 2 * HALF
DIL_KEYS = DIL_SUB + 2 * HALF
DIL_DEN_LANE0 = N_DIL_HEADS


def _dil_kernel(q_ref, kp_ref, kc_ref, kn_ref, vp_ref, vc_ref, vn_ref, o_ref, st_ref,
                kbuf, vbuf, *, class_len_prompt, class_len_sample, prompt_rows):
  cps, bq = q_ref.shape[0], q_ref.shape[1]
  row0 = pl.program_id(1) * bq
  in_prompt = row0 < prompt_rows
  clen = jnp.where(in_prompt, class_len_prompt, class_len_sample)
  pos = jnp.where(in_prompt, row0, row0 - prompt_rows) & (clen - 1)
  first = pos == 0
  last = pos + bq == clen

  for cls in range(cps):
    kbuf[cls, 0:HALF, :] = kp_ref[cls]
    kbuf[cls, HALF:HALF + bq, :] = kc_ref[cls]
    kbuf[cls, HALF + bq:, :] = kn_ref[cls]
    vbuf[cls, 0:HALF, :] = vp_ref[cls]
    vbuf[cls, HALF:HALF + bq, :] = vc_ref[cls]
    vbuf[cls, HALF + bq:, :] = vn_ref[cls]

  r = lax.broadcasted_iota(jnp.int32, (DIL_SUB, DIL_KEYS), 0)
  c = lax.broadcasted_iota(jnp.int32, (DIL_SUB, DIL_KEYS), 1)
  band = jnp.where((c >= r) & (c <= r + 2 * HALF), 0.0, NEG_INF).astype(F32)
  lane = lax.broadcasted_iota(jnp.int32, (DIL_SUB, HEAD_DIM), 1)
  lo = jnp.where(c < HALF, jnp.where(first, NEG_INF, 0.0), 0.0).astype(F32)
  hi = jnp.where(c >= DIL_KEYS - HALF, jnp.where(last, NEG_INF, 0.0), 0.0).astype(F32)
  nsub = bq // DIL_SUB
  biases = [band] * nsub
  biases[0] = biases[0] + lo
  biases[-1] = biases[-1] + hi

  for cls in range(cps):
    for j in range(nsub):
      rs = slice(j * DIL_SUB, (j + 1) * DIL_SUB)
      ks = slice(j * DIL_SUB, j * DIL_SUB + DIL_KEYS)
      stats = jnp.zeros((DIL_SUB, HEAD_DIM), F32)
      for h in range(N_DIL_HEADS):
        cs = slice(h * HEAD_DIM, (h + 1) * HEAD_DIM)
        s = lax.dot_general(q_ref[cls, rs, cs], kbuf[cls, ks, cs], (((1,), (1,)), ((), ())),
                            preferred_element_type=F32) + biases[j]
        m = jnp.max(s, axis=-1, keepdims=True)
        p = jnp.exp2(s - m)
        den = jnp.sum(p, axis=-1, keepdims=True)
        num = jnp.dot(p.astype(BF16), vbuf[cls, ks, cs], preferred_element_type=F32)
        o_ref[cls, rs, cs] = num.astype(o_ref.dtype)
        stats = jnp.where(lane == h, m, jnp.where(lane == DIL_DEN_LANE0 + h, den, stats))
      st_ref[cls, rs, :] = stats


def _dilated_branch(src, *, d, bq, cps, prompt_rows, s_prompt, s_sample):
  rows = src.shape[1]
  hb = bq // HALF
  n_half_blocks = rows // HALF
  body = functools.partial(
      _dil_kernel, class_len_prompt=s_prompt // d, class_len_sample=s_sample // d,
      prompt_rows=prompt_rows // d)

  def cur(c):
    return pl.BlockSpec((cps, bq, DIL_WIDTH), lambda r, i: (r, i, c))

  def prev(c):
    return pl.BlockSpec((cps, HALF, DIL_WIDTH), lambda r, i: (r, jnp.maximum(i * hb - 1, 0), c))

  def nxt(c):
    return pl.BlockSpec((cps, HALF, DIL_WIDTH),
                        lambda r, i: (r, jnp.minimum((i + 1) * hb, n_half_blocks - 1), c))

  return pl.pallas_call(
      body,
      grid=(d // cps, rows // bq),
      in_specs=[cur(0), prev(1), cur(1), nxt(1), prev(2), cur(2), nxt(2)],
      out_specs=[pl.BlockSpec((cps, bq, DIL_WIDTH), lambda r, i: (r, i, 0)),
                 pl.BlockSpec((cps, bq, HEAD_DIM), lambda r, i: (r, i, 0))],
      out_shape=[jax.ShapeDtypeStruct((d, rows, DIL_WIDTH), BF16),
                 jax.ShapeDtypeStruct((d, rows, HEAD_DIM), F32)],
      scratch_shapes=[pltpu.VMEM((cps, bq + 2 * HALF, DIL_WIDTH), BF16),
                      pltpu.VMEM((cps, bq + 2 * HALF, DIL_WIDTH), BF16)],
      compiler_params=_params("parallel", "parallel"),
      name=f"dilated_d{d}",
  )(*([src] * 7))


def _merge_tile(o1_ref, o4_ref, o16_ref, l1_ref, l4_ref, l16_ref, e_ref, gb_ref, b_ref,
                s4, s16, ls4, ls16, bs):
  tm = o1_ref.shape[0]
  for rho in range(D4):
    ls4[pl.ds(rho, tm // D4, stride=D4), :] = l4_ref[rho]
  for r in range(D16):
    ls16[pl.ds(r, tm // D16, stride=D16), :] = l16_ref[r]

  t1, t2, t3 = l1_ref[...], ls4[...], ls16[...]
  m = jnp.maximum(jnp.maximum(t1, t2), t3)
  w1, w2, w3 = jnp.exp2(t1 - m), jnp.exp2(t2 - m), jnp.exp2(t3 - m)
  den_of = lambda t: pltpu.roll(t, HEAD_DIM - DIL_DEN_LANE0, 1)
  total = w1 * den_of(t1) + w2 * den_of(t2) + w3 * den_of(t3)
  head_lane = lax.broadcasted_iota(jnp.int32, (tm, HEAD_DIM), 1) < N_DIL_HEADS
  inv = 1.0 / jnp.where(head_lane, total, 1.0)

  def over_head_lanes(w):
    wn = w * inv
    hi = wn.astype(BF16)
    lo = (wn - hi.astype(F32)).astype(BF16)
    return jnp.dot(jnp.concatenate([hi, lo], axis=1), e_ref[...], preferred_element_type=F32)

  wb1, wb2, wb3 = over_head_lanes(w1), over_head_lanes(w2), over_head_lanes(w3)

  ssq = jnp.zeros((tm, 1), F32)
  for h in range(N_DIL_HEADS):
    cs = slice(h * HEAD_DIM, (h + 1) * HEAD_DIM)
    for rho in range(D4):
      s4.at[h][pl.ds(rho, tm // D4, stride=D4), :] = o4_ref[rho, :, cs].astype(F32)
    for r in range(D16):
      s16.at[h][pl.ds(r, tm // D16, stride=D16), :] = o16_ref[r, :, cs].astype(F32)
    b = wb1[:, cs] * o1_ref[:, cs].astype(F32) + wb2[:, cs] * s4[h] + wb3[:, cs] * s16[h]
    ssq = ssq + jnp.sum(b * b, axis=-1, keepdims=True)
    bs[h] = b
  scale = jnp.broadcast_to(lax.rsqrt(ssq * (1.0 / DIL_WIDTH) + EPS), (tm, HEAD_DIM))
  for h in range(N_DIL_HEADS):
    cs = slice(h * HEAD_DIM, (h + 1) * HEAD_DIM)
    b_ref[:, cs] = (bs[h] * scale * gb_ref[:, cs]).astype(b_ref.dtype)


def _mix_out_kernel(o1_ref, o4_ref, o16_ref, l1_ref, l4_ref, l16_ref, e_ref, gb_ref,
                    a_ref, c_ref, w_ref, *rest, prompt_tiles):
  if prompt_tiles is None:
    x_ref, y_ref, s4, s16, ls4, ls16, bs, bn = rest
    xs_ref = None
  else:
    x_ref, xs_ref, y_ref, s4, s16, ls4, ls16, bs, bn = rest
  step = pl.program_id(0)
  b0 = SGU_WIDTH
  c0 = SGU_WIDTH + DIL_WIDTH

  def merge_into(slot):
    _merge_tile(o1_ref, o4_ref, o16_ref, l1_ref, l4_ref, l16_ref, e_ref, gb_ref, bn.at[slot],
                s4, s16, ls4, ls16, bs)

  @pl.when(step == 0)
  def _():
    merge_into(0)

  @pl.when(step > 0)
  def _():
    slot = step % 2
    merge_into(slot)
    b_prev = bn.at[1 - slot]
    for chunk in range(D_MODEL // OUT_CHUNK_N):
      cols = slice(chunk * OUT_CHUNK_N, (chunk + 1) * OUT_CHUNK_N)
      acc = jnp.dot(a_ref[...], w_ref[0:b0, cols], preferred_element_type=F32)
      acc += jnp.dot(b_prev[...], w_ref[b0:c0, cols], preferred_element_type=F32)
      acc += jnp.dot(c_ref[...], w_ref[c0:, cols], preferred_element_type=F32)
      res = x_ref[:, cols]
      if xs_ref is not None:
        res = jnp.where(step - 1 < prompt_tiles, res, xs_ref[:, cols])
      y_ref[:, cols] = res + acc


def _mix_out(o_list, lse_list, g_b, a_n, c_n, w_out, layer, x, *, tm):
  o1, o4, o16 = o_list
  l1, l4, l16 = lse_list
  tok = a_n.shape[0]
  n = tok // tm
  head_of_lane = jnp.arange(DIL_WIDTH, dtype=jnp.int32) // HEAD_DIM
  selector = (jnp.arange(HEAD_DIM, dtype=jnp.int32)[:, None] == head_of_lane[None, :]).astype(BF16)
  selector = jnp.concatenate([selector, selector], axis=0)
  head_scratch = pltpu.VMEM((N_DIL_HEADS, tm, HEAD_DIM), F32)
  ahead = lambda s: jnp.minimum(s, n - 1)
  behind = lambda s: jnp.maximum(s - 1, 0)
  const = lambda shape: pl.BlockSpec(shape, lambda s: (0,) * len(shape))
  row = lambda w: pl.BlockSpec((tm, w), lambda s: (behind(s), 0))
  if isinstance(x, tuple):
    prompt_tiles = x[0].shape[0] // tm
    x_specs = [pl.BlockSpec((tm, D_MODEL), lambda s: (jnp.minimum(behind(s), prompt_tiles - 1), 0)),
               pl.BlockSpec((tm, D_MODEL), lambda s: (jnp.maximum(behind(s) - prompt_tiles, 0), 0))]
  else:
    prompt_tiles, x_specs, x = None, [row(D_MODEL)], (x,)
  return pl.pallas_call(
      functools.partial(_mix_out_kernel, prompt_tiles=prompt_tiles),
      grid=(n + 1,),
      in_specs=[pl.BlockSpec((None, tm, DIL_WIDTH), lambda s: (0, ahead(s), 0)),
                pl.BlockSpec((D4, tm // D4, DIL_WIDTH), lambda s: (0, ahead(s), 0)),
                pl.BlockSpec((D16, tm // D16, DIL_WIDTH), lambda s: (0, ahead(s), 0)),
                pl.BlockSpec((None, tm, HEAD_DIM), lambda s: (0, ahead(s), 0)),
                pl.BlockSpec((D4, tm // D4, HEAD_DIM), lambda s: (0, ahead(s), 0)),
                pl.BlockSpec((D16, tm // D16, HEAD_DIM), lambda s: (0, ahead(s), 0)),
                const((2 * HEAD_DIM, DIL_WIDTH)),
                const((1, DIL_WIDTH)),
                row(SGU_WIDTH), row(MEM_WIDTH),
                pl.BlockSpec((None, MIX_WIDTH, D_MODEL), lambda s: (layer, 0, 0),
                             pipeline_mode=pl.Buffered(1))] + x_specs,
      out_specs=row(D_MODEL),
      out_shape=jax.ShapeDtypeStruct((tok, D_MODEL), F32),
      scratch_shapes=[head_scratch, head_scratch,
                      pltpu.VMEM((tm, HEAD_DIM), F32), pltpu.VMEM((tm, HEAD_DIM), F32),
                      head_scratch, pltpu.VMEM((2, tm, DIL_WIDTH), BF16)],
      compiler_params=_params("arbitrary"),
      name="mix_out",
  )(o1, o4, o16, l1, l4, l16, selector, g_b, a_n, c_n, w_out, *x)


def _ffn_kernel(x_ref, g_ref, wg_ref, wu_ref, wd_ref, gf_ref, y_ref, h2_ref, *, final_norm):
  i, k = pl.program_id(0), pl.program_id(1)
  last = pl.num_programs(1) - 1
  slot = i % 2

  def normalize_into(s):
    xf = x_ref[...]
    h2_ref[s] = (xf * _rms_scale(xf) * g_ref[...]).astype(BF16)

  def down_projection():
    h = h2_ref[slot]
    gate = jnp.dot(h, wg_ref[...], preferred_element_type=F32)
    up = jnp.dot(h, wu_ref[...], preferred_element_type=F32)
    act = (jax.nn.silu(gate) * up).astype(BF16)
    return jnp.dot(act, wd_ref[...], preferred_element_type=F32)

  @pl.when((i == 0) & (k == 0))
  def _():
    normalize_into(0)

  @pl.when(k == 0)
  def _():
    y_ref[...] = x_ref[...] + down_projection()

  @pl.when((k > 0) & (k < last))
  def _():
    y_ref[...] += down_projection()

  @pl.when(k == last)
  def _():
    normalize_into(1 - slot)
    y = y_ref[...] + down_projection()
    if final_norm:
      y = y * _rms_scale(y) * gf_ref[...]
    y_ref[...] = y


def _ffn(x, g, w_gu, w_down, layer, g_final, *, tm, th, final_norm, row0=0, rows=None):
  tok = x.shape[0] if rows is None else rows
  tile0 = row0 // tm
  n = tok // tm
  nk = FFN_HIDDEN // th
  assert nk >= 2

  def x_map(i, k):
    return (tile0 + jnp.minimum(i + (k == nk - 1).astype(jnp.int32), n - 1), 0)

  return pl.pallas_call(
      functools.partial(_ffn_kernel, final_norm=final_norm),
      grid=(n, nk),
      in_specs=[
          pl.BlockSpec((tm, D_MODEL), x_map),
          pl.BlockSpec((1, D_MODEL), lambda i, k: (0, 0)),
          pl.BlockSpec((None, D_MODEL, th), lambda i, k: (layer, 0, k)),
          pl.BlockSpec((None, D_MODEL, th), lambda i, k: (layer, 0, nk + k)),
          pl.BlockSpec((None, th, D_MODEL), lambda i, k: (layer, k, 0)),
          pl.BlockSpec((1, D_MODEL), lambda i, k: (0, 0)),
      ],
      out_specs=pl.BlockSpec((tm, D_MODEL), lambda i, k: (i, 0)),
      out_shape=jax.ShapeDtypeStruct((tok, D_MODEL), F32),
      scratch_shapes=[pltpu.VMEM((2, tm, D_MODEL), BF16)],
      compiler_params=_params("arbitrary", "arbitrary", vmem=BIG_TILE_VMEM_LIMIT),
      name="ffn",
  )(x, g, w_gu, w_gu, w_down, g_final)


def _rope_tables(positions):
  inv = ROPE_THETA ** (-jnp.arange(0, ROPE_DIM, 2, dtype=F32) / ROPE_DIM)
  ang = positions.astype(F32)[:, None] * inv[None, :]
  cos, sin = jnp.cos(ang), jnp.sin(ang)
  n = positions.shape[0]
  rest = HEAD_DIM - ROPE_DIM
  c = jnp.concatenate([cos, cos, jnp.ones((n, rest), F32)], axis=1)
  s1 = jnp.concatenate([jnp.zeros((n, ROPE_HALF), F32), sin, jnp.zeros((n, rest), F32)], axis=1)
  s2 = jnp.concatenate([-sin, jnp.zeros((n, HEAD_DIM - ROPE_HALF), F32)], axis=1)
  return c, s1, s2


def kernel(x_prompt, x_sample, mem_prompt, mem_sample, g_mix_norm, w_in, g_sgu, w_spatial,
           b_spatial, g_mem_norm, w_mem_kv, g_group_out, w_out, g_ffn_norm, w_gate_up,
           w_down, g_final):
  n_prompt, s_prompt, _ = x_prompt.shape
  n_sample, s_sample, _ = x_sample.shape
  assert n_prompt == 1 and s_prompt % s_sample == 0
  assert s_sample % (D16 * DIL_SUB) == 0 and s_sample % TM_FFN == 0
  prompt_rows = n_prompt * s_prompt
  depth = w_in.shape[0]

  x = (x_prompt.reshape(prompt_rows, D_MODEL), x_sample.reshape(n_sample * s_sample, D_MODEL))
  tok = prompt_rows + n_sample * s_sample
  mem = jnp.concatenate([mem_prompt, mem_sample], axis=0)
  n_mems = mem.shape[0]
  mem = mem.reshape(n_mems * N_MEM, D_MODEL)
  rope = _rope_tables(jnp.arange(s_prompt, dtype=jnp.int32))
  seqs = dict(prompt_rows=prompt_rows, s_prompt=s_prompt, s_sample=s_sample)
  w_in, w_spatial, w_mem_kv, w_out, w_gate_up, w_down = (
      w.astype(BF16) for w in (w_in, w_spatial, w_mem_kv, w_out, w_gate_up, w_down))

  kv = _norm_matmul_layers(mem, g_mem_norm, w_mem_kv, tm=n_mems * N_MEM, tn=2 * MEM_WIDTH,
                           name="mem_kv")
  kv = kv.reshape(depth, n_mems, N_MEM, 2 * MEM_WIDTH)

  for l in range(depth):
    g_out = g_group_out[l].reshape(1, MIX_WIDTH)
    b_sp = jnp.broadcast_to(b_spatial[l][:, :, None], (N_SGU_GROUPS, SGU_CHUNK, HEAD_DIM))
    in_proj = functools.partial(
        _in_proj, g=g_mix_norm[l], w=w_in, layer=l, rope=rope, kv=kv,
        g_sgu=g_sgu[l].reshape(1, SGU_WIDTH), w_sp=w_spatial, b_sp=b_sp, g_out=g_out,
        tm=TM_IN_PROJ, tok=tok, rows_per_mem=s_sample,
        first_sample_mem_tile=prompt_rows // s_sample)
    if isinstance(x, tuple):
      mixed = in_proj(x[0], row0=0)
      mixed = in_proj(x[1], row0=prompt_rows, into=mixed)
    else:
      mixed = in_proj(x, row0=0)
    zq, c4, c16, a_n, c_n = mixed
    o_list, lse_list = [], []
    for d, src in ((1, zq.reshape(1, *zq.shape)), (D4, c4), (D16, c16)):
      bq = min(DIL_ROWS_PER_STEP, s_sample // d)
      o, lse = _dilated_branch(src, d=d, bq=bq, cps=min(d, DIL_ROWS_PER_STEP // bq), **seqs)
      o_list.append(o)
      lse_list.append(lse)
    x = _mix_out(o_list, lse_list, g_out[:, SGU_WIDTH:SGU_WIDTH + DIL_WIDTH], a_n, c_n, w_out, l, x,
                 tm=TM_MIX_OUT)
    ffn = functools.partial(_ffn, x, g_ffn_norm[l].reshape(1, D_MODEL), w_gate_up, w_down, l,
                            g_final.reshape(1, D_MODEL), tm=TM_FFN, th=TH_FFN)
    if l < depth - 1:
      x = ffn(final_norm=False)

  y_prompt = ffn(final_norm=True, row0=0, rows=prompt_rows)
  y_sample = ffn(final_norm=True, row0=prompt_rows, rows=n_sample * s_sample)
  return (y_prompt.reshape(n_prompt, s_prompt, D_MODEL),
          y_sample.reshape(n_sample, s_sample, D_MODEL))
```

```python
import functools
import math

import jax
import jax.numpy as jnp
from jax import lax
from jax.experimental import pallas as pl
from jax.experimental.pallas import tpu as pltpu

F32 = jnp.float32
BF16 = jnp.bfloat16

D_MODEL = 2048
HEAD_DIM = 128
N_SGU_GROUPS = 4
SGU_WIDTH = N_SGU_GROUPS * HEAD_DIM
SGU_CHUNK = 128
N_DIL_HEADS = 8
DIL_WIDTH = N_DIL_HEADS * HEAD_DIM
DILATIONS = (1, 4, 16)
HALF = 64
N_MEM_HEADS = 4
MEM_WIDTH = N_MEM_HEADS * HEAD_DIM
N_MEM = 256
MIX_WIDTH = SGU_WIDTH + DIL_WIDTH + MEM_WIDTH
IN_WIDTH = 2 * SGU_WIDTH + 3 * DIL_WIDTH + MEM_WIDTH
ROPE_THETA = 500000.0
ROPE_DIM = HEAD_DIM // 4
ROPE_HALF = ROPE_DIM // 2
FFN_HIDDEN = 5632
EPS = 1e-6
NEG_INF = -1e30
ATTN_SCALE = HEAD_DIM ** -0.5
DIL_Q_SCALE = ATTN_SCALE * math.log2(math.e)

V7X_VMEM_BYTES = 64 * 1024 * 1024
VMEM_LIMIT = V7X_VMEM_BYTES - 8 * 1024 * 1024

IN_CHUNK_N = 256
IN_SLABS = 4
BIG_TILE_VMEM_LIMIT = V7X_VMEM_BYTES - 3 * 1024 * 1024
IN_V0 = SGU_WIDTH
IN_QKV0 = 2 * SGU_WIDTH
IN_QC0 = IN_QKV0 + 3 * DIL_WIDTH
QKV_WIDTH = 3 * DIL_WIDTH
D4, D16 = DILATIONS[1], DILATIONS[2]
D16_PER_D4 = D16 // D4

TM_IN_PROJ = 512
TM_MIX_OUT = 512
OUT_CHUNK_N = 256
TM_FFN = 1024
TH_FFN = 512
DIL_ROWS_PER_STEP = 1024


def _params(*sem, vmem=VMEM_LIMIT):
  return pltpu.CompilerParams(dimension_semantics=sem, vmem_limit_bytes=vmem)


def _rms_scale(x):
  return lax.rsqrt(jnp.mean(x * x, axis=-1, keepdims=True) + EPS)


def _norm_matmul_kernel(x_ref, g_ref, w_ref, o_ref, h_ref):
  @pl.when(pl.program_id(2) == 0)
  def _():
    xf = x_ref[...]
    h_ref[...] = (xf * _rms_scale(xf) * g_ref[...]).astype(BF16)

  o_ref[...] = jnp.dot(h_ref[...], w_ref[...], preferred_element_type=F32).astype(o_ref.dtype)


def _norm_matmul_layers(x, g, w, *, tm, tn, name):
  m, k = x.shape
  depth, _, n = w.shape
  return pl.pallas_call(
      _norm_matmul_kernel,
      grid=(depth, m // tm, n // tn),
      in_specs=[pl.BlockSpec((tm, k), lambda l, i, j: (i, 0)),
                pl.BlockSpec((None, 1, k), lambda l, i, j: (l, 0, 0)),
                pl.BlockSpec((None, k, tn), lambda l, i, j: (l, 0, j))],
      out_specs=pl.BlockSpec((None, tm, tn), lambda l, i, j: (l, i, j)),
      out_shape=jax.ShapeDtypeStruct((depth, m, n), BF16),
      scratch_shapes=[pltpu.VMEM((tm, k), BF16)],
      compiler_params=_params("parallel", "parallel", "arbitrary"),
      name=name,
  )(x, g.reshape(depth, 1, k), w)


def _in_col_kind(col):
  bounds = ((IN_V0, "u"), (IN_QKV0, "v"), (IN_QKV0 + DIL_WIDTH, "q"),
            (IN_QKV0 + 2 * DIL_WIDTH, "k"), (IN_QC0, "vb"))
  for end, kind in bounds:
    if col < end:
      return kind
  return "qc"


def _mem_attention(q_heads, kv_ref, go_ref, o_ref):
  outs = []
  ssq = jnp.zeros((q_heads[0].shape[0], 1), F32)
  for h, q in enumerate(q_heads):
    k = kv_ref[:, h * HEAD_DIM:(h + 1) * HEAD_DIM]
    v = kv_ref[:, MEM_WIDTH + h * HEAD_DIM:MEM_WIDTH + (h + 1) * HEAD_DIM]
    s = lax.dot_general(q, k, (((1,), (1,)), ((), ())), preferred_element_type=F32) * ATTN_SCALE
    m = jnp.max(s, axis=-1, keepdims=True)
    p = jnp.exp(s - m)
    den = jnp.sum(p, axis=-1, keepdims=True)
    o = jnp.dot(p.astype(BF16), v, preferred_element_type=F32) / den
    ssq = ssq + jnp.sum(o * o, axis=-1, keepdims=True)
    outs.append(o)
  scale = lax.rsqrt(ssq * (1.0 / MEM_WIDTH) + EPS)
  for h in range(N_MEM_HEADS):
    cs = slice(h * HEAD_DIM, (h + 1) * HEAD_DIM)
    o_ref[:, cs] = (outs[h] * scale * go_ref[:, cs]).astype(o_ref.dtype)


def _spatial_gating(uv_ref, gs_ref, w_ref, b_ref, go_ref, o_ref):
  for c in range(uv_ref.shape[0] // SGU_CHUNK):
    rs = slice(c * SGU_CHUNK, (c + 1) * SGU_CHUNK)
    outs = []
    ssq = jnp.zeros((SGU_CHUNK, 1), F32)
    for g in range(N_SGU_GROUPS):
      cs = slice(g * HEAD_DIM, (g + 1) * HEAD_DIM)
      v = uv_ref[rs, IN_V0 + g * HEAD_DIM:IN_V0 + (g + 1) * HEAD_DIM]
      vv = v * _rms_scale(v) * gs_ref[:, cs]
      vs = jnp.dot(w_ref[g], vv.astype(BF16), preferred_element_type=F32) + b_ref[g]
      a = uv_ref[rs, cs] * vs
      ssq = ssq + jnp.sum(a * a, axis=-1, keepdims=True)
      outs.append(a)
    scale = lax.rsqrt(ssq * (1.0 / SGU_WIDTH) + EPS)
    for g in range(N_SGU_GROUPS):
      cs = slice(g * HEAD_DIM, (g + 1) * HEAD_DIM)
      o_ref[rs, cs] = (outs[g] * scale * go_ref[:, cs]).astype(o_ref.dtype)


def _in_proj_kernel(x_ref, g_ref, w_ref, c_ref, s1_ref, s2_ref, kv_ref, gs_ref, wsp_ref, bsp_ref,
                    go_ref, z_ref, c4_ref, c16_ref, a_ref, cn_ref, h_ref, slab, slab4, uv_ref):
  _in_proj_rows(x_ref[...], g_ref, w_ref, c_ref, s1_ref, s2_ref, kv_ref, gs_ref, wsp_ref, bsp_ref,
                go_ref, z_ref, c4_ref, c16_ref, a_ref, cn_ref, h_ref, slab, slab4, uv_ref)


def _in_proj_kernel_two_sources(xp_ref, xs_ref, g_ref, w_ref, c_ref, s1_ref, s2_ref, kv_ref, gs_ref,
                                wsp_ref, bsp_ref, go_ref, z_ref, c4_ref, c16_ref, a_ref, cn_ref,
                                h_ref, slab, slab4, uv_ref, xbuf, sem, *, prompt_tiles):
  i, n = pl.program_id(0), pl.num_programs(0)
  tm = xbuf.shape[1]
  slot = i % 2

  def copy(src_ref, src_tile, dst_slot):
    return pltpu.make_async_copy(src_ref.at[pl.ds(src_tile * tm, tm), :], xbuf.at[dst_slot],
                                 sem.at[dst_slot])

  def for_tile(tile, dst_slot, action):
    @pl.when(tile < prompt_tiles)
    def _():
      action(copy(xp_ref, tile, dst_slot))

    @pl.when(tile >= prompt_tiles)
    def _():
      action(copy(xs_ref, tile - prompt_tiles, dst_slot))

  @pl.when(i == 0)
  def _():
    for_tile(i, slot, lambda c: c.start())

  @pl.when(i + 1 < n)
  def _():
    for_tile(i + 1, 1 - slot, lambda c: c.start())

  for_tile(i, slot, lambda c: c.wait())
  _in_proj_rows(xbuf[slot], g_ref, w_ref, c_ref, s1_ref, s2_ref, kv_ref, gs_ref, wsp_ref, bsp_ref,
                go_ref, z_ref, c4_ref, c16_ref, a_ref, cn_ref, h_ref, slab, slab4, uv_ref)


def _in_proj_rows(xf, g_ref, w_ref, c_ref, s1_ref, s2_ref, kv_ref, gs_ref, wsp_ref, bsp_ref,
                  go_ref, z_ref, c4_ref, c16_ref, a_ref, cn_ref, h_ref, slab, slab4, uv_ref):
  tm = xf.shape[0]
  h_ref[...] = (xf * _rms_scale(xf) * g_ref[...]).astype(BF16)
  c, s1, s2 = c_ref[...], s1_ref[...], s2_ref[...]
  tables = {"k": (c, s1, s2), "q": (c * DIL_Q_SCALE, s1 * DIL_Q_SCALE, s2 * DIL_Q_SCALE)}
  n_slabs = slab.shape[0]
  heads_per_chunk = IN_CHUNK_N // HEAD_DIM
  n_chunks = IN_WIDTH // IN_CHUNK_N
  first_qc, first_qkv = IN_QC0 // IN_CHUNK_N, IN_QKV0 // IN_CHUNK_N
  order = (list(range(first_qc, n_chunks)) + list(range(first_qkv))
           + list(range(first_qkv, first_qc)))
  qc_heads = []
  for chunk in order:
    col0 = chunk * IN_CHUNK_N
    acc = jnp.dot(h_ref[...], w_ref[:, col0:col0 + IN_CHUNK_N], preferred_element_type=F32)
    for hh in range(heads_per_chunk):
      col = col0 + hh * HEAD_DIM
      kind = _in_col_kind(col)
      t = acc[:, hh * HEAD_DIM:(hh + 1) * HEAD_DIM]
      if kind == "qc":
        qc_heads.append(t.astype(BF16))
        continue
      if kind in ("u", "v"):
        uv_ref[:, col:col + HEAD_DIM] = jax.nn.gelu(t)
        continue
      if kind in tables:
        tc, ts1, ts2 = tables[kind]
        t = (t * tc + pltpu.roll(t, ROPE_HALF, 1) * ts1
             + pltpu.roll(t, HEAD_DIM - ROPE_HALF, 1) * ts2)
      cs = slice(col - IN_QKV0, col - IN_QKV0 + HEAD_DIM)
      z_ref[:, cs] = t.astype(z_ref.dtype)
      sl = (chunk * heads_per_chunk + hh) % n_slabs
      slab[sl] = t
      for rho in range(D4):
        t4 = slab.at[sl][pl.ds(rho, tm // D4, stride=D4), :]
        c4_ref[rho, :, cs] = t4.astype(c4_ref.dtype)
        slab4[sl, rho] = t4
        for q in range(D16_PER_D4):
          t16 = slab4.at[sl, rho][pl.ds(q, tm // D16, stride=D16_PER_D4), :]
          c16_ref[D4 * q + rho, :, cs] = t16.astype(c16_ref.dtype)
    if chunk == n_chunks - 1:
      _mem_attention(qc_heads, kv_ref, go_ref.at[:, SGU_WIDTH + DIL_WIDTH:], cn_ref)
    if chunk == first_qkv - 1:
      _spatial_gating(uv_ref, gs_ref, wsp_ref, bsp_ref, go_ref.at[:, :SGU_WIDTH], a_ref)


def _in_proj(x, g, w, layer, rope, kv, g_sgu, w_sp, b_sp, g_out, *, tm, rows_per_mem,
             first_sample_mem_tile):
  two_sources = isinstance(x, tuple)
  xs = x if two_sources else (x,)
  tok = sum(a.shape[0] for a in xs)
  k = xs[0].shape[1]
  tiles_per_mem = rows_per_mem // tm
  prompt_tiles = first_sample_mem_tile * tiles_per_mem

  def kv_map(i):
    return (layer, jnp.maximum(i // tiles_per_mem - first_sample_mem_tile + 1, 0), 0, 0)

  def pos_map(i):
    return (jnp.where(i < prompt_tiles, i, (i - prompt_tiles) % tiles_per_mem), 0)

  const = lambda shape: pl.BlockSpec(shape, lambda i: (0,) * len(shape))
  scratch = [pltpu.VMEM((tm, k), BF16),
             pltpu.VMEM((IN_SLABS, tm, HEAD_DIM), F32),
             pltpu.VMEM((IN_SLABS, D4, tm // D4, HEAD_DIM), F32),
             pltpu.VMEM((tm, 2 * SGU_WIDTH), F32)]
  if two_sources:
    assert xs[0].shape[0] == prompt_tiles * tm
    body = functools.partial(_in_proj_kernel_two_sources, prompt_tiles=prompt_tiles)
    x_specs = [pl.BlockSpec(memory_space=pl.ANY)] * 2
    scratch += [pltpu.VMEM((2, tm, k), F32), pltpu.SemaphoreType.DMA((2,))]
    semantics = "arbitrary"
  else:
    body = _in_proj_kernel
    x_specs = [pl.BlockSpec((tm, k), lambda i: (i, 0))]
    semantics = "parallel"
  return pl.pallas_call(
      body,
      grid=(tok // tm,),
      in_specs=x_specs + [
          const((1, k)),
          pl.BlockSpec((None, k, IN_WIDTH), lambda i: (layer, 0, 0), pipeline_mode=pl.Buffered(1)),
          pl.BlockSpec((tm, HEAD_DIM), pos_map),
          pl.BlockSpec((tm, HEAD_DIM), pos_map),
          pl.BlockSpec((tm, HEAD_DIM), pos_map),
          pl.BlockSpec((None, None, N_MEM, 2 * MEM_WIDTH), kv_map),
          const((1, SGU_WIDTH)),
          pl.BlockSpec((None, N_SGU_GROUPS, SGU_CHUNK, SGU_CHUNK), lambda i: (layer, 0, 0, 0)),
          const((N_SGU_GROUPS, SGU_CHUNK, HEAD_DIM)),
          const((1, MIX_WIDTH))],
      out_specs=[pl.BlockSpec((tm, QKV_WIDTH), lambda i: (i, 0)),
                 pl.BlockSpec((D4, tm // D4, QKV_WIDTH), lambda i: (0, i, 0)),
                 pl.BlockSpec((D16, tm // D16, QKV_WIDTH), lambda i: (0, i, 0)),
                 pl.BlockSpec((tm, SGU_WIDTH), lambda i: (i, 0)),
                 pl.BlockSpec((tm, MEM_WIDTH), lambda i: (i, 0))],
      out_shape=[jax.ShapeDtypeStruct((tok, QKV_WIDTH), BF16),
                 jax.ShapeDtypeStruct((D4, tok // D4, QKV_WIDTH), BF16),
                 jax.ShapeDtypeStruct((D16, tok // D16, QKV_WIDTH), BF16),
                 jax.ShapeDtypeStruct((tok, SGU_WIDTH), BF16),
                 jax.ShapeDtypeStruct((tok, MEM_WIDTH), BF16)],
      scratch_shapes=scratch,
      compiler_params=_params(semantics, vmem=BIG_TILE_VMEM_LIMIT),
      name="in_proj",
  )(*xs, g.reshape(1, k), w, *rope, kv, g_sgu, w_sp, b_sp, g_out)


DIL_SUB = 2 * HALF
DIL_KEYS = DIL_SUB + 2 * HALF
DIL_DEN_LANE0 = N_DIL_HEADS


def _dil_kernel(q_ref, kp_ref, kc_ref, kn_ref, vp_ref, vc_ref, vn_ref, o_ref, st_ref,
                kbuf, vbuf, *, class_len_prompt, class_len_sample, prompt_rows):
  cps, bq = q_ref.shape[0], q_ref.shape[1]
  row0 = pl.program_id(1) * bq
  in_prompt = row0 < prompt_rows
  clen = jnp.where(in_prompt, class_len_prompt, class_len_sample)
  pos = jnp.where(in_prompt, row0, row0 - prompt_rows) & (clen - 1)
  first = pos == 0
  last = pos + bq == clen

  for cls in range(cps):
    kbuf[cls, 0:HALF, :] = kp_ref[cls]
    kbuf[cls, HALF:HALF + bq, :] = kc_ref[cls]
    kbuf[cls, HALF + bq:, :] = kn_ref[cls]
    vbuf[cls, 0:HALF, :] = vp_ref[cls]
    vbuf[cls, HALF:HALF + bq, :] = vc_ref[cls]
    vbuf[cls, HALF + bq:, :] = vn_ref[cls]

  r = lax.broadcasted_iota(jnp.int32, (DIL_SUB, DIL_KEYS), 0)
  c = lax.broadcasted_iota(jnp.int32, (DIL_SUB, DIL_KEYS), 1)
  band = jnp.where((c >= r) & (c <= r + 2 * HALF), 0.0, NEG_INF).astype(F32)
  lane = lax.broadcasted_iota(jnp.int32, (DIL_SUB, HEAD_DIM), 1)
  lo = jnp.where(c < HALF, jnp.where(first, NEG_INF, 0.0), 0.0).astype(F32)
  hi = jnp.where(c >= DIL_KEYS - HALF, jnp.where(last, NEG_INF, 0.0), 0.0).astype(F32)
  nsub = bq // DIL_SUB
  biases = [band] * nsub
  biases[0] = biases[0] + lo
  biases[-1] = biases[-1] + hi

  for cls in range(cps):
    for j in range(nsub):
      rs = slice(j * DIL_SUB, (j + 1) * DIL_SUB)
      ks = slice(j * DIL_SUB, j * DIL_SUB + DIL_KEYS)
      stats = jnp.zeros((DIL_SUB, HEAD_DIM), F32)
      for h in range(N_DIL_HEADS):
        cs = slice(h * HEAD_DIM, (h + 1) * HEAD_DIM)
        s = lax.dot_general(q_ref[cls, rs, cs], kbuf[cls, ks, cs], (((1,), (1,)), ((), ())),
                            preferred_element_type=F32) + biases[j]
        m = jnp.max(s, axis=-1, keepdims=True)
        p = jnp.exp2(s - m)
        den = jnp.sum(p, axis=-1, keepdims=True)
        num = jnp.dot(p.astype(BF16), vbuf[cls, ks, cs], preferred_element_type=F32)
        o_ref[cls, rs, cs] = num.astype(o_ref.dtype)
        stats = jnp.where(lane == h, m, jnp.where(lane == DIL_DEN_LANE0 + h, den, stats))
      st_ref[cls, rs, :] = stats


def _dilated_branch(src, *, d, bq, cps, prompt_rows, s_prompt, s_sample):
  rows = src.shape[1]
  hb = bq // HALF
  n_half_blocks = rows // HALF
  body = functools.partial(
      _dil_kernel, class_len_prompt=s_prompt // d, class_len_sample=s_sample // d,
      prompt_rows=prompt_rows // d)

  def cur(c):
    return pl.BlockSpec((cps, bq, DIL_WIDTH), lambda r, i: (r, i, c))

  def prev(c):
    return pl.BlockSpec((cps, HALF, DIL_WIDTH), lambda r, i: (r, jnp.maximum(i * hb - 1, 0), c))

  def nxt(c):
    return pl.BlockSpec((cps, HALF, DIL_WIDTH),
                        lambda r, i: (r, jnp.minimum((i + 1) * hb, n_half_blocks - 1), c))

  return pl.pallas_call(
      body,
      grid=(d // cps, rows // bq),
      in_specs=[cur(0), prev(1), cur(1), nxt(1), prev(2), cur(2), nxt(2)],
      out_specs=[pl.BlockSpec((cps, bq, DIL_WIDTH), lambda r, i: (r, i, 0)),
                 pl.BlockSpec((cps, bq, HEAD_DIM), lambda r, i: (r, i, 0))],
      out_shape=[jax.ShapeDtypeStruct((d, rows, DIL_WIDTH), BF16),
                 jax.ShapeDtypeStruct((d, rows, HEAD_DIM), F32)],
      scratch_shapes=[pltpu.VMEM((cps, bq + 2 * HALF, DIL_WIDTH), BF16),
                      pltpu.VMEM((cps, bq + 2 * HALF, DIL_WIDTH), BF16)],
      compiler_params=_params("parallel", "parallel"),
      name=f"dilated_d{d}",
  )(*([src] * 7))


def _merge_tile(o1_ref, o4_ref, o16_ref, l1_ref, l4_ref, l16_ref, e_ref, gb_ref, b_ref,
                s4, s16, ls4, ls16, bs):
  tm = o1_ref.shape[0]
  for rho in range(D4):
    ls4[pl.ds(rho, tm // D4, stride=D4), :] = l4_ref[rho]
  for r in range(D16):
    ls16[pl.ds(r, tm // D16, stride=D16), :] = l16_ref[r]

  t1, t2, t3 = l1_ref[...], ls4[...], ls16[...]
  m = jnp.maximum(jnp.maximum(t1, t2), t3)
  w1, w2, w3 = jnp.exp2(t1 - m), jnp.exp2(t2 - m), jnp.exp2(t3 - m)
  den_of = lambda t: pltpu.roll(t, HEAD_DIM - DIL_DEN_LANE0, 1)
  total = w1 * den_of(t1) + w2 * den_of(t2) + w3 * den_of(t3)
  head_lane = lax.broadcasted_iota(jnp.int32, (tm, HEAD_DIM), 1) < N_DIL_HEADS
  inv = 1.0 / jnp.where(head_lane, total, 1.0)

  def over_head_lanes(w):
    wn = w * inv
    hi = wn.astype(BF16)
    lo = (wn - hi.astype(F32)).astype(BF16)
    return jnp.dot(jnp.concatenate([hi, lo], axis=1), e_ref[...], preferred_element_type=F32)

  wb1, wb2, wb3 = over_head_lanes(w1), over_head_lanes(w2), over_head_lanes(w3)

  ssq = jnp.zeros((tm, 1), F32)
  for h in range(N_DIL_HEADS):
    cs = slice(h * HEAD_DIM, (h + 1) * HEAD_DIM)
    for rho in range(D4):
      s4.at[h][pl.ds(rho, tm // D4, stride=D4), :] = o4_ref[rho, :, cs].astype(F32)
    for r in range(D16):
      s16.at[h][pl.ds(r, tm // D16, stride=D16), :] = o16_ref[r, :, cs].astype(F32)
    b = wb1[:, cs] * o1_ref[:, cs].astype(F32) + wb2[:, cs] * s4[h] + wb3[:, cs] * s16[h]
    ssq = ssq + jnp.sum(b * b, axis=-1, keepdims=True)
    bs[h] = b
  scale = jnp.broadcast_to(lax.rsqrt(ssq * (1.0 / DIL_WIDTH) + EPS), (tm, HEAD_DIM))
  for h in range(N_DIL_HEADS):
    cs = slice(h * HEAD_DIM, (h + 1) * HEAD_DIM)
    b_ref[:, cs] = (bs[h] * scale * gb_ref[:, cs]).astype(b_ref.dtype)


def _mix_out_kernel(o1_ref, o4_ref, o16_ref, l1_ref, l4_ref, l16_ref, e_ref, gb_ref,
                    a_ref, c_ref, w_ref, *rest, prompt_tiles):
  if prompt_tiles is None:
    x_ref, y_ref, s4, s16, ls4, ls16, bs, bn = rest
    xs_ref = None
  else:
    x_ref, xs_ref, y_ref, s4, s16, ls4, ls16, bs, bn = rest
  step = pl.program_id(0)
  b0 = SGU_WIDTH
  c0 = SGU_WIDTH + DIL_WIDTH

  def merge_into(slot):
    _merge_tile(o1_ref, o4_ref, o16_ref, l1_ref, l4_ref, l16_ref, e_ref, gb_ref, bn.at[slot],
                s4, s16, ls4, ls16, bs)

  @pl.when(step == 0)
  def _():
    merge_into(0)

  @pl.when(step > 0)
  def _():
    slot = step % 2
    merge_into(slot)
    b_prev = bn.at[1 - slot]
    for chunk in range(D_MODEL // OUT_CHUNK_N):
      cols = slice(chunk * OUT_CHUNK_N, (chunk + 1) * OUT_CHUNK_N)
      acc = jnp.dot(a_ref[...], w_ref[0:b0, cols], preferred_element_type=F32)
      acc += jnp.dot(b_prev[...], w_ref[b0:c0, cols], preferred_element_type=F32)
      acc += jnp.dot(c_ref[...], w_ref[c0:, cols], preferred_element_type=F32)
      res = x_ref[:, cols]
      if xs_ref is not None:
        res = jnp.where(step - 1 < prompt_tiles, res, xs_ref[:, cols])
      y_ref[:, cols] = res + acc


def _mix_out(o_list, lse_list, g_b, a_n, c_n, w_out, layer, x, *, tm):
  o1, o4, o16 = o_list
  l1, l4, l16 = lse_list
  tok = a_n.shape[0]
  n = tok // tm
  head_of_lane = jnp.arange(DIL_WIDTH, dtype=jnp.int32) // HEAD_DIM
  selector = (jnp.arange(HEAD_DIM, dtype=jnp.int32)[:, None] == head_of_lane[None, :]).astype(BF16)
  selector = jnp.concatenate([selector, selector], axis=0)
  head_scratch = pltpu.VMEM((N_DIL_HEADS, tm, HEAD_DIM), F32)
  ahead = lambda s: jnp.minimum(s, n - 1)
  behind = lambda s: jnp.maximum(s - 1, 0)
  const = lambda shape: pl.BlockSpec(shape, lambda s: (0,) * len(shape))
  row = lambda w: pl.BlockSpec((tm, w), lambda s: (behind(s), 0))
  if isinstance(x, tuple):
    prompt_tiles = x[0].shape[0] // tm
    x_specs = [pl.BlockSpec((tm, D_MODEL), lambda s: (jnp.minimum(behind(s), prompt_tiles - 1), 0)),
               pl.BlockSpec((tm, D_MODEL), lambda s: (jnp.maximum(behind(s) - prompt_tiles, 0), 0))]
  else:
    prompt_tiles, x_specs, x = None, [row(D_MODEL)], (x,)
  return pl.pallas_call(
      functools.partial(_mix_out_kernel, prompt_tiles=prompt_tiles),
      grid=(n + 1,),
      in_specs=[pl.BlockSpec((None, tm, DIL_WIDTH), lambda s: (0, ahead(s), 0)),
                pl.BlockSpec((D4, tm // D4, DIL_WIDTH), lambda s: (0, ahead(s), 0)),
                pl.BlockSpec((D16, tm // D16, DIL_WIDTH), lambda s: (0, ahead(s), 0)),
                pl.BlockSpec((None, tm, HEAD_DIM), lambda s: (0, ahead(s), 0)),
                pl.BlockSpec((D4, tm // D4, HEAD_DIM), lambda s: (0, ahead(s), 0)),
                pl.BlockSpec((D16, tm // D16, HEAD_DIM), lambda s: (0, ahead(s), 0)),
                const((2 * HEAD_DIM, DIL_WIDTH)),
                const((1, DIL_WIDTH)),
                row(SGU_WIDTH), row(MEM_WIDTH),
                pl.BlockSpec((None, MIX_WIDTH, D_MODEL), lambda s: (layer, 0, 0),
                             pipeline_mode=pl.Buffered(1))] + x_specs,
      out_specs=row(D_MODEL),
      out_shape=jax.ShapeDtypeStruct((tok, D_MODEL), F32),
      scratch_shapes=[head_scratch, head_scratch,
                      pltpu.VMEM((tm, HEAD_DIM), F32), pltpu.VMEM((tm, HEAD_DIM), F32),
                      head_scratch, pltpu.VMEM((2, tm, DIL_WIDTH), BF16)],
      compiler_params=_params("arbitrary"),
      name="mix_out",
  )(o1, o4, o16, l1, l4, l16, selector, g_b, a_n, c_n, w_out, *x)


def _ffn_kernel(x_ref, g_ref, wg_ref, wu_ref, wd_ref, gf_ref, y_ref, h2_ref, *, final_norm):
  i, k = pl.program_id(0), pl.program_id(1)
  last = pl.num_programs(1) - 1
  slot = i % 2

  def normalize_into(s):
    xf = x_ref[...]
    h2_ref[s] = (xf * _rms_scale(xf) * g_ref[...]).astype(BF16)

  def down_projection():
    h = h2_ref[slot]
    gate = jnp.dot(h, wg_ref[...], preferred_element_type=F32)
    up = jnp.dot(h, wu_ref[...], preferred_element_type=F32)
    act = (jax.nn.silu(gate) * up).astype(BF16)
    return jnp.dot(act, wd_ref[...], preferred_element_type=F32)

  @pl.when((i == 0) & (k == 0))
  def _():
    normalize_into(0)

  @pl.when(k == 0)
  def _():
    y_ref[...] = x_ref[...] + down_projection()

  @pl.when((k > 0) & (k < last))
  def _():
    y_ref[...] += down_projection()

  @pl.when(k == last)
  def _():
    normalize_into(1 - slot)
    y = y_ref[...] + down_projection()
    if final_norm:
      y = y * _rms_scale(y) * gf_ref[...]
    y_ref[...] = y


def _ffn(x, g, w_gu, w_down, layer, g_final, *, tm, th, final_norm, row0=0, rows=None):
  tok = x.shape[0] if rows is None else rows
  tile0 = row0 // tm
  n = tok // tm
  nk = FFN_HIDDEN // th
  assert nk >= 2

  def x_map(i, k):
    return (tile0 + jnp.minimum(i + (k == nk - 1).astype(jnp.int32), n - 1), 0)

  return pl.pallas_call(
      functools.partial(_ffn_kernel, final_norm=final_norm),
      grid=(n, nk),
      in_specs=[
          pl.BlockSpec((tm, D_MODEL), x_map),
          pl.BlockSpec((1, D_MODEL), lambda i, k: (0, 0)),
          pl.BlockSpec((None, D_MODEL, th), lambda i, k: (layer, 0, k)),
          pl.BlockSpec((None, D_MODEL, th), lambda i, k: (layer, 0, nk + k)),
          pl.BlockSpec((None, th, D_MODEL), lambda i, k: (layer, k, 0)),
          pl.BlockSpec((1, D_MODEL), lambda i, k: (0, 0)),
      ],
      out_specs=pl.BlockSpec((tm, D_MODEL), lambda i, k: (i, 0)),
      out_shape=jax.ShapeDtypeStruct((tok, D_MODEL), F32),
      scratch_shapes=[pltpu.VMEM((2, tm, D_MODEL), BF16)],
      compiler_params=_params("arbitrary", "arbitrary", vmem=BIG_TILE_VMEM_LIMIT),
      name="ffn",
  )(x, g, w_gu, w_gu, w_down, g_final)


def _rope_tables(positions):
  inv = ROPE_THETA ** (-jnp.arange(0, ROPE_DIM, 2, dtype=F32) / ROPE_DIM)
  ang = positions.astype(F32)[:, None] * inv[None, :]
  cos, sin = jnp.cos(ang), jnp.sin(ang)
  n = positions.shape[0]
  rest = HEAD_DIM - ROPE_DIM
  c = jnp.concatenate([cos, cos, jnp.ones((n, rest), F32)], axis=1)
  s1 = jnp.concatenate([jnp.zeros((n, ROPE_HALF), F32), sin, jnp.zeros((n, rest), F32)], axis=1)
  s2 = jnp.concatenate([-sin, jnp.zeros((n, HEAD_DIM - ROPE_HALF), F32)], axis=1)
  return c, s1, s2


def kernel(x_prompt, x_sample, mem_prompt, mem_sample, g_mix_norm, w_in, g_sgu, w_spatial,
           b_spatial, g_mem_norm, w_mem_kv, g_group_out, w_out, g_ffn_norm, w_gate_up,
           w_down, g_final):
  n_prompt, s_prompt, _ = x_prompt.shape
  n_sample, s_sample, _ = x_sample.shape
  assert n_prompt == 1 and s_prompt % s_sample == 0
  assert s_sample % (D16 * DIL_SUB) == 0 and s_sample % TM_FFN == 0
  prompt_rows = n_prompt * s_prompt
  depth = w_in.shape[0]

  x = (x_prompt.reshape(prompt_rows, D_MODEL), x_sample.reshape(n_sample * s_sample, D_MODEL))
  mem = jnp.concatenate([mem_prompt, mem_sample], axis=0)
  n_mems = mem.shape[0]
  mem = mem.reshape(n_mems * N_MEM, D_MODEL)
  rope = _rope_tables(jnp.arange(s_prompt, dtype=jnp.int32))
  seqs = dict(prompt_rows=prompt_rows, s_prompt=s_prompt, s_sample=s_sample)
  w_in, w_spatial, w_mem_kv, w_out, w_gate_up, w_down = (
      w.astype(BF16) for w in (w_in, w_spatial, w_mem_kv, w_out, w_gate_up, w_down))

  kv = _norm_matmul_layers(mem, g_mem_norm, w_mem_kv, tm=n_mems * N_MEM, tn=2 * MEM_WIDTH,
                           name="mem_kv")
  kv = kv.reshape(depth, n_mems, N_MEM, 2 * MEM_WIDTH)

  for l in range(depth):
    g_out = g_group_out[l].reshape(1, MIX_WIDTH)
    b_sp = jnp.broadcast_to(b_spatial[l][:, :, None], (N_SGU_GROUPS, SGU_CHUNK, HEAD_DIM))
    zq, c4, c16, a_n, c_n = _in_proj(
        x, g_mix_norm[l], w_in, l, rope, kv, g_sgu[l].reshape(1, SGU_WIDTH), w_spatial, b_sp,
        g_out, tm=TM_IN_PROJ, rows_per_mem=s_sample,
        first_sample_mem_tile=prompt_rows // s_sample)
    o_list, lse_list = [], []
    for d, src in ((1, zq.reshape(1, *zq.shape)), (D4, c4), (D16, c16)):
      bq = min(DIL_ROWS_PER_STEP, s_sample // d)
      o, lse = _dilated_branch(src, d=d, bq=bq, cps=min(d, DIL_ROWS_PER_STEP // bq), **seqs)
      o_list.append(o)
      lse_list.append(lse)
    x = _mix_out(o_list, lse_list, g_out[:, SGU_WIDTH:SGU_WIDTH + DIL_WIDTH], a_n, c_n, w_out, l, x,
                 tm=TM_MIX_OUT)
    ffn = functools.partial(_ffn, x, g_ffn_norm[l].reshape(1, D_MODEL), w_gate_up, w_down, l,
                            g_final.reshape(1, D_MODEL), tm=TM_FFN, th=TH_FFN)
    if l < depth - 1:
      x = ffn(final_norm=False)

  y_prompt = ffn(final_norm=True, row0=0, rows=prompt_rows)
  y_sample = ffn(final_norm=True, row0=prompt_rows, rows=n_sample * s_sample)
  return (y_prompt.reshape(n_prompt, s_prompt, D_MODEL),
          y_sample.reshape(n_sample, s_sample, D_MODEL))
```

```python
import functools
import math

import jax
import jax.numpy as jnp
from jax import lax
from jax.experimental import pallas as pl
from jax.experimental.pallas import tpu as pltpu

F32 = jnp.float32
BF16 = jnp.bfloat16

D_MODEL = 2048
HEAD_DIM = 128
N_SGU_GROUPS = 4
SGU_WIDTH = N_SGU_GROUPS * HEAD_DIM
SGU_CHUNK = 128
N_DIL_HEADS = 8
DIL_WIDTH = N_DIL_HEADS * HEAD_DIM
DILATIONS = (1, 4, 16)
HALF = 64
N_MEM_HEADS = 4
MEM_WIDTH = N_MEM_HEADS * HEAD_DIM
N_MEM = 256
MIX_WIDTH = SGU_WIDTH + DIL_WIDTH + MEM_WIDTH
IN_WIDTH = 2 * SGU_WIDTH + 3 * DIL_WIDTH + MEM_WIDTH
ROPE_THETA = 500000.0
ROPE_DIM = HEAD_DIM // 4
ROPE_HALF = ROPE_DIM // 2
FFN_HIDDEN = 5632
EPS = 1e-6
NEG_INF = -1e30
ATTN_SCALE = HEAD_DIM ** -0.5
DIL_Q_SCALE = ATTN_SCALE * math.log2(math.e)

V7X_VMEM_BYTES = 64 * 1024 * 1024
VMEM_LIMIT = V7X_VMEM_BYTES - 8 * 1024 * 1024

IN_CHUNK_N = 256
IN_SLABS = 4
BIG_TILE_VMEM_LIMIT = V7X_VMEM_BYTES - 3 * 1024 * 1024
IN_V0 = SGU_WIDTH
IN_QKV0 = 2 * SGU_WIDTH
IN_QC0 = IN_QKV0 + 3 * DIL_WIDTH
QKV_WIDTH = 3 * DIL_WIDTH
D4, D16 = DILATIONS[1], DILATIONS[2]
D16_PER_D4 = D16 // D4

TM_IN_PROJ = 512
TM_MIX_OUT = 512
OUT_CHUNK_N = 256
TM_FFN = 1024
TH_FFN = 512
DIL_ROWS_PER_STEP = 2048


def _params(*sem, vmem=VMEM_LIMIT):
  return pltpu.CompilerParams(dimension_semantics=sem, vmem_limit_bytes=vmem)


def _rms_scale(x):
  return lax.rsqrt(jnp.mean(x * x, axis=-1, keepdims=True) + EPS)


def _norm_matmul_kernel(x_ref, g_ref, w_ref, o_ref, h_ref):
  @pl.when(pl.program_id(2) == 0)
  def _():
    xf = x_ref[...]
    h_ref[...] = (xf * _rms_scale(xf) * g_ref[...]).astype(BF16)

  o_ref[...] = jnp.dot(h_ref[...], w_ref[...], preferred_element_type=F32).astype(o_ref.dtype)


def _norm_matmul_layers(x, g, w, *, tm, tn, name):
  m, k = x.shape
  depth, _, n = w.shape
  return pl.pallas_call(
      _norm_matmul_kernel,
      grid=(depth, m // tm, n // tn),
      in_specs=[pl.BlockSpec((tm, k), lambda l, i, j: (i, 0)),
                pl.BlockSpec((None, 1, k), lambda l, i, j: (l, 0, 0)),
                pl.BlockSpec((None, k, tn), lambda l, i, j: (l, 0, j))],
      out_specs=pl.BlockSpec((None, tm, tn), lambda l, i, j: (l, i, j)),
      out_shape=jax.ShapeDtypeStruct((depth, m, n), BF16),
      scratch_shapes=[pltpu.VMEM((tm, k), BF16)],
      compiler_params=_params("parallel", "parallel", "arbitrary"),
      name=name,
  )(x, g.reshape(depth, 1, k), w)


def _in_col_kind(col):
  bounds = ((IN_V0, "u"), (IN_QKV0, "v"), (IN_QKV0 + DIL_WIDTH, "q"),
            (IN_QKV0 + 2 * DIL_WIDTH, "k"), (IN_QC0, "vb"))
  for end, kind in bounds:
    if col < end:
      return kind
  return "qc"


def _mem_attention(q_heads, kv_ref, go_ref, o_ref):
  outs = []
  ssq = jnp.zeros((q_heads[0].shape[0], 1), F32)
  for h, q in enumerate(q_heads):
    k = kv_ref[:, h * HEAD_DIM:(h + 1) * HEAD_DIM]
    v = kv_ref[:, MEM_WIDTH + h * HEAD_DIM:MEM_WIDTH + (h + 1) * HEAD_DIM]
    s = lax.dot_general(q, k, (((1,), (1,)), ((), ())), preferred_element_type=F32) * ATTN_SCALE
    m = jnp.max(s, axis=-1, keepdims=True)
    p = jnp.exp(s - m)
    den = jnp.sum(p, axis=-1, keepdims=True)
    o = jnp.dot(p.astype(BF16), v, preferred_element_type=F32) / den
    ssq = ssq + jnp.sum(o * o, axis=-1, keepdims=True)
    outs.append(o)
  scale = lax.rsqrt(ssq * (1.0 / MEM_WIDTH) + EPS)
  for h in range(N_MEM_HEADS):
    cs = slice(h * HEAD_DIM, (h + 1) * HEAD_DIM)
    o_ref[:, cs] = (outs[h] * scale * go_ref[:, cs]).astype(o_ref.dtype)


def _spatial_gating(uv_ref, gs_ref, w_ref, b_ref, go_ref, o_ref):
  for c in range(uv_ref.shape[0] // SGU_CHUNK):
    rs = slice(c * SGU_CHUNK, (c + 1) * SGU_CHUNK)
    outs = []
    ssq = jnp.zeros((SGU_CHUNK, 1), F32)
    for g in range(N_SGU_GROUPS):
      cs = slice(g * HEAD_DIM, (g + 1) * HEAD_DIM)
      v = uv_ref[rs, IN_V0 + g * HEAD_DIM:IN_V0 + (g + 1) * HEAD_DIM]
      vv = v * _rms_scale(v) * gs_ref[:, cs]
      vs = jnp.dot(w_ref[g], vv.astype(BF16), preferred_element_type=F32) + b_ref[g]
      a = uv_ref[rs, cs] * vs
      ssq = ssq + jnp.sum(a * a, axis=-1, keepdims=True)
      outs.append(a)
    scale = lax.rsqrt(ssq * (1.0 / SGU_WIDTH) + EPS)
    for g in range(N_SGU_GROUPS):
      cs = slice(g * HEAD_DIM, (g + 1) * HEAD_DIM)
      o_ref[rs, cs] = (outs[g] * scale * go_ref[:, cs]).astype(o_ref.dtype)


def _in_proj_kernel(x_ref, g_ref, w_ref, c_ref, s1_ref, s2_ref, kv_ref, gs_ref, wsp_ref, bsp_ref,
                    go_ref, z_ref, c4_ref, c16_ref, a_ref, cn_ref, h_ref, slab, slab4, uv_ref):
  _in_proj_rows(x_ref[...], g_ref, w_ref, c_ref, s1_ref, s2_ref, kv_ref, gs_ref, wsp_ref, bsp_ref,
                go_ref, z_ref, c4_ref, c16_ref, a_ref, cn_ref, h_ref, slab, slab4, uv_ref)


def _in_proj_kernel_two_sources(xp_ref, xs_ref, g_ref, w_ref, c_ref, s1_ref, s2_ref, kv_ref, gs_ref,
                                wsp_ref, bsp_ref, go_ref, z_ref, c4_ref, c16_ref, a_ref, cn_ref,
                                h_ref, slab, slab4, uv_ref, xbuf, sem, *, prompt_tiles):
  i, n = pl.program_id(0), pl.num_programs(0)
  tm = xbuf.shape[1]
  slot = i % 2

  def copy(src_ref, src_tile, dst_slot):
    return pltpu.make_async_copy(src_ref.at[pl.ds(src_tile * tm, tm), :], xbuf.at[dst_slot],
                                 sem.at[dst_slot])

  def for_tile(tile, dst_slot, action):
    @pl.when(tile < prompt_tiles)
    def _():
      action(copy(xp_ref, tile, dst_slot))

    @pl.when(tile >= prompt_tiles)
    def _():
      action(copy(xs_ref, tile - prompt_tiles, dst_slot))

  @pl.when(i == 0)
  def _():
    for_tile(i, slot, lambda c: c.start())

  @pl.when(i + 1 < n)
  def _():
    for_tile(i + 1, 1 - slot, lambda c: c.start())

  for_tile(i, slot, lambda c: c.wait())
  _in_proj_rows(xbuf[slot], g_ref, w_ref, c_ref, s1_ref, s2_ref, kv_ref, gs_ref, wsp_ref, bsp_ref,
                go_ref, z_ref, c4_ref, c16_ref, a_ref, cn_ref, h_ref, slab, slab4, uv_ref)


def _in_proj_rows(xf, g_ref, w_ref, c_ref, s1_ref, s2_ref, kv_ref, gs_ref, wsp_ref, bsp_ref,
                  go_ref, z_ref, c4_ref, c16_ref, a_ref, cn_ref, h_ref, slab, slab4, uv_ref):
  tm = xf.shape[0]
  h_ref[...] = (xf * _rms_scale(xf) * g_ref[...]).astype(BF16)
  c, s1, s2 = c_ref[...], s1_ref[...], s2_ref[...]
  tables = {"k": (c, s1, s2), "q": (c * DIL_Q_SCALE, s1 * DIL_Q_SCALE, s2 * DIL_Q_SCALE)}
  n_slabs = slab.shape[0]
  heads_per_chunk = IN_CHUNK_N // HEAD_DIM
  n_chunks = IN_WIDTH // IN_CHUNK_N
  first_qc, first_qkv = IN_QC0 // IN_CHUNK_N, IN_QKV0 // IN_CHUNK_N
  order = (list(range(first_qc, n_chunks)) + list(range(first_qkv))
           + list(range(first_qkv, first_qc)))
  qc_heads = []
  for chunk in order:
    col0 = chunk * IN_CHUNK_N
    acc = jnp.dot(h_ref[...], w_ref[:, col0:col0 + IN_CHUNK_N], preferred_element_type=F32)
    for hh in range(heads_per_chunk):
      col = col0 + hh * HEAD_DIM
      kind = _in_col_kind(col)
      t = acc[:, hh * HEAD_DIM:(hh + 1) * HEAD_DIM]
      if kind == "qc":
        qc_heads.append(t.astype(BF16))
        continue
      if kind in ("u", "v"):
        uv_ref[:, col:col + HEAD_DIM] = jax.nn.gelu(t)
        continue
      if kind in tables:
        tc, ts1, ts2 = tables[kind]
        t = (t * tc + pltpu.roll(t, ROPE_HALF, 1) * ts1
             + pltpu.roll(t, HEAD_DIM - ROPE_HALF, 1) * ts2)
      cs = slice(col - IN_QKV0, col - IN_QKV0 + HEAD_DIM)
      z_ref[:, cs] = t.astype(z_ref.dtype)
      sl = (chunk * heads_per_chunk + hh) % n_slabs
      slab[sl] = t
      for rho in range(D4):
        t4 = slab.at[sl][pl.ds(rho, tm // D4, stride=D4), :]
        c4_ref[rho, :, cs] = t4.astype(c4_ref.dtype)
        slab4[sl, rho] = t4
        for q in range(D16_PER_D4):
          t16 = slab4.at[sl, rho][pl.ds(q, tm // D16, stride=D16_PER_D4), :]
          c16_ref[D4 * q + rho, :, cs] = t16.astype(c16_ref.dtype)
    if chunk == n_chunks - 1:
      _mem_attention(qc_heads, kv_ref, go_ref.at[:, SGU_WIDTH + DIL_WIDTH:], cn_ref)
    if chunk == first_qkv - 1:
      _spatial_gating(uv_ref, gs_ref, wsp_ref, bsp_ref, go_ref.at[:, :SGU_WIDTH], a_ref)


def _in_proj(x, g, w, layer, rope, kv, g_sgu, w_sp, b_sp, g_out, *, tm, rows_per_mem,
             first_sample_mem_tile):
  two_sources = isinstance(x, tuple)
  xs = x if two_sources else (x,)
  tok = sum(a.shape[0] for a in xs)
  k = xs[0].shape[1]
  tiles_per_mem = rows_per_mem // tm
  prompt_tiles = first_sample_mem_tile * tiles_per_mem

  def kv_map(i):
    return (layer, jnp.maximum(i // tiles_per_mem - first_sample_mem_tile + 1, 0), 0, 0)

  def pos_map(i):
    return (jnp.where(i < prompt_tiles, i, (i - prompt_tiles) % tiles_per_mem), 0)

  const = lambda shape: pl.BlockSpec(shape, lambda i: (0,) * len(shape))
  scratch = [pltpu.VMEM((tm, k), BF16),
             pltpu.VMEM((IN_SLABS, tm, HEAD_DIM), F32),
             pltpu.VMEM((IN_SLABS, D4, tm // D4, HEAD_DIM), F32),
             pltpu.VMEM((tm, 2 * SGU_WIDTH), F32)]
  if two_sources:
    assert xs[0].shape[0] == prompt_tiles * tm
    body = functools.partial(_in_proj_kernel_two_sources, prompt_tiles=prompt_tiles)
    x_specs = [pl.BlockSpec(memory_space=pl.ANY)] * 2
    scratch += [pltpu.VMEM((2, tm, k), F32), pltpu.SemaphoreType.DMA((2,))]
    semantics = "arbitrary"
  else:
    body = _in_proj_kernel
    x_specs = [pl.BlockSpec((tm, k), lambda i: (i, 0))]
    semantics = "parallel"
  return pl.pallas_call(
      body,
      grid=(tok // tm,),
      in_specs=x_specs + [
          const((1, k)),
          pl.BlockSpec((None, k, IN_WIDTH), lambda i: (layer, 0, 0), pipeline_mode=pl.Buffered(1)),
          pl.BlockSpec((tm, HEAD_DIM), pos_map),
          pl.BlockSpec((tm, HEAD_DIM), pos_map),
          pl.BlockSpec((tm, HEAD_DIM), pos_map),
          pl.BlockSpec((None, None, N_MEM, 2 * MEM_WIDTH), kv_map),
          const((1, SGU_WIDTH)),
          pl.BlockSpec((None, N_SGU_GROUPS, SGU_CHUNK, SGU_CHUNK), lambda i: (layer, 0, 0, 0)),
          const((N_SGU_GROUPS, SGU_CHUNK, HEAD_DIM)),
          const((1, MIX_WIDTH))],
      out_specs=[pl.BlockSpec((tm, QKV_WIDTH), lambda i: (i, 0)),
                 pl.BlockSpec((D4, tm // D4, QKV_WIDTH), lambda i: (0, i, 0)),
                 pl.BlockSpec((D16, tm // D16, QKV_WIDTH), lambda i: (0, i, 0)),
                 pl.BlockSpec((tm, SGU_WIDTH), lambda i: (i, 0)),
                 pl.BlockSpec((tm, MEM_WIDTH), lambda i: (i, 0))],
      out_shape=[jax.ShapeDtypeStruct((tok, QKV_WIDTH), BF16),
                 jax.ShapeDtypeStruct((D4, tok // D4, QKV_WIDTH), BF16),
                 jax.ShapeDtypeStruct((D16, tok // D16, QKV_WIDTH), BF16),
                 jax.ShapeDtypeStruct((tok, SGU_WIDTH), BF16),
                 jax.ShapeDtypeStruct((tok, MEM_WIDTH), BF16)],
      scratch_shapes=scratch,
      compiler_params=_params(semantics, vmem=BIG_TILE_VMEM_LIMIT),
      name="in_proj",
  )(*xs, g.reshape(1, k), w, *rope, kv, g_sgu, w_sp, b_sp, g_out)


DIL_SUB = 2 * HALF
DIL_KEYS = DIL_SUB + 2 * HALF
DIL_DEN_LANE0 = N_DIL_HEADS


def _dil_kernel(q_ref, kp_ref, kc_ref, kn_ref, vp_ref, vc_ref, vn_ref, o_ref, st_ref,
                kbuf, vbuf, *, class_len_prompt, class_len_sample, prompt_rows):
  cps, bq = q_ref.shape[0], q_ref.shape[1]
  row0 = pl.program_id(1) * bq
  in_prompt = row0 < prompt_rows
  clen = jnp.where(in_prompt, class_len_prompt, class_len_sample)
  pos = jnp.where(in_prompt, row0, row0 - prompt_rows) & (clen - 1)
  first = pos == 0
  last = pos + bq == clen

  for cls in range(cps):
    kbuf[cls, 0:HALF, :] = kp_ref[cls]
    kbuf[cls, HALF:HALF + bq, :] = kc_ref[cls]
    kbuf[cls, HALF + bq:, :] = kn_ref[cls]
    vbuf[cls, 0:HALF, :] = vp_ref[cls]
    vbuf[cls, HALF:HALF + bq, :] = vc_ref[cls]
    vbuf[cls, HALF + bq:, :] = vn_ref[cls]

  r = lax.broadcasted_iota(jnp.int32, (DIL_SUB, DIL_KEYS), 0)
  c = lax.broadcasted_iota(jnp.int32, (DIL_SUB, DIL_KEYS), 1)
  band = jnp.where((c >= r) & (c <= r + 2 * HALF), 0.0, NEG_INF).astype(F32)
  lane = lax.broadcasted_iota(jnp.int32, (DIL_SUB, HEAD_DIM), 1)
  lo = jnp.where(c < HALF, jnp.where(first, NEG_INF, 0.0), 0.0).astype(F32)
  hi = jnp.where(c >= DIL_KEYS - HALF, jnp.where(last, NEG_INF, 0.0), 0.0).astype(F32)
  nsub = bq // DIL_SUB
  biases = [band] * nsub
  biases[0] = biases[0] + lo
  biases[-1] = biases[-1] + hi

  for cls in range(cps):
    for j in range(nsub):
      rs = slice(j * DIL_SUB, (j + 1) * DIL_SUB)
      ks = slice(j * DIL_SUB, j * DIL_SUB + DIL_KEYS)
      stats = jnp.zeros((DIL_SUB, HEAD_DIM), F32)
      for h in range(N_DIL_HEADS):
        cs = slice(h * HEAD_DIM, (h + 1) * HEAD_DIM)
        s = lax.dot_general(q_ref[cls, rs, cs], kbuf[cls, ks, cs], (((1,), (1,)), ((), ())),
                            preferred_element_type=F32) + biases[j]
        m = jnp.max(s, axis=-1, keepdims=True)
        p = jnp.exp2(s - m)
        den = jnp.sum(p, axis=-1, keepdims=True)
        num = jnp.dot(p.astype(BF16), vbuf[cls, ks, cs], preferred_element_type=F32)
        o_ref[cls, rs, cs] = num.astype(o_ref.dtype)
        stats = jnp.where(lane == h, m, jnp.where(lane == DIL_DEN_LANE0 + h, den, stats))
      st_ref[cls, rs, :] = stats


def _dilated_branch(src, *, d, bq, cps, prompt_rows, s_prompt, s_sample):
  rows = src.shape[1]
  hb = bq // HALF
  n_half_blocks = rows // HALF
  body = functools.partial(
      _dil_kernel, class_len_prompt=s_prompt // d, class_len_sample=s_sample // d,
      prompt_rows=prompt_rows // d)

  def cur(c):
    return pl.BlockSpec((cps, bq, DIL_WIDTH), lambda r, i: (r, i, c))

  def prev(c):
    return pl.BlockSpec((cps, HALF, DIL_WIDTH), lambda r, i: (r, jnp.maximum(i * hb - 1, 0), c))

  def nxt(c):
    return pl.BlockSpec((cps, HALF, DIL_WIDTH),
                        lambda r, i: (r, jnp.minimum((i + 1) * hb, n_half_blocks - 1), c))

  return pl.pallas_call(
      body,
      grid=(d // cps, rows // bq),
      in_specs=[cur(0), prev(1), cur(1), nxt(1), prev(2), cur(2), nxt(2)],
      out_specs=[pl.BlockSpec((cps, bq, DIL_WIDTH), lambda r, i: (r, i, 0)),
                 pl.BlockSpec((cps, bq, HEAD_DIM), lambda r, i: (r, i, 0))],
      out_shape=[jax.ShapeDtypeStruct((d, rows, DIL_WIDTH), BF16),
                 jax.ShapeDtypeStruct((d, rows, HEAD_DIM), F32)],
      scratch_shapes=[pltpu.VMEM((cps, bq + 2 * HALF, DIL_WIDTH), BF16),
                      pltpu.VMEM((cps, bq + 2 * HALF, DIL_WIDTH), BF16)],
      compiler_params=_params("parallel", "parallel"),
      name=f"dilated_d{d}",
  )(*([src] * 7))


def _merge_tile(o1_ref, o4_ref, o16_ref, l1_ref, l4_ref, l16_ref, e_ref, gb_ref, b_ref,
                s4, s16, ls4, ls16, bs):
  tm = o1_ref.shape[0]
  for rho in range(D4):
    ls4[pl.ds(rho, tm // D4, stride=D4), :] = l4_ref[rho]
  for r in range(D16):
    ls16[pl.ds(r, tm // D16, stride=D16), :] = l16_ref[r]

  t1, t2, t3 = l1_ref[...], ls4[...], ls16[...]
  m = jnp.maximum(jnp.maximum(t1, t2), t3)
  w1, w2, w3 = jnp.exp2(t1 - m), jnp.exp2(t2 - m), jnp.exp2(t3 - m)
  den_of = lambda t: pltpu.roll(t, HEAD_DIM - DIL_DEN_LANE0, 1)
  total = w1 * den_of(t1) + w2 * den_of(t2) + w3 * den_of(t3)
  head_lane = lax.broadcasted_iota(jnp.int32, (tm, HEAD_DIM), 1) < N_DIL_HEADS
  inv = 1.0 / jnp.where(head_lane, total, 1.0)

  def over_head_lanes(w):
    wn = w * inv
    hi = wn.astype(BF16)
    lo = (wn - hi.astype(F32)).astype(BF16)
    return jnp.dot(jnp.concatenate([hi, lo], axis=1), e_ref[...], preferred_element_type=F32)

  wb1, wb2, wb3 = over_head_lanes(w1), over_head_lanes(w2), over_head_lanes(w3)

  ssq = jnp.zeros((tm, 1), F32)
  for h in range(N_DIL_HEADS):
    cs = slice(h * HEAD_DIM, (h + 1) * HEAD_DIM)
    for rho in range(D4):
      s4.at[h][pl.ds(rho, tm // D4, stride=D4), :] = o4_ref[rho, :, cs].astype(F32)
    for r in range(D16):
      s16.at[h][pl.ds(r, tm // D16, stride=D16), :] = o16_ref[r, :, cs].astype(F32)
    b = wb1[:, cs] * o1_ref[:, cs].astype(F32) + wb2[:, cs] * s4[h] + wb3[:, cs] * s16[h]
    ssq = ssq + jnp.sum(b * b, axis=-1, keepdims=True)
    bs[h] = b
  scale = jnp.broadcast_to(lax.rsqrt(ssq * (1.0 / DIL_WIDTH) + EPS), (tm, HEAD_DIM))
  for h in range(N_DIL_HEADS):
    cs = slice(h * HEAD_DIM, (h + 1) * HEAD_DIM)
    b_ref[:, cs] = (bs[h] * scale * gb_ref[:, cs]).astype(b_ref.dtype)


def _mix_out_kernel(o1_ref, o4_ref, o16_ref, l1_ref, l4_ref, l16_ref, e_ref, gb_ref,
                    a_ref, c_ref, w_ref, *rest, prompt_tiles):
  if prompt_tiles is None:
    x_ref, y_ref, s4, s16, ls4, ls16, bs, bn = rest
    xs_ref = None
  else:
    x_ref, xs_ref, y_ref, s4, s16, ls4, ls16, bs, bn = rest
  step = pl.program_id(0)
  b0 = SGU_WIDTH
  c0 = SGU_WIDTH + DIL_WIDTH

  def merge_into(slot):
    _merge_tile(o1_ref, o4_ref, o16_ref, l1_ref, l4_ref, l16_ref, e_ref, gb_ref, bn.at[slot],
                s4, s16, ls4, ls16, bs)

  @pl.when(step == 0)
  def _():
    merge_into(0)

  @pl.when(step > 0)
  def _():
    slot = step % 2
    merge_into(slot)
    b_prev = bn.at[1 - slot]
    for chunk in range(D_MODEL // OUT_CHUNK_N):
      cols = slice(chunk * OUT_CHUNK_N, (chunk + 1) * OUT_CHUNK_N)
      acc = jnp.dot(a_ref[...], w_ref[0:b0, cols], preferred_element_type=F32)
      acc += jnp.dot(b_prev[...], w_ref[b0:c0, cols], preferred_element_type=F32)
      acc += jnp.dot(c_ref[...], w_ref[c0:, cols], preferred_element_type=F32)
      res = x_ref[:, cols]
      if xs_ref is not None:
        res = jnp.where(step - 1 < prompt_tiles, res, xs_ref[:, cols])
      y_ref[:, cols] = res + acc


def _mix_out(o_list, lse_list, g_b, a_n, c_n, w_out, layer, x, *, tm):
  o1, o4, o16 = o_list
  l1, l4, l16 = lse_list
  tok = a_n.shape[0]
  n = tok // tm
  head_of_lane = jnp.arange(DIL_WIDTH, dtype=jnp.int32) // HEAD_DIM
  selector = (jnp.arange(HEAD_DIM, dtype=jnp.int32)[:, None] == head_of_lane[None, :]).astype(BF16)
  selector = jnp.concatenate([selector, selector], axis=0)
  head_scratch = pltpu.VMEM((N_DIL_HEADS, tm, HEAD_DIM), F32)
  ahead = lambda s: jnp.minimum(s, n - 1)
  behind = lambda s: jnp.maximum(s - 1, 0)
  const = lambda shape: pl.BlockSpec(shape, lambda s: (0,) * len(shape))
  row = lambda w: pl.BlockSpec((tm, w), lambda s: (behind(s), 0))
  if isinstance(x, tuple):
    prompt_tiles = x[0].shape[0] // tm
    x_specs = [pl.BlockSpec((tm, D_MODEL), lambda s: (jnp.minimum(behind(s), prompt_tiles - 1), 0)),
               pl.BlockSpec((tm, D_MODEL), lambda s: (jnp.maximum(behind(s) - prompt_tiles, 0), 0))]
  else:
    prompt_tiles, x_specs, x = None, [row(D_MODEL)], (x,)
  return pl.pallas_call(
      functools.partial(_mix_out_kernel, prompt_tiles=prompt_tiles),
      grid=(n + 1,),
      in_specs=[pl.BlockSpec((None, tm, DIL_WIDTH), lambda s: (0, ahead(s), 0)),
                pl.BlockSpec((D4, tm // D4, DIL_WIDTH), lambda s: (0, ahead(s), 0)),
                pl.BlockSpec((D16, tm // D16, DIL_WIDTH), lambda s: (0, ahead(s), 0)),
                pl.BlockSpec((None, tm, HEAD_DIM), lambda s: (0, ahead(s), 0)),
                pl.BlockSpec((D4, tm // D4, HEAD_DIM), lambda s: (0, ahead(s), 0)),
                pl.BlockSpec((D16, tm // D16, HEAD_DIM), lambda s: (0, ahead(s), 0)),
                const((2 * HEAD_DIM, DIL_WIDTH)),
                const((1, DIL_WIDTH)),
                row(SGU_WIDTH), row(MEM_WIDTH),
                pl.BlockSpec((None, MIX_WIDTH, D_MODEL), lambda s: (layer, 0, 0),
                             pipeline_mode=pl.Buffered(1))] + x_specs,
      out_specs=row(D_MODEL),
      out_shape=jax.ShapeDtypeStruct((tok, D_MODEL), F32),
      scratch_shapes=[head_scratch, head_scratch,
                      pltpu.VMEM((tm, HEAD_DIM), F32), pltpu.VMEM((tm, HEAD_DIM), F32),
                      head_scratch, pltpu.VMEM((2, tm, DIL_WIDTH), BF16)],
      compiler_params=_params("arbitrary"),
      name="mix_out",
  )(o1, o4, o16, l1, l4, l16, selector, g_b, a_n, c_n, w_out, *x)


def _ffn_kernel(x_ref, g_ref, wg_ref, wu_ref, wd_ref, gf_ref, y_ref, h2_ref, *, final_norm):
  i, k = pl.program_id(0), pl.program_id(1)
  last = pl.num_programs(1) - 1
  slot = i % 2

  def normalize_into(s):
    xf = x_ref[...]
    h2_ref[s] = (xf * _rms_scale(xf) * g_ref[...]).astype(BF16)

  def down_projection():
    h = h2_ref[slot]
    gate = jnp.dot(h, wg_ref[...], preferred_element_type=F32)
    up = jnp.dot(h, wu_ref[...], preferred_element_type=F32)
    act = (jax.nn.silu(gate) * up).astype(BF16)
    return jnp.dot(act, wd_ref[...], preferred_element_type=F32)

  @pl.when((i == 0) & (k == 0))
  def _():
    normalize_into(0)

  @pl.when(k == 0)
  def _():
    y_ref[...] = x_ref[...] + down_projection()

  @pl.when((k > 0) & (k < last))
  def _():
    y_ref[...] += down_projection()

  @pl.when(k == last)
  def _():
    normalize_into(1 - slot)
    y = y_ref[...] + down_projection()
    if final_norm:
      y = y * _rms_scale(y) * gf_ref[...]
    y_ref[...] = y


def _ffn(x, g, w_gu, w_down, layer, g_final, *, tm, th, final_norm, row0=0, rows=None):
  tok = x.shape[0] if rows is None else rows
  tile0 = row0 // tm
  n = tok // tm
  nk = FFN_HIDDEN // th
  assert nk >= 2

  def x_map(i, k):
    return (tile0 + jnp.minimum(i + (k == nk - 1).astype(jnp.int32), n - 1), 0)

  return pl.pallas_call(
      functools.partial(_ffn_kernel, final_norm=final_norm),
      grid=(n, nk),
      in_specs=[
          pl.BlockSpec((tm, D_MODEL), x_map),
          pl.BlockSpec((1, D_MODEL), lambda i, k: (0, 0)),
          pl.BlockSpec((None, D_MODEL, th), lambda i, k: (layer, 0, k)),
          pl.BlockSpec((None, D_MODEL, th), lambda i, k: (layer, 0, nk + k)),
          pl.BlockSpec((None, th, D_MODEL), lambda i, k: (layer, k, 0)),
          pl.BlockSpec((1, D_MODEL), lambda i, k: (0, 0)),
      ],
      out_specs=pl.BlockSpec((tm, D_MODEL), lambda i, k: (i, 0)),
      out_shape=jax.ShapeDtypeStruct((tok, D_MODEL), F32),
      scratch_shapes=[pltpu.VMEM((2, tm, D_MODEL), BF16)],
      compiler_params=_params("arbitrary", "arbitrary", vmem=BIG_TILE_VMEM_LIMIT),
      name="ffn",
  )(x, g, w_gu, w_gu, w_down, g_final)


def _rope_tables(positions):
  inv = ROPE_THETA ** (-jnp.arange(0, ROPE_DIM, 2, dtype=F32) / ROPE_DIM)
  ang = positions.astype(F32)[:, None] * inv[None, :]
  cos, sin = jnp.cos(ang), jnp.sin(ang)
  n = positions.shape[0]
  rest = HEAD_DIM - ROPE_DIM
  c = jnp.concatenate([cos, cos, jnp.ones((n, rest), F32)], axis=1)
  s1 = jnp.concatenate([jnp.zeros((n, ROPE_HALF), F32), sin, jnp.zeros((n, rest), F32)], axis=1)
  s2 = jnp.concatenate([-sin, jnp.zeros((n, HEAD_DIM - ROPE_HALF), F32)], axis=1)
  return c, s1, s2


def kernel(x_prompt, x_sample, mem_prompt, mem_sample, g_mix_norm, w_in, g_sgu, w_spatial,
           b_spatial, g_mem_norm, w_mem_kv, g_group_out, w_out, g_ffn_norm, w_gate_up,
           w_down, g_final):
  n_prompt, s_prompt, _ = x_prompt.shape
  n_sample, s_sample, _ = x_sample.shape
  assert n_prompt == 1 and s_prompt % s_sample == 0
  assert s_sample % (D16 * DIL_SUB) == 0 and s_sample % TM_FFN == 0
  prompt_rows = n_prompt * s_prompt
  depth = w_in.shape[0]

  x = (x_prompt.reshape(prompt_rows, D_MODEL), x_sample.reshape(n_sample * s_sample, D_MODEL))
  mem = jnp.concatenate([mem_prompt, mem_sample], axis=0)
  n_mems = mem.shape[0]
  mem = mem.reshape(n_mems * N_MEM, D_MODEL)
  rope = _rope_tables(jnp.arange(s_prompt, dtype=jnp.int32))
  seqs = dict(prompt_rows=prompt_rows, s_prompt=s_prompt, s_sample=s_sample)
  w_in, w_spatial, w_mem_kv, w_out, w_gate_up, w_down = (
      w.astype(BF16) for w in (w_in, w_spatial, w_mem_kv, w_out, w_gate_up, w_down))

  kv = _norm_matmul_layers(mem, g_mem_norm, w_mem_kv, tm=n_mems * N_MEM, tn=2 * MEM_WIDTH,
                           name="mem_kv")
  kv = kv.reshape(depth, n_mems, N_MEM, 2 * MEM_WIDTH)

  for l in range(depth):
    g_out = g_group_out[l].reshape(1, MIX_WIDTH)
    b_sp = jnp.broadcast_to(b_spatial[l][:, :, None], (N_SGU_GROUPS, SGU_CHUNK, HEAD_DIM))
    zq, c4, c16, a_n, c_n = _in_proj(
        x, g_mix_norm[l], w_in, l, rope, kv, g_sgu[l].reshape(1, SGU_WIDTH), w_spatial, b_sp,
        g_out, tm=TM_IN_PROJ, rows_per_mem=s_sample,
        first_sample_mem_tile=prompt_rows // s_sample)
    o_list, lse_list = [], []
    for d, src in ((1, zq.reshape(1, *zq.shape)), (D4, c4), (D16, c16)):
      bq = min(DIL_ROWS_PER_STEP, s_sample // d)
      o, lse = _dilated_branch(src, d=d, bq=bq, cps=min(d, DIL_ROWS_PER_STEP // bq), **seqs)
      o_list.append(o)
      lse_list.append(lse)
    x = _mix_out(o_list, lse_list, g_out[:, SGU_WIDTH:SGU_WIDTH + DIL_WIDTH], a_n, c_n, w_out, l, x,
                 tm=TM_MIX_OUT)
    ffn = functools.partial(_ffn, x, g_ffn_norm[l].reshape(1, D_MODEL), w_gate_up, w_down, l,
                            g_final.reshape(1, D_MODEL), tm=TM_FFN, th=TH_FFN)
    if l < depth - 1:
      x = ffn(final_norm=False)

  y_prompt = ffn(final_norm=True, row0=0, rows=prompt_rows)
  y_sample = ffn(final_norm=True, row0=prompt_rows, rows=n_sample * s_sample)
  return (y_prompt.reshape(n_prompt, s_prompt, D_MODEL),
          y_sample.reshape(n_sample, s_sample, D_MODEL))
```

```python
import functools
import math

import jax
import jax.numpy as jnp
from jax import lax
from jax.experimental import pallas as pl
from jax.experimental.pallas import tpu as pltpu

F32 = jnp.float32
BF16 = jnp.bfloat16

D_MODEL = 2048
HEAD_DIM = 128
N_SGU_GROUPS = 4
SGU_WIDTH = N_SGU_GROUPS * HEAD_DIM
SGU_CHUNK = 128
N_DIL_HEADS = 8
DIL_WIDTH = N_DIL_HEADS * HEAD_DIM
DILATIONS = (1, 4, 16)
HALF = 64
N_MEM_HEADS = 4
MEM_WIDTH = N_MEM_HEADS * HEAD_DIM
N_MEM = 256
MIX_WIDTH = SGU_WIDTH + DIL_WIDTH + MEM_WIDTH
IN_WIDTH = 2 * SGU_WIDTH + 3 * DIL_WIDTH + MEM_WIDTH
ROPE_THETA = 500000.0
ROPE_DIM = HEAD_DIM // 4
ROPE_HALF = ROPE_DIM // 2
FFN_HIDDEN = 5632
EPS = 1e-6
NEG_INF = -1e30
ATTN_SCALE = HEAD_DIM ** -0.5
DIL_Q_SCALE = ATTN_SCALE * math.log2(math.e)

V7X_VMEM_BYTES = 64 * 1024 * 1024
VMEM_LIMIT = V7X_VMEM_BYTES - 8 * 1024 * 1024

IN_CHUNK_N = 256
IN_SLABS = 4
BIG_TILE_VMEM_LIMIT = V7X_VMEM_BYTES - 3 * 1024 * 1024
IN_V0 = SGU_WIDTH
IN_QKV0 = 2 * SGU_WIDTH
IN_QC0 = IN_QKV0 + 3 * DIL_WIDTH
QKV_WIDTH = 3 * DIL_WIDTH
D4, D16 = DILATIONS[1], DILATIONS[2]
D16_PER_D4 = D16 // D4

TM_IN_PROJ = 512
TM_MIX_OUT = 512
OUT_CHUNK_N = 256
TM_FFN = 1024
TH_FFN = 512
DIL_ROWS_PER_STEP = 2048


def _params(*sem, vmem=VMEM_LIMIT):
  return pltpu.CompilerParams(dimension_semantics=sem, vmem_limit_bytes=vmem)


def _rms_scale(x):
  return lax.rsqrt(jnp.mean(x * x, axis=-1, keepdims=True) + EPS)


def _norm_matmul_kernel(x_ref, g_ref, w_ref, o_ref, h_ref):
  @pl.when(pl.program_id(2) == 0)
  def _():
    xf = x_ref[...]
    h_ref[...] = (xf * _rms_scale(xf) * g_ref[...]).astype(BF16)

  o_ref[...] = jnp.dot(h_ref[...], w_ref[...], preferred_element_type=F32).astype(o_ref.dtype)


def _norm_matmul_layers(x, g, w, *, tm, tn, name):
  m, k = x.shape
  depth, _, n = w.shape
  return pl.pallas_call(
      _norm_matmul_kernel,
      grid=(depth, m // tm, n // tn),
      in_specs=[pl.BlockSpec((tm, k), lambda l, i, j: (i, 0)),
                pl.BlockSpec((None, 1, k), lambda l, i, j: (l, 0, 0)),
                pl.BlockSpec((None, k, tn), lambda l, i, j: (l, 0, j))],
      out_specs=pl.BlockSpec((None, tm, tn), lambda l, i, j: (l, i, j)),
      out_shape=jax.ShapeDtypeStruct((depth, m, n), BF16),
      scratch_shapes=[pltpu.VMEM((tm, k), BF16)],
      compiler_params=_params("parallel", "parallel", "arbitrary"),
      name=name,
  )(x, g.reshape(depth, 1, k), w)


def _in_col_kind(col):
  bounds = ((IN_V0, "u"), (IN_QKV0, "v"), (IN_QKV0 + DIL_WIDTH, "q"),
            (IN_QKV0 + 2 * DIL_WIDTH, "k"), (IN_QC0, "vb"))
  for end, kind in bounds:
    if col < end:
      return kind
  return "qc"


def _mem_attention(q_heads, kv_ref, go_ref, o_ref):
  outs = []
  ssq = jnp.zeros((q_heads[0].shape[0], 1), F32)
  for h, q in enumerate(q_heads):
    k = kv_ref[:, h * HEAD_DIM:(h + 1) * HEAD_DIM]
    v = kv_ref[:, MEM_WIDTH + h * HEAD_DIM:MEM_WIDTH + (h + 1) * HEAD_DIM]
    s = lax.dot_general(q, k, (((1,), (1,)), ((), ())), preferred_element_type=F32) * ATTN_SCALE
    m = jnp.max(s, axis=-1, keepdims=True)
    p = jnp.exp(s - m)
    den = jnp.sum(p, axis=-1, keepdims=True)
    o = jnp.dot(p.astype(BF16), v, preferred_element_type=F32) / den
    ssq = ssq + jnp.sum(o * o, axis=-1, keepdims=True)
    outs.append(o)
  scale = lax.rsqrt(ssq * (1.0 / MEM_WIDTH) + EPS)
  for h in range(N_MEM_HEADS):
    cs = slice(h * HEAD_DIM, (h + 1) * HEAD_DIM)
    o_ref[:, cs] = (outs[h] * scale * go_ref[:, cs]).astype(o_ref.dtype)


def _spatial_gating(uv_ref, gs_ref, w_ref, b_ref, go_ref, o_ref):
  for c in range(uv_ref.shape[0] // SGU_CHUNK):
    rs = slice(c * SGU_CHUNK, (c + 1) * SGU_CHUNK)
    outs = []
    ssq = jnp.zeros((SGU_CHUNK, 1), F32)
    for g in range(N_SGU_GROUPS):
      cs = slice(g * HEAD_DIM, (g + 1) * HEAD_DIM)
      v = uv_ref[rs, IN_V0 + g * HEAD_DIM:IN_V0 + (g + 1) * HEAD_DIM]
      vv = v * _rms_scale(v) * gs_ref[:, cs]
      vs = jnp.dot(w_ref[g], vv.astype(BF16), preferred_element_type=F32) + b_ref[g]
      a = uv_ref[rs, cs] * vs
      ssq = ssq + jnp.sum(a * a, axis=-1, keepdims=True)
      outs.append(a)
    scale = lax.rsqrt(ssq * (1.0 / SGU_WIDTH) + EPS)
    for g in range(N_SGU_GROUPS):
      cs = slice(g * HEAD_DIM, (g + 1) * HEAD_DIM)
      o_ref[rs, cs] = (outs[g] * scale * go_ref[:, cs]).astype(o_ref.dtype)


def _in_proj_kernel(x_ref, g_ref, w_ref, c_ref, s1_ref, s2_ref, kv_ref, gs_ref, wsp_ref, bsp_ref,
                    go_ref, z_ref, c4_ref, c16_ref, a_ref, cn_ref, h_ref, slab, slab4, uv_ref):
  _in_proj_rows(x_ref[...], g_ref, w_ref, c_ref, s1_ref, s2_ref, kv_ref, gs_ref, wsp_ref, bsp_ref,
                go_ref, z_ref, c4_ref, c16_ref, a_ref, cn_ref, h_ref, slab, slab4, uv_ref)


def _in_proj_kernel_two_sources(xp_ref, xs_ref, g_ref, w_ref, c_ref, s1_ref, s2_ref, kv_ref, gs_ref,
                                wsp_ref, bsp_ref, go_ref, z_ref, c4_ref, c16_ref, a_ref, cn_ref,
                                h_ref, slab, slab4, uv_ref, xbuf, sem, *, prompt_tiles):
  i, n = pl.program_id(0), pl.num_programs(0)
  tm = xbuf.shape[1]
  slot = i % 2

  def copy(src_ref, src_tile, dst_slot):
    return pltpu.make_async_copy(src_ref.at[pl.ds(src_tile * tm, tm), :], xbuf.at[dst_slot],
                                 sem.at[dst_slot])

  def for_tile(tile, dst_slot, action):
    @pl.when(tile < prompt_tiles)
    def _():
      action(copy(xp_ref, tile, dst_slot))

    @pl.when(tile >= prompt_tiles)
    def _():
      action(copy(xs_ref, tile - prompt_tiles, dst_slot))

  @pl.when(i == 0)
  def _():
    for_tile(i, slot, lambda c: c.start())

  @pl.when(i + 1 < n)
  def _():
    for_tile(i + 1, 1 - slot, lambda c: c.start())

  for_tile(i, slot, lambda c: c.wait())
  _in_proj_rows(xbuf[slot], g_ref, w_ref, c_ref, s1_ref, s2_ref, kv_ref, gs_ref, wsp_ref, bsp_ref,
                go_ref, z_ref, c4_ref, c16_ref, a_ref, cn_ref, h_ref, slab, slab4, uv_ref)


def _in_proj_rows(xf, g_ref, w_ref, c_ref, s1_ref, s2_ref, kv_ref, gs_ref, wsp_ref, bsp_ref,
                  go_ref, z_ref, c4_ref, c16_ref, a_ref, cn_ref, h_ref, slab, slab4, uv_ref):
  tm = xf.shape[0]
  h_ref[...] = (xf * _rms_scale(xf) * g_ref[...]).astype(BF16)
  c, s1, s2 = c_ref[...], s1_ref[...], s2_ref[...]
  tables = {"k": (c, s1, s2), "q": (c * DIL_Q_SCALE, s1 * DIL_Q_SCALE, s2 * DIL_Q_SCALE)}
  n_slabs = slab.shape[0]
  heads_per_chunk = IN_CHUNK_N // HEAD_DIM
  n_chunks = IN_WIDTH // IN_CHUNK_N
  first_qc, first_qkv = IN_QC0 // IN_CHUNK_N, IN_QKV0 // IN_CHUNK_N
  order = (list(range(first_qc, n_chunks)) + list(range(first_qkv))
           + list(range(first_qkv, first_qc)))
  qc_heads = []
  for chunk in order:
    col0 = chunk * IN_CHUNK_N
    acc = jnp.dot(h_ref[...], w_ref[:, col0:col0 + IN_CHUNK_N], preferred_element_type=F32)
    for hh in range(heads_per_chunk):
      col = col0 + hh * HEAD_DIM
      kind = _in_col_kind(col)
      t = acc[:, hh * HEAD_DIM:(hh + 1) * HEAD_DIM]
      if kind == "qc":
        qc_heads.append(t.astype(BF16))
        continue
      if kind in ("u", "v"):
        uv_ref[:, col:col + HEAD_DIM] = jax.nn.gelu(t)
        continue
      if kind in tables:
        tc, ts1, ts2 = tables[kind]
        t = (t * tc + pltpu.roll(t, ROPE_HALF, 1) * ts1
             + pltpu.roll(t, HEAD_DIM - ROPE_HALF, 1) * ts2)
      cs = slice(col - IN_QKV0, col - IN_QKV0 + HEAD_DIM)
      z_ref[:, cs] = t.astype(z_ref.dtype)
      sl = (chunk * heads_per_chunk + hh) % n_slabs
      slab[sl] = t
      for rho in range(D4):
        t4 = slab.at[sl][pl.ds(rho, tm // D4, stride=D4), :]
        c4_ref[rho, :, cs] = t4.astype(c4_ref.dtype)
        slab4[sl, rho] = t4
        for q in range(D16_PER_D4):
          t16 = slab4.at[sl, rho][pl.ds(q, tm // D16, stride=D16_PER_D4), :]
          c16_ref[D4 * q + rho, :, cs] = t16.astype(c16_ref.dtype)
    if chunk == n_chunks - 1:
      _mem_attention(qc_heads, kv_ref, go_ref.at[:, SGU_WIDTH + DIL_WIDTH:], cn_ref)
    if chunk == first_qkv - 1:
      _spatial_gating(uv_ref, gs_ref, wsp_ref, bsp_ref, go_ref.at[:, :SGU_WIDTH], a_ref)


def _in_proj(x, g, w, layer, rope, kv, g_sgu, w_sp, b_sp, g_out, *, tm, rows_per_mem,
             first_sample_mem_tile):
  two_sources = isinstance(x, tuple)
  xs = x if two_sources else (x,)
  tok = sum(a.shape[0] for a in xs)
  k = xs[0].shape[1]
  tiles_per_mem = rows_per_mem // tm
  prompt_tiles = first_sample_mem_tile * tiles_per_mem

  def kv_map(i):
    return (layer, jnp.maximum(i // tiles_per_mem - first_sample_mem_tile + 1, 0), 0, 0)

  def pos_map(i):
    return (jnp.where(i < prompt_tiles, i, (i - prompt_tiles) % tiles_per_mem), 0)

  const = lambda shape: pl.BlockSpec(shape, lambda i: (0,) * len(shape))
  scratch = [pltpu.VMEM((tm, k), BF16),
             pltpu.VMEM((IN_SLABS, tm, HEAD_DIM), F32),
             pltpu.VMEM((IN_SLABS, D4, tm // D4, HEAD_DIM), F32),
             pltpu.VMEM((tm, 2 * SGU_WIDTH), F32)]
  if two_sources:
    assert xs[0].shape[0] == prompt_tiles * tm
    body = functools.partial(_in_proj_kernel_two_sources, prompt_tiles=prompt_tiles)
    x_specs = [pl.BlockSpec(memory_space=pl.ANY)] * 2
    scratch += [pltpu.VMEM((2, tm, k), F32), pltpu.SemaphoreType.DMA((2,))]
    semantics = "arbitrary"
  else:
    body = _in_proj_kernel
    x_specs = [pl.BlockSpec((tm, k), lambda i: (i, 0))]
    semantics = "parallel"
  return pl.pallas_call(
      body,
      grid=(tok // tm,),
      in_specs=x_specs + [
          const((1, k)),
          pl.BlockSpec((None, k, IN_WIDTH), lambda i: (layer, 0, 0), pipeline_mode=pl.Buffered(1)),
          pl.BlockSpec((tm, HEAD_DIM), pos_map),
          pl.BlockSpec((tm, HEAD_DIM), pos_map),
          pl.BlockSpec((tm, HEAD_DIM), pos_map),
          pl.BlockSpec((None, None, N_MEM, 2 * MEM_WIDTH), kv_map),
          const((1, SGU_WIDTH)),
          pl.BlockSpec((None, N_SGU_GROUPS, SGU_CHUNK, SGU_CHUNK), lambda i: (layer, 0, 0, 0)),
          const((N_SGU_GROUPS, SGU_CHUNK, HEAD_DIM)),
          const((1, MIX_WIDTH))],
      out_specs=[pl.BlockSpec((tm, QKV_WIDTH), lambda i: (i, 0)),
                 pl.BlockSpec((D4, tm // D4, QKV_WIDTH), lambda i: (0, i, 0)),
                 pl.BlockSpec((D16, tm // D16, QKV_WIDTH), lambda i: (0, i, 0)),
                 pl.BlockSpec((tm, SGU_WIDTH), lambda i: (i, 0)),
                 pl.BlockSpec((tm, MEM_WIDTH), lambda i: (i, 0))],
      out_shape=[jax.ShapeDtypeStruct((tok, QKV_WIDTH), BF16),
                 jax.ShapeDtypeStruct((D4, tok // D4, QKV_WIDTH), BF16),
                 jax.ShapeDtypeStruct((D16, tok // D16, QKV_WIDTH), BF16),
                 jax.ShapeDtypeStruct((tok, SGU_WIDTH), BF16),
                 jax.ShapeDtypeStruct((tok, MEM_WIDTH), BF16)],
      scratch_shapes=scratch,
      compiler_params=_params(semantics, vmem=BIG_TILE_VMEM_LIMIT),
      name="in_proj",
  )(*xs, g.reshape(1, k), w, *rope, kv, g_sgu, w_sp, b_sp, g_out)


DIL_SUB = 2 * HALF
DIL_KEYS = DIL_SUB + 2 * HALF
DIL_DEN_LANE0 = N_DIL_HEADS


def _dil_kernel(q_ref, kp_ref, kc_ref, kn_ref, vp_ref, vc_ref, vn_ref, o_ref, st_ref,
                kbuf, vbuf, *, class_len_prompt, class_len_sample, prompt_rows):
  cps, bq = q_ref.shape[0], q_ref.shape[1]
  row0 = pl.program_id(1) * bq
  in_prompt = row0 < prompt_rows
  clen = jnp.where(in_prompt, class_len_prompt, class_len_sample)
  pos = jnp.where(in_prompt, row0, row0 - prompt_rows) & (clen - 1)
  first = pos == 0
  last = pos + bq == clen

  for cls in range(cps):
    kbuf[cls, 0:HALF, :] = kp_ref[cls]
    kbuf[cls, HALF:HALF + bq, :] = kc_ref[cls]
    kbuf[cls, HALF + bq:, :] = kn_ref[cls]
    vbuf[cls, 0:HALF, :] = vp_ref[cls]
    vbuf[cls, HALF:HALF + bq, :] = vc_ref[cls]
    vbuf[cls, HALF + bq:, :] = vn_ref[cls]

  r = lax.broadcasted_iota(jnp.int32, (DIL_SUB, DIL_KEYS), 0)
  c = lax.broadcasted_iota(jnp.int32, (DIL_SUB, DIL_KEYS), 1)
  band = jnp.where((c >= r) & (c <= r + 2 * HALF), 0.0, NEG_INF).astype(F32)
  lane = lax.broadcasted_iota(jnp.int32, (DIL_SUB, HEAD_DIM), 1)
  lo = jnp.where(c < HALF, jnp.where(first, NEG_INF, 0.0), 0.0).astype(F32)
  hi = jnp.where(c >= DIL_KEYS - HALF, jnp.where(last, NEG_INF, 0.0), 0.0).astype(F32)
  nsub = bq // DIL_SUB
  biases = [band] * nsub
  biases[0] = biases[0] + lo
  biases[-1] = biases[-1] + hi

  for cls in range(cps):
    for j in range(nsub):
      rs = slice(j * DIL_SUB, (j + 1) * DIL_SUB)
      ks = slice(j * DIL_SUB, j * DIL_SUB + DIL_KEYS)
      stats = jnp.zeros((DIL_SUB, HEAD_DIM), F32)
      for h in range(N_DIL_HEADS):
        cs = slice(h * HEAD_DIM, (h + 1) * HEAD_DIM)
        s = lax.dot_general(q_ref[cls, rs, cs], kbuf[cls, ks, cs], (((1,), (1,)), ((), ())),
                            preferred_element_type=F32) + biases[j]
        m = jnp.max(s, axis=-1, keepdims=True)
        p = jnp.exp2(s - m)
        den = jnp.sum(p, axis=-1, keepdims=True)
        num = jnp.dot(p.astype(BF16), vbuf[cls, ks, cs], preferred_element_type=F32)
        o_ref[cls, rs, cs] = num.astype(o_ref.dtype)
        stats = jnp.where(lane == h, m, jnp.where(lane == DIL_DEN_LANE0 + h, den, stats))
      st_ref[cls, rs, :] = stats


def _dilated_branch(src, *, d, bq, cps, prompt_rows, s_prompt, s_sample):
  rows = src.shape[1]
  hb = bq // HALF
  n_half_blocks = rows // HALF
  body = functools.partial(
      _dil_kernel, class_len_prompt=s_prompt // d, class_len_sample=s_sample // d,
      prompt_rows=prompt_rows // d)

  def cur(c):
    return pl.BlockSpec((cps, bq, DIL_WIDTH), lambda r, i: (r, i, c))

  def prev(c):
    return pl.BlockSpec((cps, HALF, DIL_WIDTH), lambda r, i: (r, jnp.maximum(i * hb - 1, 0), c))

  def nxt(c):
    return pl.BlockSpec((cps, HALF, DIL_WIDTH),
                        lambda r, i: (r, jnp.minimum((i + 1) * hb, n_half_blocks - 1), c))

  return pl.pallas_call(
      body,
      grid=(d // cps, rows // bq),
      in_specs=[cur(0), prev(1), cur(1), nxt(1), prev(2), cur(2), nxt(2)],
      out_specs=[pl.BlockSpec((cps, bq, DIL_WIDTH), lambda r, i: (r, i, 0)),
                 pl.BlockSpec((cps, bq, HEAD_DIM), lambda r, i: (r, i, 0))],
      out_shape=[jax.ShapeDtypeStruct((d, rows, DIL_WIDTH), BF16),
                 jax.ShapeDtypeStruct((d, rows, HEAD_DIM), F32)],
      scratch_shapes=[pltpu.VMEM((cps, bq + 2 * HALF, DIL_WIDTH), BF16),
                      pltpu.VMEM((cps, bq + 2 * HALF, DIL_WIDTH), BF16)],
      compiler_params=_params("parallel", "parallel"),
      name=f"dilated_d{d}",
  )(*([src] * 7))


def _merge_tile(o1_ref, o4_ref, o16_ref, l1_ref, l4_ref, l16_ref, e_ref, gb_ref, b_ref,
                s4, s16, ls4, ls16, bs):
  tm = o1_ref.shape[0]
  for rho in range(D4):
    ls4[pl.ds(rho, tm // D4, stride=D4), :] = l4_ref[rho]
  for r in range(D16):
    ls16[pl.ds(r, tm // D16, stride=D16), :] = l16_ref[r]

  t1, t2, t3 = l1_ref[...], ls4[...], ls16[...]
  m = jnp.maximum(jnp.maximum(t1, t2), t3)
  w1, w2, w3 = jnp.exp2(t1 - m), jnp.exp2(t2 - m), jnp.exp2(t3 - m)
  den_of = lambda t: pltpu.roll(t, HEAD_DIM - DIL_DEN_LANE0, 1)
  total = w1 * den_of(t1) + w2 * den_of(t2) + w3 * den_of(t3)
  head_lane = lax.broadcasted_iota(jnp.int32, (tm, HEAD_DIM), 1) < N_DIL_HEADS
  inv = 1.0 / jnp.where(head_lane, total, 1.0)

  def over_head_lanes(w):
    wn = w * inv
    hi = wn.astype(BF16)
    lo = (wn - hi.astype(F32)).astype(BF16)
    return jnp.dot(jnp.concatenate([hi, lo], axis=1), e_ref[...], preferred_element_type=F32)

  wb1, wb2, wb3 = over_head_lanes(w1), over_head_lanes(w2), over_head_lanes(w3)

  ssq = jnp.zeros((tm, 1), F32)
  for h in range(N_DIL_HEADS):
    cs = slice(h * HEAD_DIM, (h + 1) * HEAD_DIM)
    for rho in range(D4):
      s4.at[h][pl.ds(rho, tm // D4, stride=D4), :] = o4_ref[rho, :, cs].astype(F32)
    for r in range(D16):
      s16.at[h][pl.ds(r, tm // D16, stride=D16), :] = o16_ref[r, :, cs].astype(F32)
    b = wb1[:, cs] * o1_ref[:, cs].astype(F32) + wb2[:, cs] * s4[h] + wb3[:, cs] * s16[h]
    ssq = ssq + jnp.sum(b * b, axis=-1, keepdims=True)
    bs[h] = b
  scale = jnp.broadcast_to(lax.rsqrt(ssq * (1.0 / DIL_WIDTH) + EPS), (tm, HEAD_DIM))
  for h in range(N_DIL_HEADS):
    cs = slice(h * HEAD_DIM, (h + 1) * HEAD_DIM)
    b_ref[:, cs] = (bs[h] * scale * gb_ref[:, cs]).astype(b_ref.dtype)


def _mix_out_kernel(o1_ref, o4_ref, o16_ref, l1_ref, l4_ref, l16_ref, e_ref, gb_ref,
                    a_ref, c_ref, w_ref, *rest, prompt_tiles):
  if prompt_tiles is None:
    x_ref, y_ref, s4, s16, ls4, ls16, bs, bn = rest
    xs_ref = None
  else:
    x_ref, xs_ref, y_ref, s4, s16, ls4, ls16, bs, bn = rest
  step = pl.program_id(0)
  b0 = SGU_WIDTH
  c0 = SGU_WIDTH + DIL_WIDTH

  def merge_into(slot):
    _merge_tile(o1_ref, o4_ref, o16_ref, l1_ref, l4_ref, l16_ref, e_ref, gb_ref, bn.at[slot],
                s4, s16, ls4, ls16, bs)

  @pl.when(step == 0)
  def _():
    merge_into(0)

  @pl.when(step > 0)
  def _():
    slot = step % 2
    merge_into(slot)
    b_prev = bn.at[1 - slot]
    for chunk in range(D_MODEL // OUT_CHUNK_N):
      cols = slice(chunk * OUT_CHUNK_N, (chunk + 1) * OUT_CHUNK_N)
      acc = jnp.dot(a_ref[...], w_ref[0:b0, cols], preferred_element_type=F32)
      acc += jnp.dot(b_prev[...], w_ref[b0:c0, cols], preferred_element_type=F32)
      acc += jnp.dot(c_ref[...], w_ref[c0:, cols], preferred_element_type=F32)
      res = x_ref[:, cols]
      if xs_ref is not None:
        res = jnp.where(step - 1 < prompt_tiles, res, xs_ref[:, cols])
      y_ref[:, cols] = res + acc


def _mix_out(o_list, lse_list, g_b, a_n, c_n, w_out, layer, x, *, tm):
  o1, o4, o16 = o_list
  l1, l4, l16 = lse_list
  tok = a_n.shape[0]
  n = tok // tm
  head_of_lane = jnp.arange(DIL_WIDTH, dtype=jnp.int32) // HEAD_DIM
  selector = (jnp.arange(HEAD_DIM, dtype=jnp.int32)[:, None] == head_of_lane[None, :]).astype(BF16)
  selector = jnp.concatenate([selector, selector], axis=0)
  head_scratch = pltpu.VMEM((N_DIL_HEADS, tm, HEAD_DIM), F32)
  ahead = lambda s: jnp.minimum(s, n - 1)
  behind = lambda s: jnp.maximum(s - 1, 0)
  const = lambda shape: pl.BlockSpec(shape, lambda s: (0,) * len(shape))
  row = lambda w: pl.BlockSpec((tm, w), lambda s: (behind(s), 0))
  if isinstance(x, tuple):
    prompt_tiles = x[0].shape[0] // tm
    x_specs = [pl.BlockSpec((tm, D_MODEL), lambda s: (jnp.minimum(behind(s), prompt_tiles - 1), 0)),
               pl.BlockSpec((tm, D_MODEL), lambda s: (jnp.maximum(behind(s) - prompt_tiles, 0), 0))]
  else:
    prompt_tiles, x_specs, x = None, [row(D_MODEL)], (x,)
  return pl.pallas_call(
      functools.partial(_mix_out_kernel, prompt_tiles=prompt_tiles),
      grid=(n + 1,),
      in_specs=[pl.BlockSpec((None, tm, DIL_WIDTH), lambda s: (0, ahead(s), 0)),
                pl.BlockSpec((D4, tm // D4, DIL_WIDTH), lambda s: (0, ahead(s), 0)),
                pl.BlockSpec((D16, tm // D16, DIL_WIDTH), lambda s: (0, ahead(s), 0)),
                pl.BlockSpec((None, tm, HEAD_DIM), lambda s: (0, ahead(s), 0)),
                pl.BlockSpec((D4, tm // D4, HEAD_DIM), lambda s: (0, ahead(s), 0)),
                pl.BlockSpec((D16, tm // D16, HEAD_DIM), lambda s: (0, ahead(s), 0)),
                const((2 * HEAD_DIM, DIL_WIDTH)),
                const((1, DIL_WIDTH)),
                row(SGU_WIDTH), row(MEM_WIDTH),
                pl.BlockSpec((None, MIX_WIDTH, D_MODEL), lambda s: (layer, 0, 0),
                             pipeline_mode=pl.Buffered(1))] + x_specs,
      out_specs=row(D_MODEL),
      out_shape=jax.ShapeDtypeStruct((tok, D_MODEL), F32),
      scratch_shapes=[head_scratch, head_scratch,
                      pltpu.VMEM((tm, HEAD_DIM), F32), pltpu.VMEM((tm, HEAD_DIM), F32),
                      head_scratch, pltpu.VMEM((2, tm, DIL_WIDTH), BF16)],
      compiler_params=_params("arbitrary"),
      name="mix_out",
  )(o1, o4, o16, l1, l4, l16, selector, g_b, a_n, c_n, w_out, *x)


def _ffn_kernel(x_ref, g_ref, wg_ref, wu_ref, wd_ref, gf_ref, y_ref, h_ref, *, final_norm):
  k = pl.program_id(1)

  @pl.when(k == 0)
  def _():
    xf = x_ref[...]
    h_ref[...] = (xf * _rms_scale(xf) * g_ref[...]).astype(BF16)
    y_ref[...] = xf

  h = h_ref[...]
  gate = jnp.dot(h, wg_ref[...], preferred_element_type=F32)
  up = jnp.dot(h, wu_ref[...], preferred_element_type=F32)
  act = (jax.nn.silu(gate) * up).astype(BF16)
  y_ref[...] += jnp.dot(act, wd_ref[...], preferred_element_type=F32)

  if final_norm:
    @pl.when(k == pl.num_programs(1) - 1)
    def _():
      y = y_ref[...]
      y_ref[...] = y * _rms_scale(y) * gf_ref[...]


def _ffn(x, g, w_gu, w_down, layer, g_final, *, tm, th, final_norm, row0=0, rows=None):
  tok = x.shape[0] if rows is None else rows
  tile0 = row0 // tm
  nk = FFN_HIDDEN // th
  return pl.pallas_call(
      functools.partial(_ffn_kernel, final_norm=final_norm),
      grid=(tok // tm, nk),
      in_specs=[
          pl.BlockSpec((tm, D_MODEL), lambda i, k: (tile0 + i, 0)),
          pl.BlockSpec((1, D_MODEL), lambda i, k: (0, 0)),
          pl.BlockSpec((None, D_MODEL, th), lambda i, k: (layer, 0, k)),
          pl.BlockSpec((None, D_MODEL, th), lambda i, k: (layer, 0, nk + k)),
          pl.BlockSpec((None, th, D_MODEL), lambda i, k: (layer, k, 0)),
          pl.BlockSpec((1, D_MODEL), lambda i, k: (0, 0)),
      ],
      out_specs=pl.BlockSpec((tm, D_MODEL), lambda i, k: (i, 0)),
      out_shape=jax.ShapeDtypeStruct((tok, D_MODEL), F32),
      scratch_shapes=[pltpu.VMEM((tm, D_MODEL), BF16)],
      compiler_params=_params("parallel", "arbitrary", vmem=BIG_TILE_VMEM_LIMIT),
      name="ffn",
  )(x, g, w_gu, w_gu, w_down, g_final)


def _rope_tables(positions):
  inv = ROPE_THETA ** (-jnp.arange(0, ROPE_DIM, 2, dtype=F32) / ROPE_DIM)
  ang = positions.astype(F32)[:, None] * inv[None, :]
  cos, sin = jnp.cos(ang), jnp.sin(ang)
  n = positions.shape[0]
  rest = HEAD_DIM - ROPE_DIM
  c = jnp.concatenate([cos, cos, jnp.ones((n, rest), F32)], axis=1)
  s1 = jnp.concatenate([jnp.zeros((n, ROPE_HALF), F32), sin, jnp.zeros((n, rest), F32)], axis=1)
  s2 = jnp.concatenate([-sin, jnp.zeros((n, HEAD_DIM - ROPE_HALF), F32)], axis=1)
  return c, s1, s2


def kernel(x_prompt, x_sample, mem_prompt, mem_sample, g_mix_norm, w_in, g_sgu, w_spatial,
           b_spatial, g_mem_norm, w_mem_kv, g_group_out, w_out, g_ffn_norm, w_gate_up,
           w_down, g_final):
  n_prompt, s_prompt, _ = x_prompt.shape
  n_sample, s_sample, _ = x_sample.shape
  assert n_prompt == 1 and s_prompt % s_sample == 0
  assert s_sample % (D16 * DIL_SUB) == 0 and s_sample % TM_FFN == 0
  prompt_rows = n_prompt * s_prompt
  depth = w_in.shape[0]

  x = (x_prompt.reshape(prompt_rows, D_MODEL), x_sample.reshape(n_sample * s_sample, D_MODEL))
  mem = jnp.concatenate([mem_prompt, mem_sample], axis=0)
  n_mems = mem.shape[0]
  mem = mem.reshape(n_mems * N_MEM, D_MODEL)
  rope = _rope_tables(jnp.arange(s_prompt, dtype=jnp.int32))
  seqs = dict(prompt_rows=prompt_rows, s_prompt=s_prompt, s_sample=s_sample)
  w_in, w_spatial, w_mem_kv, w_out, w_gate_up, w_down = (
      w.astype(BF16) for w in (w_in, w_spatial, w_mem_kv, w_out, w_gate_up, w_down))

  kv = _norm_matmul_layers(mem, g_mem_norm, w_mem_kv, tm=n_mems * N_MEM, tn=2 * MEM_WIDTH,
                           name="mem_kv")
  kv = kv.reshape(depth, n_mems, N_MEM, 2 * MEM_WIDTH)

  for l in range(depth):
    g_out = g_group_out[l].reshape(1, MIX_WIDTH)
    b_sp = jnp.broadcast_to(b_spatial[l][:, :, None], (N_SGU_GROUPS, SGU_CHUNK, HEAD_DIM))
    zq, c4, c16, a_n, c_n = _in_proj(
        x, g_mix_norm[l], w_in, l, rope, kv, g_sgu[l].reshape(1, SGU_WIDTH), w_spatial, b_sp,
        g_out, tm=TM_IN_PROJ, rows_per_mem=s_sample,
        first_sample_mem_tile=prompt_rows // s_sample)
    o_list, lse_list = [], []
    for d, src in ((1, zq.reshape(1, *zq.shape)), (D4, c4), (D16, c16)):
      bq = min(DIL_ROWS_PER_STEP, s_sample // d)
      o, lse = _dilated_branch(src, d=d, bq=bq, cps=min(d, DIL_ROWS_PER_STEP // bq), **seqs)
      o_list.append(o)
      lse_list.append(lse)
    x = _mix_out(o_list, lse_list, g_out[:, SGU_WIDTH:SGU_WIDTH + DIL_WIDTH], a_n, c_n, w_out, l, x,
                 tm=TM_MIX_OUT)
    ffn = functools.partial(_ffn, x, g_ffn_norm[l].reshape(1, D_MODEL), w_gate_up, w_down, l,
                            g_final.reshape(1, D_MODEL), tm=TM_FFN, th=TH_FFN)
    if l < depth - 1:
      x = ffn(final_norm=False)

  y_prompt = ffn(final_norm=True, row0=0, rows=prompt_rows)
  y_sample = ffn(final_norm=True, row0=prompt_rows, rows=n_sample * s_sample)
  return (y_prompt.reshape(n_prompt, s_prompt, D_MODEL),
          y_sample.reshape(n_sample, s_sample, D_MODEL))
```

```python
import functools
import math

import jax
import jax.numpy as jnp
from jax import lax
from jax.experimental import pallas as pl
from jax.experimental.pallas import tpu as pltpu

F32 = jnp.float32
BF16 = jnp.bfloat16

D_MODEL = 2048
HEAD_DIM = 128
N_SGU_GROUPS = 4
SGU_WIDTH = N_SGU_GROUPS * HEAD_DIM
SGU_CHUNK = 128
N_DIL_HEADS = 8
DIL_WIDTH = N_DIL_HEADS * HEAD_DIM
DILATIONS = (1, 4, 16)
HALF = 64
N_MEM_HEADS = 4
MEM_WIDTH = N_MEM_HEADS * HEAD_DIM
N_MEM = 256
MIX_WIDTH = SGU_WIDTH + DIL_WIDTH + MEM_WIDTH
IN_WIDTH = 2 * SGU_WIDTH + 3 * DIL_WIDTH + MEM_WIDTH
ROPE_THETA = 500000.0
ROPE_DIM = HEAD_DIM // 4
ROPE_HALF = ROPE_DIM // 2
FFN_HIDDEN = 5632
EPS = 1e-6
NEG_INF = -1e30
ATTN_SCALE = HEAD_DIM ** -0.5
DIL_Q_SCALE = ATTN_SCALE * math.log2(math.e)

V7X_VMEM_BYTES = 64 * 1024 * 1024
VMEM_LIMIT = V7X_VMEM_BYTES - 8 * 1024 * 1024

IN_CHUNK_N = 256
IN_SLABS = 4
BIG_TILE_VMEM_LIMIT = V7X_VMEM_BYTES - 3 * 1024 * 1024
IN_V0 = SGU_WIDTH
IN_QKV0 = 2 * SGU_WIDTH
IN_QC0 = IN_QKV0 + 3 * DIL_WIDTH
QKV_WIDTH = 3 * DIL_WIDTH
D4, D16 = DILATIONS[1], DILATIONS[2]
D16_PER_D4 = D16 // D4

TM_IN_PROJ = 512
TM_MIX_OUT = 512
OUT_CHUNK_N = 256
TM_FFN = 1024
TH_FFN = 512
DIL_ROWS_PER_STEP = 2048


def _params(*sem, vmem=VMEM_LIMIT):
  return pltpu.CompilerParams(dimension_semantics=sem, vmem_limit_bytes=vmem)


def _rms_scale(x):
  return lax.rsqrt(jnp.mean(x * x, axis=-1, keepdims=True) + EPS)


def _norm_matmul_kernel(x_ref, g_ref, w_ref, o_ref, h_ref):
  @pl.when(pl.program_id(2) == 0)
  def _():
    xf = x_ref[...]
    h_ref[...] = (xf * _rms_scale(xf) * g_ref[...]).astype(BF16)

  o_ref[...] = jnp.dot(h_ref[...], w_ref[...], preferred_element_type=F32).astype(o_ref.dtype)


def _norm_matmul_layers(x, g, w, *, tm, tn, name):
  m, k = x.shape
  depth, _, n = w.shape
  return pl.pallas_call(
      _norm_matmul_kernel,
      grid=(depth, m // tm, n // tn),
      in_specs=[pl.BlockSpec((tm, k), lambda l, i, j: (i, 0)),
                pl.BlockSpec((None, 1, k), lambda l, i, j: (l, 0, 0)),
                pl.BlockSpec((None, k, tn), lambda l, i, j: (l, 0, j))],
      out_specs=pl.BlockSpec((None, tm, tn), lambda l, i, j: (l, i, j)),
      out_shape=jax.ShapeDtypeStruct((depth, m, n), BF16),
      scratch_shapes=[pltpu.VMEM((tm, k), BF16)],
      compiler_params=_params("parallel", "parallel", "arbitrary"),
      name=name,
  )(x, g.reshape(depth, 1, k), w)


def _in_col_kind(col):
  bounds = ((IN_V0, "u"), (IN_QKV0, "v"), (IN_QKV0 + DIL_WIDTH, "q"),
            (IN_QKV0 + 2 * DIL_WIDTH, "k"), (IN_QC0, "vb"))
  for end, kind in bounds:
    if col < end:
      return kind
  return "qc"


def _mem_attention(q_heads, kv_ref, go_ref, o_ref):
  outs = []
  ssq = jnp.zeros((q_heads[0].shape[0], 1), F32)
  for h, q in enumerate(q_heads):
    k = kv_ref[:, h * HEAD_DIM:(h + 1) * HEAD_DIM]
    v = kv_ref[:, MEM_WIDTH + h * HEAD_DIM:MEM_WIDTH + (h + 1) * HEAD_DIM]
    s = lax.dot_general(q, k, (((1,), (1,)), ((), ())), preferred_element_type=F32) * ATTN_SCALE
    m = jnp.max(s, axis=-1, keepdims=True)
    p = jnp.exp(s - m)
    den = jnp.sum(p, axis=-1, keepdims=True)
    o = jnp.dot(p.astype(BF16), v, preferred_element_type=F32) / den
    ssq = ssq + jnp.sum(o * o, axis=-1, keepdims=True)
    outs.append(o)
  scale = lax.rsqrt(ssq * (1.0 / MEM_WIDTH) + EPS)
  for h in range(N_MEM_HEADS):
    cs = slice(h * HEAD_DIM, (h + 1) * HEAD_DIM)
    o_ref[:, cs] = (outs[h] * scale * go_ref[:, cs]).astype(o_ref.dtype)


def _spatial_gating(uv_ref, gs_ref, w_ref, b_ref, go_ref, o_ref):
  for c in range(uv_ref.shape[0] // SGU_CHUNK):
    rs = slice(c * SGU_CHUNK, (c + 1) * SGU_CHUNK)
    outs = []
    ssq = jnp.zeros((SGU_CHUNK, 1), F32)
    for g in range(N_SGU_GROUPS):
      cs = slice(g * HEAD_DIM, (g + 1) * HEAD_DIM)
      v = uv_ref[rs, IN_V0 + g * HEAD_DIM:IN_V0 + (g + 1) * HEAD_DIM]
      vv = v * _rms_scale(v) * gs_ref[:, cs]
      vs = jnp.dot(w_ref[g], vv.astype(BF16), preferred_element_type=F32) + b_ref[g]
      a = uv_ref[rs, cs] * vs
      ssq = ssq + jnp.sum(a * a, axis=-1, keepdims=True)
      outs.append(a)
    scale = lax.rsqrt(ssq * (1.0 / SGU_WIDTH) + EPS)
    for g in range(N_SGU_GROUPS):
      cs = slice(g * HEAD_DIM, (g + 1) * HEAD_DIM)
      o_ref[rs, cs] = (outs[g] * scale * go_ref[:, cs]).astype(o_ref.dtype)


def _in_proj_kernel(x_ref, g_ref, w_ref, c_ref, s1_ref, s2_ref, kv_ref, gs_ref, wsp_ref, bsp_ref,
                    go_ref, z_ref, c4_ref, c16_ref, a_ref, cn_ref, h_ref, slab, slab4, uv_ref):
  _in_proj_rows(x_ref[...], g_ref, w_ref, c_ref, s1_ref, s2_ref, kv_ref, gs_ref, wsp_ref, bsp_ref,
                go_ref, z_ref, c4_ref, c16_ref, a_ref, cn_ref, h_ref, slab, slab4, uv_ref)


def _in_proj_kernel_two_sources(xp_ref, xs_ref, g_ref, w_ref, c_ref, s1_ref, s2_ref, kv_ref, gs_ref,
                                wsp_ref, bsp_ref, go_ref, z_ref, c4_ref, c16_ref, a_ref, cn_ref,
                                h_ref, slab, slab4, uv_ref, xbuf, sem, *, prompt_tiles):
  i, n = pl.program_id(0), pl.num_programs(0)
  tm = xbuf.shape[1]
  slot = i % 2

  def copy(src_ref, src_tile, dst_slot):
    return pltpu.make_async_copy(src_ref.at[pl.ds(src_tile * tm, tm), :], xbuf.at[dst_slot],
                                 sem.at[dst_slot])

  def for_tile(tile, dst_slot, action):
    @pl.when(tile < prompt_tiles)
    def _():
      action(copy(xp_ref, tile, dst_slot))

    @pl.when(tile >= prompt_tiles)
    def _():
      action(copy(xs_ref, tile - prompt_tiles, dst_slot))

  @pl.when(i == 0)
  def _():
    for_tile(i, slot, lambda c: c.start())

  @pl.when(i + 1 < n)
  def _():
    for_tile(i + 1, 1 - slot, lambda c: c.start())

  for_tile(i, slot, lambda c: c.wait())
  _in_proj_rows(xbuf[slot], g_ref, w_ref, c_ref, s1_ref, s2_ref, kv_ref, gs_ref, wsp_ref, bsp_ref,
                go_ref, z_ref, c4_ref, c16_ref, a_ref, cn_ref, h_ref, slab, slab4, uv_ref)


def _in_proj_rows(xf, g_ref, w_ref, c_ref, s1_ref, s2_ref, kv_ref, gs_ref, wsp_ref, bsp_ref,
                  go_ref, z_ref, c4_ref, c16_ref, a_ref, cn_ref, h_ref, slab, slab4, uv_ref):
  tm = xf.shape[0]
  h_ref[...] = (xf * _rms_scale(xf) * g_ref[...]).astype(BF16)
  c, s1, s2 = c_ref[...], s1_ref[...], s2_ref[...]
  tables = {"k": (c, s1, s2), "q": (c * DIL_Q_SCALE, s1 * DIL_Q_SCALE, s2 * DIL_Q_SCALE)}
  n_slabs = slab.shape[0]
  heads_per_chunk = IN_CHUNK_N // HEAD_DIM
  n_chunks = IN_WIDTH // IN_CHUNK_N
  first_qc, first_qkv = IN_QC0 // IN_CHUNK_N, IN_QKV0 // IN_CHUNK_N
  order = (list(range(first_qc, n_chunks)) + list(range(first_qkv))
           + list(range(first_qkv, first_qc)))
  qc_heads = []
  for chunk in order:
    col0 = chunk * IN_CHUNK_N
    acc = jnp.dot(h_ref[...], w_ref[:, col0:col0 + IN_CHUNK_N], preferred_element_type=F32)
    for hh in range(heads_per_chunk):
      col = col0 + hh * HEAD_DIM
      kind = _in_col_kind(col)
      t = acc[:, hh * HEAD_DIM:(hh + 1) * HEAD_DIM]
      if kind == "qc":
        qc_heads.append(t.astype(BF16))
        continue
      if kind in ("u", "v"):
        uv_ref[:, col:col + HEAD_DIM] = jax.nn.gelu(t)
        continue
      if kind in tables:
        tc, ts1, ts2 = tables[kind]
        t = (t * tc + pltpu.roll(t, ROPE_HALF, 1) * ts1
             + pltpu.roll(t, HEAD_DIM - ROPE_HALF, 1) * ts2)
      cs = slice(col - IN_QKV0, col - IN_QKV0 + HEAD_DIM)
      z_ref[:, cs] = t.astype(z_ref.dtype)
      sl = (chunk * heads_per_chunk + hh) % n_slabs
      slab[sl] = t
      for rho in range(D4):
        t4 = slab.at[sl][pl.ds(rho, tm // D4, stride=D4), :]
        c4_ref[rho, :, cs] = t4.astype(c4_ref.dtype)
        slab4[sl, rho] = t4
        for q in range(D16_PER_D4):
          t16 = slab4.at[sl, rho][pl.ds(q, tm // D16, stride=D16_PER_D4), :]
          c16_ref[D4 * q + rho, :, cs] = t16.astype(c16_ref.dtype)
    if chunk == n_chunks - 1:
      _mem_attention(qc_heads, kv_ref, go_ref.at[:, SGU_WIDTH + DIL_WIDTH:], cn_ref)
    if chunk == first_qkv - 1:
      _spatial_gating(uv_ref, gs_ref, wsp_ref, bsp_ref, go_ref.at[:, :SGU_WIDTH], a_ref)


def _in_proj(x, g, w, layer, rope, kv, g_sgu, w_sp, b_sp, g_out, *, tm, rows_per_mem,
             first_sample_mem_tile):
  two_sources = isinstance(x, tuple)
  xs = x if two_sources else (x,)
  tok = sum(a.shape[0] for a in xs)
  k = xs[0].shape[1]
  tiles_per_mem = rows_per_mem // tm
  prompt_tiles = first_sample_mem_tile * tiles_per_mem

  def kv_map(i):
    return (layer, jnp.maximum(i // tiles_per_mem - first_sample_mem_tile + 1, 0), 0, 0)

  def pos_map(i):
    return (jnp.where(i < prompt_tiles, i, (i - prompt_tiles) % tiles_per_mem), 0)

  const = lambda shape: pl.BlockSpec(shape, lambda i: (0,) * len(shape))
  scratch = [pltpu.VMEM((tm, k), BF16),
             pltpu.VMEM((IN_SLABS, tm, HEAD_DIM), F32),
             pltpu.VMEM((IN_SLABS, D4, tm // D4, HEAD_DIM), F32),
             pltpu.VMEM((tm, 2 * SGU_WIDTH), F32)]
  if two_sources:
    assert xs[0].shape[0] == prompt_tiles * tm
    body = functools.partial(_in_proj_kernel_two_sources, prompt_tiles=prompt_tiles)
    x_specs = [pl.BlockSpec(memory_space=pl.ANY)] * 2
    scratch += [pltpu.VMEM((2, tm, k), F32), pltpu.SemaphoreType.DMA((2,))]
    semantics = "arbitrary"
  else:
    body = _in_proj_kernel
    x_specs = [pl.BlockSpec((tm, k), lambda i: (i, 0))]
    semantics = "parallel"
  return pl.pallas_call(
      body,
      grid=(tok // tm,),
      in_specs=x_specs + [
          const((1, k)),
          pl.BlockSpec((None, k, IN_WIDTH), lambda i: (layer, 0, 0), pipeline_mode=pl.Buffered(1)),
          pl.BlockSpec((tm, HEAD_DIM), pos_map),
          pl.BlockSpec((tm, HEAD_DIM), pos_map),
          pl.BlockSpec((tm, HEAD_DIM), pos_map),
          pl.BlockSpec((None, None, N_MEM, 2 * MEM_WIDTH), kv_map),
          const((1, SGU_WIDTH)),
          pl.BlockSpec((None, N_SGU_GROUPS, SGU_CHUNK, SGU_CHUNK), lambda i: (layer, 0, 0, 0)),
          const((N_SGU_GROUPS, SGU_CHUNK, HEAD_DIM)),
          const((1, MIX_WIDTH))],
      out_specs=[pl.BlockSpec((tm, QKV_WIDTH), lambda i: (i, 0)),
                 pl.BlockSpec((D4, tm // D4, QKV_WIDTH), lambda i: (0, i, 0)),
                 pl.BlockSpec((D16, tm // D16, QKV_WIDTH), lambda i: (0, i, 0)),
                 pl.BlockSpec((tm, SGU_WIDTH), lambda i: (i, 0)),
                 pl.BlockSpec((tm, MEM_WIDTH), lambda i: (i, 0))],
      out_shape=[jax.ShapeDtypeStruct((tok, QKV_WIDTH), BF16),
                 jax.ShapeDtypeStruct((D4, tok // D4, QKV_WIDTH), BF16),
                 jax.ShapeDtypeStruct((D16, tok // D16, QKV_WIDTH), BF16),
                 jax.ShapeDtypeStruct((tok, SGU_WIDTH), BF16),
                 jax.ShapeDtypeStruct((tok, MEM_WIDTH), BF16)],
      scratch_shapes=scratch,
      compiler_params=_params(semantics, vmem=BIG_TILE_VMEM_LIMIT),
      name="in_proj",
  )(*xs, g.reshape(1, k), w, *rope, kv, g_sgu, w_sp, b_sp, g_out)


DIL_SUB = 2 * HALF
DIL_KEYS = DIL_SUB + 2 * HALF
DIL_DEN_LANE0 = N_DIL_HEADS


def _dil_kernel(q_ref, kp_ref, kc_ref, kn_ref, vp_ref, vc_ref, vn_ref, o_ref, st_ref,
                kbuf, vbuf, *, class_len_prompt, class_len_sample, prompt_rows):
  cps, bq = q_ref.shape[0], q_ref.shape[1]
  row0 = pl.program_id(1) * bq
  in_prompt = row0 < prompt_rows
  clen = jnp.where(in_prompt, class_len_prompt, class_len_sample)
  pos = jnp.where(in_prompt, row0, row0 - prompt_rows) & (clen - 1)
  first = pos == 0
  last = pos + bq == clen

  for cls in range(cps):
    kbuf[cls, 0:HALF, :] = kp_ref[cls]
    kbuf[cls, HALF:HALF + bq, :] = kc_ref[cls]
    kbuf[cls, HALF + bq:, :] = kn_ref[cls]
    vbuf[cls, 0:HALF, :] = vp_ref[cls]
    vbuf[cls, HALF:HALF + bq, :] = vc_ref[cls]
    vbuf[cls, HALF + bq:, :] = vn_ref[cls]

  r = lax.broadcasted_iota(jnp.int32, (DIL_SUB, DIL_KEYS), 0)
  c = lax.broadcasted_iota(jnp.int32, (DIL_SUB, DIL_KEYS), 1)
  band = jnp.where((c >= r) & (c <= r + 2 * HALF), 0.0, NEG_INF).astype(F32)
  lane = lax.broadcasted_iota(jnp.int32, (DIL_SUB, HEAD_DIM), 1)
  lo = jnp.where(c < HALF, jnp.where(first, NEG_INF, 0.0), 0.0).astype(F32)
  hi = jnp.where(c >= DIL_KEYS - HALF, jnp.where(last, NEG_INF, 0.0), 0.0).astype(F32)
  nsub = bq // DIL_SUB
  biases = [band] * nsub
  biases[0] = biases[0] + lo
  biases[-1] = biases[-1] + hi

  for cls in range(cps):
    for j in range(nsub):
      rs = slice(j * DIL_SUB, (j + 1) * DIL_SUB)
      ks = slice(j * DIL_SUB, j * DIL_SUB + DIL_KEYS)
      stats = jnp.zeros((DIL_SUB, HEAD_DIM), F32)
      for h in range(N_DIL_HEADS):
        cs = slice(h * HEAD_DIM, (h + 1) * HEAD_DIM)
        s = lax.dot_general(q_ref[cls, rs, cs], kbuf[cls, ks, cs], (((1,), (1,)), ((), ())),
                            preferred_element_type=F32) + biases[j]
        m = jnp.max(s, axis=-1, keepdims=True)
        p = jnp.exp2(s - m)
        den = jnp.sum(p, axis=-1, keepdims=True)
        num = jnp.dot(p.astype(BF16), vbuf[cls, ks, cs], preferred_element_type=F32)
        o_ref[cls, rs, cs] = num.astype(o_ref.dtype)
        stats = jnp.where(lane == h, m, jnp.where(lane == DIL_DEN_LANE0 + h, den, stats))
      st_ref[cls, rs, :] = stats


def _dilated_branch(src, *, d, bq, cps, prompt_rows, s_prompt, s_sample):
  rows = src.shape[1]
  hb = bq // HALF
  n_half_blocks = rows // HALF
  body = functools.partial(
      _dil_kernel, class_len_prompt=s_prompt // d, class_len_sample=s_sample // d,
      prompt_rows=prompt_rows // d)

  def cur(c):
    return pl.BlockSpec((cps, bq, DIL_WIDTH), lambda r, i: (r, i, c))

  def prev(c):
    return pl.BlockSpec((cps, HALF, DIL_WIDTH), lambda r, i: (r, jnp.maximum(i * hb - 1, 0), c))

  def nxt(c):
    return pl.BlockSpec((cps, HALF, DIL_WIDTH),
                        lambda r, i: (r, jnp.minimum((i + 1) * hb, n_half_blocks - 1), c))

  return pl.pallas_call(
      body,
      grid=(d // cps, rows // bq),
      in_specs=[cur(0), prev(1), cur(1), nxt(1), prev(2), cur(2), nxt(2)],
      out_specs=[pl.BlockSpec((cps, bq, DIL_WIDTH), lambda r, i: (r, i, 0)),
                 pl.BlockSpec((cps, bq, HEAD_DIM), lambda r, i: (r, i, 0))],
      out_shape=[jax.ShapeDtypeStruct((d, rows, DIL_WIDTH), BF16),
                 jax.ShapeDtypeStruct((d, rows, HEAD_DIM), F32)],
      scratch_shapes=[pltpu.VMEM((cps, bq + 2 * HALF, DIL_WIDTH), BF16),
                      pltpu.VMEM((cps, bq + 2 * HALF, DIL_WIDTH), BF16)],
      compiler_params=_params("parallel", "parallel"),
      name=f"dilated_d{d}",
  )(*([src] * 7))


def _merge_tile(o1_ref, o4_ref, o16_ref, l1_ref, l4_ref, l16_ref, e_ref, gb_ref, b_ref,
                s4, s16, ls4, ls16, bs):
  tm = o1_ref.shape[0]
  for rho in range(D4):
    ls4[pl.ds(rho, tm // D4, stride=D4), :] = l4_ref[rho]
  for r in range(D16):
    ls16[pl.ds(r, tm // D16, stride=D16), :] = l16_ref[r]

  t1, t2, t3 = l1_ref[...], ls4[...], ls16[...]
  m = jnp.maximum(jnp.maximum(t1, t2), t3)
  w1, w2, w3 = jnp.exp2(t1 - m), jnp.exp2(t2 - m), jnp.exp2(t3 - m)
  den_of = lambda t: pltpu.roll(t, HEAD_DIM - DIL_DEN_LANE0, 1)
  total = w1 * den_of(t1) + w2 * den_of(t2) + w3 * den_of(t3)
  head_lane = lax.broadcasted_iota(jnp.int32, (tm, HEAD_DIM), 1) < N_DIL_HEADS
  inv = 1.0 / jnp.where(head_lane, total, 1.0)

  def over_head_lanes(w):
    wn = w * inv
    hi = wn.astype(BF16)
    lo = (wn - hi.astype(F32)).astype(BF16)
    return jnp.dot(jnp.concatenate([hi, lo], axis=1), e_ref[...], preferred_element_type=F32)

  wb1, wb2, wb3 = over_head_lanes(w1), over_head_lanes(w2), over_head_lanes(w3)

  ssq = jnp.zeros((tm, 1), F32)
  for h in range(N_DIL_HEADS):
    cs = slice(h * HEAD_DIM, (h + 1) * HEAD_DIM)
    for rho in range(D4):
      s4.at[h][pl.ds(rho, tm // D4, stride=D4), :] = o4_ref[rho, :, cs].astype(F32)
    stage = bs.at[h]
    for r in range(D16):
      q, rho = divmod(r, D4)
      stage[pl.ds(rho * (tm // D4) + q, tm // D16, stride=D16_PER_D4), :] = (
          o16_ref[r, :, cs].astype(F32))
    for rho in range(D4):
      s16.at[h][pl.ds(rho, tm // D4, stride=D4), :] = stage[rho * (tm // D4):(rho + 1) * (tm // D4), :]
    b = wb1[:, cs] * o1_ref[:, cs].astype(F32) + wb2[:, cs] * s4[h] + wb3[:, cs] * s16[h]
    ssq = ssq + jnp.sum(b * b, axis=-1, keepdims=True)
    bs[h] = b
  scale = jnp.broadcast_to(lax.rsqrt(ssq * (1.0 / DIL_WIDTH) + EPS), (tm, HEAD_DIM))
  for h in range(N_DIL_HEADS):
    cs = slice(h * HEAD_DIM, (h + 1) * HEAD_DIM)
    b_ref[:, cs] = (bs[h] * scale * gb_ref[:, cs]).astype(b_ref.dtype)


def _mix_out_kernel(o1_ref, o4_ref, o16_ref, l1_ref, l4_ref, l16_ref, e_ref, gb_ref,
                    a_ref, c_ref, w_ref, *rest, prompt_tiles):
  if prompt_tiles is None:
    x_ref, y_ref, s4, s16, ls4, ls16, bs, bn = rest
    xs_ref = None
  else:
    x_ref, xs_ref, y_ref, s4, s16, ls4, ls16, bs, bn = rest
  step = pl.program_id(0)
  b0 = SGU_WIDTH
  c0 = SGU_WIDTH + DIL_WIDTH

  def merge_into(slot):
    _merge_tile(o1_ref, o4_ref, o16_ref, l1_ref, l4_ref, l16_ref, e_ref, gb_ref, bn.at[slot],
                s4, s16, ls4, ls16, bs)

  @pl.when(step == 0)
  def _():
    merge_into(0)

  @pl.when(step > 0)
  def _():
    slot = step % 2
    merge_into(slot)
    b_prev = bn.at[1 - slot]
    for chunk in range(D_MODEL // OUT_CHUNK_N):
      cols = slice(chunk * OUT_CHUNK_N, (chunk + 1) * OUT_CHUNK_N)
      acc = jnp.dot(a_ref[...], w_ref[0:b0, cols], preferred_element_type=F32)
      acc += jnp.dot(b_prev[...], w_ref[b0:c0, cols], preferred_element_type=F32)
      acc += jnp.dot(c_ref[...], w_ref[c0:, cols], preferred_element_type=F32)
      res = x_ref[:, cols]
      if xs_ref is not None:
        res = jnp.where(step - 1 < prompt_tiles, res, xs_ref[:, cols])
      y_ref[:, cols] = res + acc


def _mix_out(o_list, lse_list, g_b, a_n, c_n, w_out, layer, x, *, tm):
  o1, o4, o16 = o_list
  l1, l4, l16 = lse_list
  tok = a_n.shape[0]
  n = tok // tm
  head_of_lane = jnp.arange(DIL_WIDTH, dtype=jnp.int32) // HEAD_DIM
  selector = (jnp.arange(HEAD_DIM, dtype=jnp.int32)[:, None] == head_of_lane[None, :]).astype(BF16)
  selector = jnp.concatenate([selector, selector], axis=0)
  head_scratch = pltpu.VMEM((N_DIL_HEADS, tm, HEAD_DIM), F32)
  ahead = lambda s: jnp.minimum(s, n - 1)
  behind = lambda s: jnp.maximum(s - 1, 0)
  const = lambda shape: pl.BlockSpec(shape, lambda s: (0,) * len(shape))
  row = lambda w: pl.BlockSpec((tm, w), lambda s: (behind(s), 0))
  if isinstance(x, tuple):
    prompt_tiles = x[0].shape[0] // tm
    x_specs = [pl.BlockSpec((tm, D_MODEL), lambda s: (jnp.minimum(behind(s), prompt_tiles - 1), 0)),
               pl.BlockSpec((tm, D_MODEL), lambda s: (jnp.maximum(behind(s) - prompt_tiles, 0), 0))]
  else:
    prompt_tiles, x_specs, x = None, [row(D_MODEL)], (x,)
  return pl.pallas_call(
      functools.partial(_mix_out_kernel, prompt_tiles=prompt_tiles),
      grid=(n + 1,),
      in_specs=[pl.BlockSpec((None, tm, DIL_WIDTH), lambda s: (0, ahead(s), 0)),
                pl.BlockSpec((D4, tm // D4, DIL_WIDTH), lambda s: (0, ahead(s), 0)),
                pl.BlockSpec((D16, tm // D16, DIL_WIDTH), lambda s: (0, ahead(s), 0)),
                pl.BlockSpec((None, tm, HEAD_DIM), lambda s: (0, ahead(s), 0)),
                pl.BlockSpec((D4, tm // D4, HEAD_DIM), lambda s: (0, ahead(s), 0)),
                pl.BlockSpec((D16, tm // D16, HEAD_DIM), lambda s: (0, ahead(s), 0)),
                const((2 * HEAD_DIM, DIL_WIDTH)),
                const((1, DIL_WIDTH)),
                row(SGU_WIDTH), row(MEM_WIDTH),
                pl.BlockSpec((None, MIX_WIDTH, D_MODEL), lambda s: (layer, 0, 0),
                             pipeline_mode=pl.Buffered(1))] + x_specs,
      out_specs=row(D_MODEL),
      out_shape=jax.ShapeDtypeStruct((tok, D_MODEL), F32),
      scratch_shapes=[head_scratch, head_scratch,
                      pltpu.VMEM((tm, HEAD_DIM), F32), pltpu.VMEM((tm, HEAD_DIM), F32),
                      head_scratch, pltpu.VMEM((2, tm, DIL_WIDTH), BF16)],
      compiler_params=_params("arbitrary"),
      name="mix_out",
  )(o1, o4, o16, l1, l4, l16, selector, g_b, a_n, c_n, w_out, *x)


def _ffn_kernel(x_ref, g_ref, wg_ref, wu_ref, wd_ref, gf_ref, y_ref, h_ref, *, final_norm):
  k = pl.program_id(1)

  @pl.when(k == 0)
  def _():
    xf = x_ref[...]
    h_ref[...] = (xf * _rms_scale(xf) * g_ref[...]).astype(BF16)
    y_ref[...] = xf

  h = h_ref[...]
  gate = jnp.dot(h, wg_ref[...], preferred_element_type=F32)
  up = jnp.dot(h, wu_ref[...], preferred_element_type=F32)
  act = (jax.nn.silu(gate) * up).astype(BF16)
  y_ref[...] += jnp.dot(act, wd_ref[...], preferred_element_type=F32)

  if final_norm:
    @pl.when(k == pl.num_programs(1) - 1)
    def _():
      y = y_ref[...]
      y_ref[...] = y * _rms_scale(y) * gf_ref[...]


def _ffn(x, g, w_gu, w_down, layer, g_final, *, tm, th, final_norm, row0=0, rows=None):
  tok = x.shape[0] if rows is None else rows
  tile0 = row0 // tm
  nk = FFN_HIDDEN // th
  return pl.pallas_call(
      functools.partial(_ffn_kernel, final_norm=final_norm),
      grid=(tok // tm, nk),
      in_specs=[
          pl.BlockSpec((tm, D_MODEL), lambda i, k: (tile0 + i, 0)),
          pl.BlockSpec((1, D_MODEL), lambda i, k: (0, 0)),
          pl.BlockSpec((None, D_MODEL, th), lambda i, k: (layer, 0, k)),
          pl.BlockSpec((None, D_MODEL, th), lambda i, k: (layer, 0, nk + k)),
          pl.BlockSpec((None, th, D_MODEL), lambda i, k: (layer, k, 0)),
          pl.BlockSpec((1, D_MODEL), lambda i, k: (0, 0)),
      ],
      out_specs=pl.BlockSpec((tm, D_MODEL), lambda i, k: (i, 0)),
      out_shape=jax.ShapeDtypeStruct((tok, D_MODEL), F32),
      scratch_shapes=[pltpu.VMEM((tm, D_MODEL), BF16)],
      compiler_params=_params("parallel", "arbitrary", vmem=BIG_TILE_VMEM_LIMIT),
      name="ffn",
  )(x, g, w_gu, w_gu, w_down, g_final)


def _rope_tables(positions):
  inv = ROPE_THETA ** (-jnp.arange(0, ROPE_DIM, 2, dtype=F32) / ROPE_DIM)
  ang = positions.astype(F32)[:, None] * inv[None, :]
  cos, sin = jnp.cos(ang), jnp.sin(ang)
  n = positions.shape[0]
  rest = HEAD_DIM - ROPE_DIM
  c = jnp.concatenate([cos, cos, jnp.ones((n, rest), F32)], axis=1)
  s1 = jnp.concatenate([jnp.zeros((n, ROPE_HALF), F32), sin, jnp.zeros((n, rest), F32)], axis=1)
  s2 = jnp.concatenate([-sin, jnp.zeros((n, HEAD_DIM - ROPE_HALF), F32)], axis=1)
  return c, s1, s2


def kernel(x_prompt, x_sample, mem_prompt, mem_sample, g_mix_norm, w_in, g_sgu, w_spatial,
           b_spatial, g_mem_norm, w_mem_kv, g_group_out, w_out, g_ffn_norm, w_gate_up,
           w_down, g_final):
  n_prompt, s_prompt, _ = x_prompt.shape
  n_sample, s_sample, _ = x_sample.shape
  assert n_prompt == 1 and s_prompt % s_sample == 0
  assert s_sample % (D16 * DIL_SUB) == 0 and s_sample % TM_FFN == 0
  prompt_rows = n_prompt * s_prompt
  depth = w_in.shape[0]

  x = (x_prompt.reshape(prompt_rows, D_MODEL), x_sample.reshape(n_sample * s_sample, D_MODEL))
  mem = jnp.concatenate([mem_prompt, mem_sample], axis=0)
  n_mems = mem.shape[0]
  mem = mem.reshape(n_mems * N_MEM, D_MODEL)
  rope = _rope_tables(jnp.arange(s_prompt, dtype=jnp.int32))
  seqs = dict(prompt_rows=prompt_rows, s_prompt=s_prompt, s_sample=s_sample)
  w_in, w_spatial, w_mem_kv, w_out, w_gate_up, w_down = (
      w.astype(BF16) for w in (w_in, w_spatial, w_mem_kv, w_out, w_gate_up, w_down))

  kv = _norm_matmul_layers(mem, g_mem_norm, w_mem_kv, tm=n_mems * N_MEM, tn=2 * MEM_WIDTH,
                           name="mem_kv")
  kv = kv.reshape(depth, n_mems, N_MEM, 2 * MEM_WIDTH)

  for l in range(depth):
    g_out = g_group_out[l].reshape(1, MIX_WIDTH)
    b_sp = jnp.broadcast_to(b_spatial[l][:, :, None], (N_SGU_GROUPS, SGU_CHUNK, HEAD_DIM))
    zq, c4, c16, a_n, c_n = _in_proj(
        x, g_mix_norm[l], w_in, l, rope, kv, g_sgu[l].reshape(1, SGU_WIDTH), w_spatial, b_sp,
        g_out, tm=TM_IN_PROJ, rows_per_mem=s_sample,
        first_sample_mem_tile=prompt_rows // s_sample)
    o_list, lse_list = [], []
    for d, src in ((1, zq.reshape(1, *zq.shape)), (D4, c4), (D16, c16)):
      bq = min(DIL_ROWS_PER_STEP, s_sample // d)
      o, lse = _dilated_branch(src, d=d, bq=bq, cps=min(d, DIL_ROWS_PER_STEP // bq), **seqs)
      o_list.append(o)
      lse_list.append(lse)
    x = _mix_out(o_list, lse_list, g_out[:, SGU_WIDTH:SGU_WIDTH + DIL_WIDTH], a_n, c_n, w_out, l, x,
                 tm=TM_MIX_OUT)
    ffn = functools.partial(_ffn, x, g_ffn_norm[l].reshape(1, D_MODEL), w_gate_up, w_down, l,
                            g_final.reshape(1, D_MODEL), tm=TM_FFN, th=TH_FFN)
    if l < depth - 1:
      x = ffn(final_norm=False)

  y_prompt = ffn(final_norm=True, row0=0, rows=prompt_rows)
  y_sample = ffn(final_norm=True, row0=prompt_rows, rows=n_sample * s_sample)
  return (y_prompt.reshape(n_prompt, s_prompt, D_MODEL),
          y_sample.reshape(n_sample, s_sample, D_MODEL))
```
